```python
import math
import jax, jax.numpy as jnp
from jax import lax
import numpy as np

D_MODEL = 1024
BATCH = 2
SEQ = 8192
DEPTH = 1

GRID_W = 64
CTX_LEN = 256
ATTN_WIDTH = 512
HGRN_WIDTH = 512
D_MIX = ATTN_WIDTH + HGRN_WIDTH
ATTN_HEADS = 4
DIFF_HEAD_DIM = 64
HGRN_HEADS = 4
HGRN_KEY_DIM = HGRN_WIDTH // HGRN_HEADS
HGRN_VAL_DIM = HGRN_WIDTH // HGRN_HEADS
CHUNK = 64
Q_BLOCK = 128
ROPE_BASE = 10000.0
ROPE_FREQS = DIFF_HEAD_DIM // 4
EPS = 1e-5
IN_WIDTHS = (ATTN_WIDTH, ATTN_WIDTH, ATTN_WIDTH, ATTN_WIDTH,
             HGRN_WIDTH, HGRN_WIDTH, HGRN_WIDTH, HGRN_WIDTH, HGRN_WIDTH)
IN_COLS = sum(IN_WIDTHS)

kernel_name = "hymba_diffattn_hgrn2_prefix_ctx_block"


def _layer_norm(x, g, b):
    xf = x.astype(jnp.float32)
    mu = jnp.mean(xf, axis=-1, keepdims=True)
    var = jnp.mean(jnp.square(xf - mu), axis=-1, keepdims=True)
    return ((xf - mu) * lax.rsqrt(var + EPS) * g.astype(jnp.float32) + b.astype(jnp.float32)).astype(x.dtype)


def _rms_norm(x, g):
    xf = x.astype(jnp.float32)
    return xf * lax.rsqrt(jnp.mean(xf * xf, axis=-1, keepdims=True) + EPS) * g.astype(jnp.float32)


def _axial_rope_tables(n_tokens):
    rows = n_tokens // GRID_W
    r, cidx = jnp.meshgrid(jnp.arange(rows, dtype=jnp.float32),
                           jnp.arange(GRID_W, dtype=jnp.float32), indexing="ij")
    pos = jnp.stack([r.reshape(-1), cidx.reshape(-1)], axis=-1)
    inv_freq = ROPE_BASE ** (-jnp.arange(ROPE_FREQS, dtype=jnp.float32) / ROPE_FREQS)
    ang = pos[:, :, None] * inv_freq
    ang = jnp.stack([ang, ang], axis=2).reshape(n_tokens, DIFF_HEAD_DIM)
    return jnp.cos(ang), jnp.sin(ang)


def _apply_rope(x, cos, sin):
    xs = x.reshape(x.shape[:-1] + (2, 2, ROPE_FREQS))
    rot = jnp.concatenate([-xs[..., 1:, :], xs[..., :1, :]], axis=-2).reshape(x.shape)
    c = cos[None, :, None, None, :].astype(x.dtype)
    s = sin[None, :, None, None, :].astype(x.dtype)
    return x * c + rot * s


def _attn_heads(aq, ak, av, rope=None):
    B, T, _ = aq.shape
    q = aq.reshape(B, T, ATTN_HEADS, 2, DIFF_HEAD_DIM)
    k = ak.reshape(B, T, ATTN_HEADS, 2, DIFF_HEAD_DIM)
    if rope is not None:
        q = _apply_rope(q, *rope)
        k = _apply_rope(k, *rope)
    v = av.reshape(B, T, ATTN_HEADS, 2 * DIFF_HEAD_DIM)
    return q.transpose(0, 2, 3, 1, 4), k.transpose(0, 2, 3, 1, 4), v.transpose(0, 2, 1, 3)


def _diff_attn(q, k, v, lam):
    s = jnp.einsum("bhmqd,bhmkd->bhmqk", q, k).astype(jnp.float32) * (1.0 / math.sqrt(DIFF_HEAD_DIM))
    p = jax.nn.softmax(s, axis=-1)
    w = p[:, :, 0] - lam * p[:, :, 1]
    return jnp.einsum("bhqk,bhkv->bhqv", w.astype(v.dtype), v)


def _attn_readout(o, gain, lambda_init, dtype):
    o = _rms_norm(o, gain) * (1.0 - lambda_init)
    B, H, T, dv = o.shape
    return o.transpose(0, 2, 1, 3).reshape(B, T, H * dv).astype(dtype)


def _to_heads(a, dh):
    B, T, _ = a.shape
    return a.reshape(B, T, -1, dh).transpose(0, 2, 1, 3)


def _hgrn_gates(z, lb):
    zf = z.astype(jnp.float32)
    log_f = jnp.log(lb + (1.0 - lb) * jax.nn.sigmoid(zf))
    k = (1.0 - lb) * jax.nn.sigmoid(-zf)
    return _to_heads(k, HGRN_KEY_DIM), _to_heads(log_f, HGRN_KEY_DIM)


def _chunk_scan(q, k, v, log_f, s0):
    B, H, T, DK = q.shape
    n_chunks = T // CHUNK

    def to_chunks(a):
        return a.reshape(B, H, n_chunks, CHUNK, a.shape[-1]).transpose(2, 0, 1, 3, 4)

    incl = jnp.tril(jnp.ones((CHUNK, CHUNK), dtype=bool))

    def step(s, inp):
        qc, kc, vc, gc = inp
        b = jnp.cumsum(gc, axis=2)
        rel = b[:, :, :, None, :] - b[:, :, None, :, :]
        decay = jnp.exp(jnp.where(incl[:, :, None], rel, -jnp.inf))
        scores = jnp.einsum("bhtk,bhsk,bhtsk->bhts", qc, kc, decay)
        o = (jnp.einsum("bhts,bhsv->bhtv", scores, vc)
             + jnp.einsum("bhtk,bhkv->bhtv", qc * jnp.exp(b), s))
        b_last = b[:, :, -1:, :]
        s_new = (jnp.exp(b_last[:, :, 0, :])[..., None] * s
                 + jnp.einsum("bhsk,bhsv->bhkv", kc * jnp.exp(b_last - b), vc))
        return s_new, o

    s_final, o = lax.scan(step, s0, (to_chunks(q), to_chunks(k), to_chunks(v), to_chunks(log_f)))
    o = o.transpose(1, 2, 0, 3, 4).reshape(B, H, T, v.shape[-1])
    return o, s_final


def _hgrn2_bidir(hq, hi, hf_fwd, hf_bwd, lb_fwd, lb_bwd, s0_fwd, s0_bwd):
    q = _to_heads(jax.nn.silu(hq.astype(jnp.float32)), HGRN_KEY_DIM)
    v = _to_heads(hi.astype(jnp.float32), HGRN_VAL_DIM)
    k_f, g_f = _hgrn_gates(hf_fwd, lb_fwd)
    k_b, g_b = _hgrn_gates(hf_bwd, lb_bwd)
    o_f, s_f = _chunk_scan(q, k_f, v, g_f, s0_fwd)
    flip = lambda a: jnp.flip(a, axis=2)
    o_b, s_b = _chunk_scan(flip(q), flip(k_b), flip(v), flip(g_b), s0_bwd)
    return o_f + flip(o_b), s_f, s_b


def _hgrn_readout(o, gain, dtype):
    o = _rms_norm(o, gain)
    B, H, T, dv = o.shape
    return o.transpose(0, 2, 1, 3).reshape(B, T, H * dv).astype(dtype)


def _project(h, w):
    p = h @ w
    idx = list(np.cumsum(IN_WIDTHS)[:-1])
    return jnp.split(p, idx, axis=-1)


def _merge(attn_o, ag, hgrn_o, hg, w_out):
    y = jnp.concatenate([attn_o * jax.nn.silu(ag), hgrn_o * jax.nn.silu(hg)], axis=-1)
    return y @ w_out


def setup_inputs(seed: int = 0) -> dict:
    key = jax.random.key(seed)
    ks = jax.random.split(key, 14)
    beta = (8.0 * DEPTH) ** -0.25
    f32 = jnp.float32
    return {
        "x": jax.random.normal(ks[0], (BATCH, SEQ, D_MODEL), f32),
        "c": jax.random.normal(ks[1], (BATCH, D_MODEL), f32),
        "ctx": jax.random.normal(ks[2], (BATCH, CTX_LEN, D_MODEL), f32),
        "c_ctx": jax.random.normal(ks[3], (D_MODEL,), f32),
        "w_ada": jax.random.normal(ks[4], (DEPTH, D_MODEL, 3 * D_MODEL), f32) * (0.5 * D_MODEL ** -0.5),
        "b_ada": jax.random.normal(ks[5], (DEPTH, 3 * D_MODEL), f32) * 0.02,
        "w_in": jax.random.normal(ks[6], (DEPTH, D_MODEL, IN_COLS), f32) * D_MODEL ** -0.5,
        "w_out": jax.random.normal(ks[7], (DEPTH, D_MIX, D_MODEL), f32) * (D_MIX ** -0.5 * beta),
        "diff_lambda": jax.random.normal(ks[8], (DEPTH, 4, DIFF_HEAD_DIM), f32) * 0.1,
        "diff_subln_gain": 1.0 + 0.02 * jax.random.normal(ks[9], (DEPTH, 2 * DIFF_HEAD_DIM), f32),
        "hgrn_lower_bound": jax.random.normal(ks[10], (2, DEPTH + 1, HGRN_WIDTH), f32) * 0.1,
        "hgrn_norm_gain": 1.0 + 0.02 * jax.random.normal(ks[11], (DEPTH, HGRN_VAL_DIM), f32),
        "ln_gain": 1.0 + 0.02 * jax.random.normal(ks[12], (DEPTH, D_MODEL), f32),
        "ln_bias": 0.02 * jax.random.normal(ks[13], (DEPTH, D_MODEL), f32),
    }


def reference(x, c, ctx, c_ctx, w_ada, b_ada, w_in, w_out, diff_lambda, diff_subln_gain,
              hgrn_lower_bound, hgrn_norm_gain, ln_gain, ln_bias):
    B, N, D = x.shape
    alpha = (2.0 * DEPTH) ** 0.25
    rope = _axial_rope_tables(N)
    lower_bounds = jnp.cumsum(jax.nn.softmax(hgrn_lower_bound.astype(jnp.float32), axis=1), axis=1)
    n_blk = N // Q_BLOCK

    for layer in range(DEPTH):
        mod = jax.nn.silu(c) @ w_ada[layer] + b_ada[layer]
        shift, scale, gate = jnp.split(mod, 3, axis=-1)
        mod_c = jax.nn.silu(c_ctx) @ w_ada[layer] + b_ada[layer]
        shift_c, scale_c, gate_c = jnp.split(mod_c, 3, axis=-1)
        h_lat = x * (1.0 + scale[:, None, :]) + shift[:, None, :]
        h_ctx = ctx * (1.0 + scale_c) + shift_c

        aq_l, ak_l, av_l, ag_l, hq_l, hi_l, hff_l, hfb_l, hg_l = _project(h_lat, w_in[layer])
        aq_c, ak_c, av_c, ag_c, hq_c, hi_c, hff_c, hfb_c, hg_c = _project(h_ctx, w_in[layer])

        lambda_init = 0.8 - 0.6 * math.exp(-0.3 * layer)
        lp = diff_lambda[layer].astype(jnp.float32)
        lam = jnp.exp(jnp.sum(lp[0] * lp[1])) - jnp.exp(jnp.sum(lp[2] * lp[3])) + lambda_init
        q_c, k_c, v_c = _attn_heads(aq_c, ak_c, av_c)
        q_l, k_l, v_l = _attn_heads(aq_l, ak_l, av_l, rope)
        k_all = jnp.concatenate([k_c, k_l], axis=3)
        v_all = jnp.concatenate([v_c, v_l], axis=2)
        q_blocks = jnp.moveaxis(q_l.reshape(B, ATTN_HEADS, 2, n_blk, Q_BLOCK, DIFF_HEAD_DIM), 3, 0)
        o_blocks = lax.map(lambda qb: _diff_attn(qb, k_all, v_all, lam), q_blocks)
        o_att_l = jnp.moveaxis(o_blocks, 0, 2).reshape(B, ATTN_HEADS, N, 2 * DIFF_HEAD_DIM)
        attn_lat = _attn_readout(o_att_l, diff_subln_gain[layer], lambda_init, x.dtype)

        lb_f, lb_b = lower_bounds[0, layer], lower_bounds[1, layer]
        s_zero = jnp.zeros((B, HGRN_HEADS, HGRN_KEY_DIM, HGRN_VAL_DIM), jnp.float32)
        o_hg_c, s_ctx_f, s_ctx_b = _hgrn2_bidir(hq_c, hi_c, hff_c, hfb_c, lb_f, lb_b, s_zero, s_zero)
        o_hg_l, _, _ = _hgrn2_bidir(hq_l, hi_l, hff_l, hfb_l, lb_f, lb_b, s_ctx_f, s_ctx_b)
        hgrn_lat = _hgrn_readout(o_hg_l, hgrn_norm_gain[layer], x.dtype)

        y_lat = _merge(attn_lat, ag_l, hgrn_lat, hg_l, w_out[layer])

        if layer < DEPTH - 1:
            attn_ctx = _attn_readout(_diff_attn(q_c, k_c, v_c, lam), diff_subln_gain[layer],
                                     lambda_init, ctx.dtype)
            hgrn_ctx = _hgrn_readout(o_hg_c, hgrn_norm_gain[layer], ctx.dtype)
            y_ctx = _merge(attn_ctx, ag_c, hgrn_ctx, hg_c, w_out[layer])
            ctx = _layer_norm(alpha * ctx + gate_c * y_ctx, ln_gain[layer], ln_bias[layer])

        x = _layer_norm(alpha * x + gate[:, None, :] * y_lat, ln_gain[layer], ln_bias[layer])
    return x
```

```python
import functools
import math

import numpy as np
import jax
import jax.numpy as jnp
from jax import lax
from jax.experimental import pallas as pl
from jax.experimental.pallas import tpu as pltpu

D_MODEL = 1024
DEPTH = 1
GRID_W = 64
HEADS = 4
HEAD_W = 128
MAP_D = 64
BRANCH_W = HEADS * HEAD_W
N_GROUPS = 9
ROPE_BASE = 10000.0
ROPE_FREQS = MAP_D // 4
EPS = 1e-5
LAMBDA_INIT = 0.8 - 0.6 * math.exp(-0.3 * 0)
ALPHA = (2.0 * DEPTH) ** 0.25

V7X_VMEM_LIMIT_BYTES = 56 * 1024 * 1024
SUBLANES = 8
HGRN_CHUNK = 128
HGRN_LEVELS = 7

_NT = (((1,), (1,)), ((), ()))
_TN = (((0,), (0,)), ((), ()))


def _silu(x):
    return x * jax.nn.sigmoid(x)


def _mod_kernel(c_ref, w_ref, b_ref, o_ref):
    a = _silu(c_ref[...])
    o_ref[...] = jnp.dot(a, w_ref[...], preferred_element_type=jnp.float32,
                         precision=lax.Precision.HIGHEST) + b_ref[...]


def _modulation(cvec, w_ada, b_ada):
    rows, d = cvec.shape
    n_out = w_ada.shape[1]
    bn = 1024
    return pl.pallas_call(
        _mod_kernel,
        grid=(n_out // bn,),
        in_specs=[pl.BlockSpec((rows, d), lambda j: (0, 0)),
                  pl.BlockSpec((d, bn), lambda j: (0, j)),
                  pl.BlockSpec((1, bn), lambda j: (0, j))],
        out_specs=pl.BlockSpec((rows, bn), lambda j: (0, j)),
        out_shape=jax.ShapeDtypeStruct((rows, n_out), jnp.float32),
        compiler_params=pltpu.CompilerParams(vmem_limit_bytes=V7X_VMEM_LIMIT_BYTES),
        name="modulation",
    )(cvec, w_ada, b_ada)


_LATENT_GROUPS = ((0, "rope_q", jnp.bfloat16), (1, "rope", jnp.bfloat16), (2, "plain", jnp.bfloat16),
                  (3, "plain", jnp.float32), (4, "silu", jnp.float32), (5, "plain", jnp.bfloat16),
                  (6, "plain", jnp.float32), (7, "plain", jnp.float32), (8, "plain", jnp.float32))
_CTX_GROUPS = ((1, "plain", jnp.bfloat16), (2, "plain", jnp.bfloat16), (4, "silu", jnp.float32),
               (5, "plain", jnp.bfloat16), (6, "plain", jnp.float32), (7, "plain", jnp.float32))


def _proj_kernel(groups, use_rope, *refs):
    if use_rope:
        x_ref, s1p_ref, sh_ref, w_ref, cos_ref, sin_ref = refs[:6]
        out_refs = refs[6:]
    else:
        x_ref, s1p_ref, sh_ref, w_ref = refs[:4]
        out_refs = refs[4:]
    h = (x_ref[0] * s1p_ref[0] + sh_ref[0]).astype(jnp.bfloat16)
    if use_rope:
        cos = jnp.concatenate([cos_ref[...]] * HEADS, axis=1)
        sin = jnp.concatenate([sin_ref[...]] * HEADS, axis=1)
        lane = lax.broadcasted_iota(jnp.int32, cos.shape, 1)
        first_half = (lane & (2 * ROPE_FREQS - 1)) < ROPE_FREQS
    for (g, kind, dt), o_ref in zip(groups, out_refs):
        p = jnp.dot(h, w_ref[:, g * BRANCH_W:(g + 1) * BRANCH_W], preferred_element_type=jnp.float32)
        if kind in ("rope", "rope_q"):
            partner = jnp.where(first_half,
                                pltpu.roll(p, BRANCH_W - ROPE_FREQS, axis=1),
                                pltpu.roll(p, ROPE_FREQS, axis=1))
            p = p * cos + partner * sin
            if kind == "rope_q":
                p = p * (1.0 / math.sqrt(MAP_D))
        elif kind == "silu":
            p = _silu(p)
        o_ref[0] = p.astype(dt)


def _in_projection(x, s1p, sh, w_bf, groups, tm, rope_tables=None):
    B, T, D = x.shape
    use_rope = rope_tables is not None
    in_specs = [pl.BlockSpec((1, tm, D), lambda b, i: (b, i, 0)),
                pl.BlockSpec((1, 1, D), lambda b, i: (b, 0, 0)),
                pl.BlockSpec((1, 1, D), lambda b, i: (b, 0, 0)),
                pl.BlockSpec(w_bf.shape, lambda b, i: (0, 0), pipeline_mode=pl.Buffered(1))]
    args = [x, s1p, sh, w_bf]
    if use_rope:
        in_specs += [pl.BlockSpec((tm, HEAD_W), lambda b, i: (i, 0))] * 2
        args += list(rope_tables)
    out_specs = [pl.BlockSpec((1, tm, BRANCH_W), lambda b, i: (b, i, 0)) for _ in groups]
    out_shape = [jax.ShapeDtypeStruct((B, T, BRANCH_W), dt) for _, _, dt in groups]
    return pl.pallas_call(
        functools.partial(_proj_kernel, groups, use_rope),
        grid=(B, T // tm),
        in_specs=in_specs, out_specs=out_specs, out_shape=out_shape,
        compiler_params=pltpu.CompilerParams(
            dimension_semantics=("parallel", "parallel"), vmem_limit_bytes=V7X_VMEM_LIMIT_BYTES),
        name="in_proj_rope" if use_rope else "in_proj_ctx",
    )(*args)


def _rope_tables(n_tokens):
    t = np.arange(n_tokens)
    pos = np.stack([t // GRID_W, t % GRID_W], axis=-1).astype(np.float32)
    inv_freq = jnp.asarray(ROPE_BASE, jnp.float32) ** (-jnp.arange(ROPE_FREQS, dtype=jnp.float32) / ROPE_FREQS)
    ang = jnp.asarray(pos)[:, :, None] * inv_freq
    ang = jnp.stack([ang, ang], axis=2).reshape(n_tokens, MAP_D)
    sign = np.where((np.arange(MAP_D) % (2 * ROPE_FREQS)) < ROPE_FREQS, -1.0, 1.0).astype(np.float32)
    cos = jnp.cos(ang)
    sin = jnp.sin(ang) * sign
    return jnp.concatenate([cos, cos], axis=1), jnp.concatenate([sin, sin], axis=1)


def _attn_kernel(tk, q_ref, kc_ref, vc_ref, k_ref, v_ref, lam_ref, gain_ref, o_ref):
    q = q_ref[0]
    lane = lax.broadcasted_iota(jnp.int32, q.shape, 1)
    zero = jnp.zeros_like(q)
    qm = (jnp.where(lane < MAP_D, q, zero), jnp.where(lane >= MAP_D, q, zero))

    def step(kb, vb, carry):
        new = []
        for m in range(2):
            mx, l, acc = carry[m]
            s = lax.dot_general(qm[m], kb, _NT, preferred_element_type=jnp.float32)
            mx_new = jnp.maximum(mx, jnp.max(s, axis=-1, keepdims=True))
            corr = jnp.exp(mx - mx_new)
            p = jnp.exp(s - mx_new)
            l = l * corr + jnp.sum(p, axis=-1, keepdims=True)
            acc = acc * corr + jnp.dot(p.astype(jnp.bfloat16), vb, preferred_element_type=jnp.float32)
            new.append((mx_new, l, acc))
        return tuple(new)

    tq = q.shape[0]
    init = tuple((jnp.full((tq, 1), -jnp.inf, jnp.float32), jnp.zeros((tq, 1), jnp.float32),
                  jnp.zeros((tq, HEAD_W), jnp.float32)) for _ in range(2))
    carry = step(kc_ref[0], vc_ref[0], init)

    def body(j, carry):
        start = pl.multiple_of(j * tk, tk)
        return step(k_ref[0, pl.ds(start, tk), :], v_ref[0, pl.ds(start, tk), :], carry)

    carry = lax.fori_loop(0, k_ref.shape[1] // tk, body, carry)

    lp = lam_ref[0]
    lam = (jnp.exp(jnp.sum(lp[0:1] * lp[1:2], axis=-1, keepdims=True))
           - jnp.exp(jnp.sum(lp[2:3] * lp[3:4], axis=-1, keepdims=True)) + LAMBDA_INIT)
    (_, l0, a0), (_, l1, a1) = carry
    o = a0 / l0 - lam * (a1 / l1)
    o = o * lax.rsqrt(jnp.mean(o * o, axis=-1, keepdims=True) + EPS)
    o_ref[0] = o * gain_ref[...] * (1.0 - LAMBDA_INIT)


def _diff_attention(q, k_ctx, v_ctx, k, v, diff_lambda, gain, tq, tk):
    B, N, _ = q.shape
    T_ctx = k_ctx.shape[1]
    return pl.pallas_call(
        functools.partial(_attn_kernel, tk),
        grid=(B, HEADS, N // tq),
        in_specs=[pl.BlockSpec((1, tq, HEAD_W), lambda b, h, i: (b, i, h)),
                  pl.BlockSpec((1, T_ctx, HEAD_W), lambda b, h, i: (b, 0, h)),
                  pl.BlockSpec((1, T_ctx, HEAD_W), lambda b, h, i: (b, 0, h)),
                  pl.BlockSpec((1, N, HEAD_W), lambda b, h, i: (b, 0, h)),
                  pl.BlockSpec((1, N, HEAD_W), lambda b, h, i: (b, 0, h)),
                  pl.BlockSpec((1, 4, MAP_D), lambda b, h, i: (0, 0, 0)),
                  pl.BlockSpec((1, HEAD_W), lambda b, h, i: (0, 0))],
        out_specs=pl.BlockSpec((1, tq, HEAD_W), lambda b, h, i: (b, i, h)),
        out_shape=jax.ShapeDtypeStruct((B, N, BRANCH_W), jnp.float32),
        compiler_params=pltpu.CompilerParams(
            dimension_semantics=("parallel", "parallel", "parallel"),
            vmem_limit_bytes=V7X_VMEM_LIMIT_BYTES),
        name="diff_attention",
    )(q, k_ctx, v_ctx, k, v, diff_lambda, gain)


def _level_maps():
    t = np.arange(HGRN_CHUNK)[:, None]
    s = np.arange(HGRN_CHUNK)[None, :]
    x = t ^ s
    lvl = np.where(x > 0, np.floor(np.log2(np.maximum(x, 1))).astype(np.int32) + 1, 0)
    lvl = np.where(s > t, -1, lvl).astype(np.int32)
    return np.stack([lvl, lvl.T])


def _shift_down(x, s):
    return pltpu.roll(x, s, axis=0)


def _shift_up(x, s):
    return pltpu.roll(x, HGRN_CHUNK - s, axis=0)


def _hgrn_chain(q, z, v_bf, lb, st, lvl, reverse):
    C = HGRN_CHUNK
    sig = jax.nn.sigmoid(z)
    g = jnp.log(lb + (1.0 - lb) * sig)
    kk = (1.0 - lb) * jax.nn.sigmoid(-z)
    pos = lax.broadcasted_iota(jnp.int32, (C, HEAD_W), 0)

    b = g
    s = 1
    while s < C:
        if reverse:
            b = b + jnp.where(pos < C - s, _shift_up(b, s), 0.0)
        else:
            b = b + jnp.where(pos >= s, _shift_down(b, s), 0.0)
        s *= 2

    q_bf = q.astype(jnp.bfloat16)
    a = jnp.where(lvl == 0, lax.dot_general(q_bf, kk.astype(jnp.bfloat16), _NT,
                                            preferred_element_type=jnp.float32), 0.0)
    own = b
    nbr = _shift_up(b, 1) if reverse else _shift_down(b, 1)
    for j in range(1, HGRN_LEVELS + 1):
        h = 1 << (j - 1)
        upper = (pos & h) != 0
        ref = jnp.where(upper, own, nbr) if reverse else jnp.where(upper, nbr, own)
        w = jnp.exp(-jnp.abs(b - ref))
        pm = lax.dot_general((q * w).astype(jnp.bfloat16), (kk * w).astype(jnp.bfloat16), _NT,
                             preferred_element_type=jnp.float32)
        a = jnp.where(lvl == j, pm, a)
        if j < HGRN_LEVELS:
            if reverse:
                own = jnp.where(upper, _shift_down(own, h), own)
                nbr = jnp.where(upper, nbr, _shift_up(nbr, h))
            else:
                own = jnp.where(upper, own, _shift_up(own, h))
                nbr = jnp.where(upper, _shift_down(nbr, h), nbr)

    o = jnp.dot(a.astype(jnp.bfloat16), v_bf, preferred_element_type=jnp.float32)
    o = o + lax.dot_general((q * jnp.exp(b)).astype(jnp.bfloat16), st.astype(jnp.bfloat16), _NT,
                            preferred_element_type=jnp.float32)
    b_tot = b[0:1, :] if reverse else b[C - 1:C, :]
    k_hat = (kk * jnp.exp(b_tot - b)).astype(jnp.bfloat16)
    st_new = st * jnp.exp(b_tot) + lax.dot_general(v_bf, k_hat, _TN, preferred_element_type=jnp.float32)
    return o, st_new


def _hgrn_kernel(qf_ref, vf_ref, zf_ref, qb_ref, vb_ref, zb_ref, lbp_ref, lvl_ref, s0_ref,
                 of_ref, ob_ref, sfin_ref, st_ref):
    i = pl.program_id(1)

    @pl.when(i == 0)
    def _():
        st_ref[...] = s0_ref[0]

    for d, (q_ref, v_ref, z_ref, o_ref) in enumerate(((qf_ref, vf_ref, zf_ref, of_ref),
                                                      (qb_ref, vb_ref, zb_ref, ob_ref))):
        lvl = lvl_ref[d]
        for hh in range(HEADS):
            cols = slice(hh * HEAD_W, (hh + 1) * HEAD_W)
            p0 = lbp_ref[d, 0:1, cols]
            p1 = lbp_ref[d, 1:2, cols]
            pm = jnp.maximum(p0, p1)
            e0 = jnp.exp(p0 - pm)
            lb = e0 / (e0 + jnp.exp(p1 - pm))
            o, st_new = _hgrn_chain(q_ref[0, :, cols], z_ref[0, :, cols], v_ref[0, :, cols], lb,
                                    st_ref[d, hh], lvl, reverse=(d == 1))
            o_ref[0, :, cols] = o
            st_ref[d, hh] = st_new

    @pl.when(i == pl.num_programs(1) - 1)
    def _():
        sfin_ref[0] = st_ref[...]


def _hgrn2_bidir(hq, hi, hff, hfb, lb_param, s0):
    B, T, _ = hq.shape
    C = HGRN_CHUNK
    nc = T // C
    fwd = pl.BlockSpec((1, C, BRANCH_W), lambda b, i: (b, i, 0))
    bwd = pl.BlockSpec((1, C, BRANCH_W), lambda b, i: (b, nc - 1 - i, 0))
    st_spec = pl.BlockSpec((1, 2, HEADS, HEAD_W, HEAD_W), lambda b, i: (b, 0, 0, 0, 0))
    lvl = jnp.asarray(_level_maps())
    return pl.pallas_call(
        _hgrn_kernel,
        grid=(B, nc),
        in_specs=[fwd, fwd, fwd, bwd, bwd, bwd,
                  pl.BlockSpec(lb_param.shape, lambda b, i: (0, 0, 0)),
                  pl.BlockSpec(lvl.shape, lambda b, i: (0, 0, 0)),
                  st_spec],
        out_specs=[fwd, bwd, st_spec],
        out_shape=[jax.ShapeDtypeStruct((B, T, BRANCH_W), jnp.float32),
                   jax.ShapeDtypeStruct((B, T, BRANCH_W), jnp.float32),
                   jax.ShapeDtypeStruct((B, 2, HEADS, HEAD_W, HEAD_W), jnp.float32)],
        scratch_shapes=[pltpu.VMEM((2, HEADS, HEAD_W, HEAD_W), jnp.float32)],
        compiler_params=pltpu.CompilerParams(
            dimension_semantics=("parallel", "arbitrary"), vmem_limit_bytes=V7X_VMEM_LIMIT_BYTES),
        name="hgrn2_bidir",
    )(hq, hi, hff, hq, hi, hfb, lb_param, lvl, s0)


def _merge_kernel(att_ref, ag_ref, of_ref, ob_ref, hg_ref, x_ref, gate_ref, w_ref, hgain_ref, lng_ref, lnb_ref,
                  o_ref):
    att = att_ref[0] * _silu(ag_ref[0])
    o = of_ref[0] + ob_ref[0]
    parts = []
    for hh in range(HEADS):
        oh = o[:, hh * HEAD_W:(hh + 1) * HEAD_W]
        parts.append(oh * lax.rsqrt(jnp.mean(oh * oh, axis=-1, keepdims=True) + EPS) * hgain_ref[...])
    hg = jnp.concatenate(parts, axis=1) * _silu(hg_ref[0])
    y_in = jnp.concatenate([att, hg], axis=1).astype(jnp.bfloat16)
    y = jnp.dot(y_in, w_ref[...], preferred_element_type=jnp.float32)
    u = ALPHA * x_ref[0] + gate_ref[0] * y
    mu = jnp.mean(u, axis=-1, keepdims=True)
    uc = u - mu
    var = jnp.mean(uc * uc, axis=-1, keepdims=True)
    o_ref[0] = uc * lax.rsqrt(var + EPS) * lng_ref[...] + lnb_ref[...]


def _merge(att, ag, o_f, o_b, hg, x, gate, w_out_bf, hgain, ln_g, ln_b, tm):
    B, N, D = x.shape
    half = pl.BlockSpec((1, tm, BRANCH_W), lambda b, i: (b, i, 0))
    full = pl.BlockSpec((1, tm, D), lambda b, i: (b, i, 0))
    row = lambda w: pl.BlockSpec((1, w), lambda b, i: (0, 0))
    return pl.pallas_call(
        _merge_kernel,
        grid=(B, N // tm),
        in_specs=[half, half, half, half, half, full,
                  pl.BlockSpec((1, 1, D), lambda b, i: (b, 0, 0)),
                  pl.BlockSpec(w_out_bf.shape, lambda b, i: (0, 0)),
                  row(HEAD_W), row(D), row(D)],
        out_specs=full,
        out_shape=jax.ShapeDtypeStruct((B, N, D), jnp.float32),
        compiler_params=pltpu.CompilerParams(
            dimension_semantics=("parallel", "parallel"), vmem_limit_bytes=V7X_VMEM_LIMIT_BYTES),
        name="merge_out_proj_ln",
    )(att, ag, o_f, o_b, hg, x, gate, w_out_bf, hgain, ln_g, ln_b)


def kernel(x, c, ctx, c_ctx, w_ada, b_ada, w_in, w_out, diff_lambda, diff_subln_gain, hgrn_lower_bound,
           hgrn_norm_gain, ln_gain, ln_bias):
    B, N, D = x.shape
    assert DEPTH == 1 and w_ada.shape[0] == 1
    cvec = jnp.concatenate([c, c_ctx[None, :], jnp.zeros((SUBLANES - B - 1, D), c.dtype)], axis=0)
    mod = _modulation(cvec, w_ada[0], b_ada[0][None, :])
    shift, scale, gate = mod[:, :D], mod[:, D:2 * D], mod[:, 2 * D:]
    s1p = (1.0 + scale)[:, None, :]
    shift = shift[:, None, :]
    ctx_rows = jnp.full((B,), B, jnp.int32)

    w_bf = w_in[0].astype(jnp.bfloat16)
    k_c, v_c, hq_c, hi_c, hff_c, hfb_c = _in_projection(
        ctx, s1p[ctx_rows], shift[ctx_rows], w_bf, _CTX_GROUPS, tm=ctx.shape[1])
    q, k, v, ag, hq, hi, hff, hfb, hg = _in_projection(
        x, s1p[:B], shift[:B], w_bf, _LATENT_GROUPS, tm=512, rope_tables=_rope_tables(N))

    att = _diff_attention(q, k_c, v_c, k, v, diff_lambda, diff_subln_gain, tq=256, tk=512)

    s_zero = jnp.zeros((B, 2, HEADS, HEAD_W, HEAD_W), jnp.float32)
    _, _, s_ctx = _hgrn2_bidir(hq_c, hi_c, hff_c, hfb_c, hgrn_lower_bound, s_zero)
    o_f, o_b, _ = _hgrn2_bidir(hq, hi, hff, hfb, hgrn_lower_bound, s_ctx)

    return _merge(att, ag, o_f, o_b, hg, x, gate[:B, None, :], w_out[0].astype(jnp.bfloat16),
                  hgrn_norm_gain, ln_gain, ln_bias, tm=512)
```

```python
import functools
import math

import numpy as np
import jax
import jax.numpy as jnp
from jax import lax
from jax.experimental import pallas as pl
from jax.experimental.pallas import tpu as pltpu

D_MODEL = 1024
DEPTH = 1
GRID_W = 64
HEADS = 4
HEAD_W = 128
MAP_D = 64
BRANCH_W = HEADS * HEAD_W
N_GROUPS = 9
ROPE_BASE = 10000.0
ROPE_FREQS = MAP_D // 4
EPS = 1e-5
LAMBDA_INIT = 0.8 - 0.6 * math.exp(-0.3 * 0)
ALPHA = (2.0 * DEPTH) ** 0.25

V7X_VMEM_LIMIT_BYTES = 56 * 1024 * 1024
SUBLANES = 8
HGRN_CHUNK = 128
HGRN_LEVELS = 7

_NT = (((1,), (1,)), ((), ()))
_TN = (((0,), (0,)), ((), ()))


def _silu(x):
    return x * jax.nn.sigmoid(x)


def _mod_kernel(c_ref, w_ref, b_ref, o_ref):
    a = _silu(c_ref[...])
    o_ref[...] = jnp.dot(a, w_ref[...], preferred_element_type=jnp.float32,
                         precision=lax.Precision.HIGHEST) + b_ref[...]


def _modulation(cvec, w_ada, b_ada):
    rows, d = cvec.shape
    n_out = w_ada.shape[1]
    bn = 1024
    return pl.pallas_call(
        _mod_kernel,
        grid=(n_out // bn,),
        in_specs=[pl.BlockSpec((rows, d), lambda j: (0, 0)),
                  pl.BlockSpec((d, bn), lambda j: (0, j)),
                  pl.BlockSpec((1, bn), lambda j: (0, j))],
        out_specs=pl.BlockSpec((rows, bn), lambda j: (0, j)),
        out_shape=jax.ShapeDtypeStruct((rows, n_out), jnp.float32),
        compiler_params=pltpu.CompilerParams(vmem_limit_bytes=V7X_VMEM_LIMIT_BYTES),
        name="modulation",
    )(cvec, w_ada, b_ada)


_LATENT_GROUPS = ((0, "rope_q", jnp.bfloat16), (1, "rope", jnp.bfloat16), (2, "transposed", jnp.bfloat16),
                  (3, "plain", jnp.float32), (4, "silu", jnp.float32), (5, "plain", jnp.bfloat16),
                  (6, "plain", jnp.float32), (7, "plain", jnp.float32), (8, "plain", jnp.float32))
_CTX_GROUPS = ((1, "plain", jnp.bfloat16), (2, "transposed", jnp.bfloat16), (4, "silu", jnp.float32),
               (5, "plain", jnp.bfloat16), (6, "plain", jnp.float32), (7, "plain", jnp.float32))
Q_SCALE = math.log2(math.e) / math.sqrt(MAP_D)


def _proj_kernel(groups, use_rope, *refs):
    if use_rope:
        x_ref, s1p_ref, sh_ref, w_ref, cos_ref, sin_ref = refs[:6]
        out_refs = refs[6:]
    else:
        x_ref, s1p_ref, sh_ref, w_ref = refs[:4]
        out_refs = refs[4:]
    h = (x_ref[0] * s1p_ref[0] + sh_ref[0]).astype(jnp.bfloat16)
    if use_rope:
        cos = jnp.concatenate([cos_ref[...]] * HEADS, axis=1)
        sin = jnp.concatenate([sin_ref[...]] * HEADS, axis=1)
        lane = lax.broadcasted_iota(jnp.int32, cos.shape, 1)
        first_half = (lane & (2 * ROPE_FREQS - 1)) < ROPE_FREQS
    for (g, kind, dt), o_ref in zip(groups, out_refs):
        p = jnp.dot(h, w_ref[:, g * BRANCH_W:(g + 1) * BRANCH_W], preferred_element_type=jnp.float32)
        if kind in ("rope", "rope_q"):
            partner = jnp.where(first_half,
                                pltpu.roll(p, BRANCH_W - ROPE_FREQS, axis=1),
                                pltpu.roll(p, ROPE_FREQS, axis=1))
            p = p * cos + partner * sin
            if kind == "rope_q":
                p = p * Q_SCALE
        elif kind == "silu":
            p = _silu(p)
        elif kind == "transposed":
            p = p.T
        o_ref[0] = p.astype(dt)


def _in_projection(x, s1p, sh, w_bf, groups, tm, rope_tables=None):
    B, T, D = x.shape
    use_rope = rope_tables is not None
    in_specs = [pl.BlockSpec((1, tm, D), lambda b, i: (b, i, 0)),
                pl.BlockSpec((1, 1, D), lambda b, i: (b, 0, 0)),
                pl.BlockSpec((1, 1, D), lambda b, i: (b, 0, 0)),
                pl.BlockSpec(w_bf.shape, lambda b, i: (0, 0), pipeline_mode=pl.Buffered(1))]
    args = [x, s1p, sh, w_bf]
    if use_rope:
        in_specs += [pl.BlockSpec((tm, HEAD_W), lambda b, i: (i, 0))] * 2
        args += list(rope_tables)
    out_specs = [pl.BlockSpec((1, BRANCH_W, tm), lambda b, i: (b, 0, i)) if kind == "transposed"
                 else pl.BlockSpec((1, tm, BRANCH_W), lambda b, i: (b, i, 0)) for _, kind, _ in groups]
    out_shape = [jax.ShapeDtypeStruct((B, BRANCH_W, T) if kind == "transposed" else (B, T, BRANCH_W), dt)
                 for _, kind, dt in groups]
    return pl.pallas_call(
        functools.partial(_proj_kernel, groups, use_rope),
        grid=(B, T // tm),
        in_specs=in_specs, out_specs=out_specs, out_shape=out_shape,
        compiler_params=pltpu.CompilerParams(
            dimension_semantics=("parallel", "parallel"), vmem_limit_bytes=V7X_VMEM_LIMIT_BYTES),
        name="in_proj_rope" if use_rope else "in_proj_ctx",
    )(*args)


def _rope_tables(n_tokens):
    t = np.arange(n_tokens)
    pos = np.stack([t // GRID_W, t % GRID_W], axis=-1).astype(np.float32)
    inv_freq = jnp.asarray(ROPE_BASE, jnp.float32) ** (-jnp.arange(ROPE_FREQS, dtype=jnp.float32) / ROPE_FREQS)
    ang = jnp.asarray(pos)[:, :, None] * inv_freq
    ang = jnp.stack([ang, ang], axis=2).reshape(n_tokens, MAP_D)
    sign = np.where((np.arange(MAP_D) % (2 * ROPE_FREQS)) < ROPE_FREQS, -1.0, 1.0).astype(np.float32)
    cos = jnp.cos(ang)
    sin = jnp.sin(ang) * sign
    return jnp.concatenate([cos, cos], axis=1), jnp.concatenate([sin, sin], axis=1)


def _attn_kernel(tk, q_ref, kc_ref, vtc_ref, k_ref, vt_ref, lam_ref, gain_ref, o_ref,
                 s_even, s_odd, p_even, p_odd, acc_ref):
    qt = q_ref[0].astype(jnp.float32).T
    row = lax.broadcasted_iota(jnp.int32, qt.shape, 0)
    qm = tuple(jnp.where(sel, qt, 0.0).astype(jnp.bfloat16) for sel in (row < MAP_D, row >= MAP_D))
    tq = qt.shape[1]
    n_chunks = k_ref.shape[1] // tk
    assert n_chunks % 2 == 0 and n_chunks >= 4

    def scores(kb, s_ref):
        rows = kb.shape[0]
        smax = []
        for m in range(2):
            s = jnp.dot(kb, qm[m], preferred_element_type=jnp.float32)
            s_ref[m, :rows, :] = s
            smax.append(jnp.max(s, axis=0, keepdims=True))
        return tuple(smax)

    def weights(s_ref, rows, smax, ml, p_ref):
        new_ml, corrs = [], []
        for m in range(2):
            mx, l = ml[m]
            mx_new = jnp.maximum(mx, smax[m])
            corr = jnp.exp2(mx - mx_new)
            p = jnp.exp2(s_ref[m, :rows, :] - mx_new)
            p_ref[m, :rows, :] = p.astype(jnp.bfloat16)
            new_ml.append((mx_new, l * corr + jnp.sum(p, axis=0, keepdims=True)))
            corrs.append(corr)
        return tuple(new_ml), tuple(corrs)

    def values(p_ref, rows, corr, vtb):
        for m in range(2):
            acc_ref[m] = acc_ref[m] * corr[m] + jnp.dot(vtb, p_ref[m, :rows, :],
                                                        preferred_element_type=jnp.float32)

    def k_chunk(j):
        return k_ref[0, pl.ds(pl.multiple_of(j * tk, tk), tk), :]

    def vt_chunk(j):
        return vt_ref[0, :, pl.ds(pl.multiple_of(j * tk, tk), tk)]

    ml = tuple((jnp.full((1, tq), -jnp.inf, jnp.float32), jnp.zeros((1, tq), jnp.float32)) for _ in range(2))
    acc_ref[...] = jnp.zeros_like(acc_ref)
    t_ctx = kc_ref.shape[1]

    smax0 = scores(k_chunk(0), s_even)
    ml, corr = weights(s_odd, t_ctx, scores(kc_ref[0], s_odd), ml, p_odd)
    values(p_odd, t_ctx, corr, vtc_ref[0])
    smax1 = scores(k_chunk(1), s_odd)
    ml, corr0 = weights(s_even, tk, smax0, ml, p_even)

    def body(t, carry):
        ml, smax_cur, corr_prev = carry
        j = 2 * t + 1
        smax_next = scores(k_chunk(j + 1), s_even)
        ml, corr_cur = weights(s_odd, tk, smax_cur, ml, p_odd)
        values(p_even, tk, corr_prev, vt_chunk(j - 1))
        smax_cur, corr_prev = smax_next, corr_cur
        smax_next = scores(k_chunk(j + 2), s_odd)
        ml, corr_cur = weights(s_even, tk, smax_cur, ml, p_even)
        values(p_odd, tk, corr_prev, vt_chunk(j))
        return ml, smax_next, corr_cur

    ml, smax_last, corr_prev = lax.fori_loop(0, (n_chunks - 2) // 2, body, (ml, smax1, corr0))
    ml, corr_last = weights(s_odd, tk, smax_last, ml, p_odd)
    values(p_even, tk, corr_prev, vt_chunk(n_chunks - 2))
    values(p_odd, tk, corr_last, vt_chunk(n_chunks - 1))

    lp = lam_ref[0]
    lam = (jnp.exp(jnp.sum(lp[0:1] * lp[1:2], axis=-1, keepdims=True))
           - jnp.exp(jnp.sum(lp[2:3] * lp[3:4], axis=-1, keepdims=True)) + LAMBDA_INIT)
    (_, l0), (_, l1) = ml
    ot = acc_ref[0] / l0 - lam * (acc_ref[1] / l1)
    ot = ot * lax.rsqrt(jnp.mean(ot * ot, axis=0, keepdims=True) + EPS)
    o_ref[0] = ot.T * gain_ref[...] * (1.0 - LAMBDA_INIT)


def _diff_attention(q, k_ctx, vt_ctx, k, vt, diff_lambda, gain, tq, tk):
    B, N, _ = q.shape
    T_ctx = k_ctx.shape[1]
    assert T_ctx <= tk
    s_buf = pltpu.VMEM((2, tk, tq), jnp.float32)
    p_buf = pltpu.VMEM((2, tk, tq), jnp.bfloat16)
    return pl.pallas_call(
        functools.partial(_attn_kernel, tk),
        scratch_shapes=[s_buf, s_buf, p_buf, p_buf, pltpu.VMEM((2, HEAD_W, tq), jnp.float32)],
        grid=(B, HEADS, N // tq),
        in_specs=[pl.BlockSpec((1, tq, HEAD_W), lambda b, h, i: (b, i, h)),
                  pl.BlockSpec((1, T_ctx, HEAD_W), lambda b, h, i: (b, 0, h)),
                  pl.BlockSpec((1, HEAD_W, T_ctx), lambda b, h, i: (b, h, 0)),
                  pl.BlockSpec((1, N, HEAD_W), lambda b, h, i: (b, 0, h)),
                  pl.BlockSpec((1, HEAD_W, N), lambda b, h, i: (b, h, 0)),
                  pl.BlockSpec((1, 4, MAP_D), lambda b, h, i: (0, 0, 0)),
                  pl.BlockSpec((1, HEAD_W), lambda b, h, i: (0, 0))],
        out_specs=pl.BlockSpec((1, tq, HEAD_W), lambda b, h, i: (b, i, h)),
        out_shape=jax.ShapeDtypeStruct((B, N, BRANCH_W), jnp.float32),
        compiler_params=pltpu.CompilerParams(
            dimension_semantics=("parallel", "parallel", "parallel"),
            vmem_limit_bytes=V7X_VMEM_LIMIT_BYTES),
        name="diff_attention",
    )(q, k_ctx, vt_ctx, k, vt, diff_lambda, gain)


def _level_maps():
    t = np.arange(HGRN_CHUNK)[:, None]
    s = np.arange(HGRN_CHUNK)[None, :]
    x = t ^ s
    lvl = np.where(x > 0, np.floor(np.log2(np.maximum(x, 1))).astype(np.int32) + 1, 0)
    lvl = np.where(s > t, -1, lvl).astype(np.int32)
    return np.stack([lvl, lvl.T])


def _shift_down(x, s):
    return pltpu.roll(x, s, axis=0)


def _shift_up(x, s):
    return pltpu.roll(x, HGRN_CHUNK - s, axis=0)


def _hgrn_chain(q, z, v_bf, lb, st, lvl, reverse):
    C = HGRN_CHUNK
    sig = jax.nn.sigmoid(z)
    g = jnp.log(lb + (1.0 - lb) * sig)
    kk = (1.0 - lb) * jax.nn.sigmoid(-z)
    pos = lax.broadcasted_iota(jnp.int32, (C, HEAD_W), 0)

    b = g
    s = 1
    while s < C:
        if reverse:
            b = b + jnp.where(pos < C - s, _shift_up(b, s), 0.0)
        else:
            b = b + jnp.where(pos >= s, _shift_down(b, s), 0.0)
        s *= 2

    q_bf = q.astype(jnp.bfloat16)
    a = jnp.where(lvl == 0, lax.dot_general(q_bf, kk.astype(jnp.bfloat16), _NT,
                                            preferred_element_type=jnp.float32), 0.0)
    own = b
    nbr = _shift_up(b, 1) if reverse else _shift_down(b, 1)
    for j in range(1, HGRN_LEVELS + 1):
        h = 1 << (j - 1)
        upper = (pos & h) != 0
        ref = jnp.where(upper, own, nbr) if reverse else jnp.where(upper, nbr, own)
        w = jnp.exp(-jnp.abs(b - ref))
        pm = lax.dot_general((q * w).astype(jnp.bfloat16), (kk * w).astype(jnp.bfloat16), _NT,
                             preferred_element_type=jnp.float32)
        a = jnp.where(lvl == j, pm, a)
        if j < HGRN_LEVELS:
            if reverse:
                own = jnp.where(upper, _shift_down(own, h), own)
                nbr = jnp.where(upper, nbr, _shift_up(nbr, h))
            else:
                own = jnp.where(upper, own, _shift_up(own, h))
                nbr = jnp.where(upper, _shift_down(nbr, h), nbr)

    o = jnp.dot(a.astype(jnp.bfloat16), v_bf, preferred_element_type=jnp.float32)
    o = o + lax.dot_general((q * jnp.exp(b)).astype(jnp.bfloat16), st.astype(jnp.bfloat16), _NT,
                            preferred_element_type=jnp.float32)
    b_tot = b[0:1, :] if reverse else b[C - 1:C, :]
    k_hat = (kk * jnp.exp(b_tot - b)).astype(jnp.bfloat16)
    st_new = st * jnp.exp(b_tot) + lax.dot_general(v_bf, k_hat, _TN, preferred_element_type=jnp.float32)
    return o, st_new


def _hgrn_kernel(qf_ref, vf_ref, zf_ref, qb_ref, vb_ref, zb_ref, lbp_ref, lvl_ref, s0_ref,
                 of_ref, ob_ref, sfin_ref, st_ref):
    i = pl.program_id(1)

    @pl.when(i == 0)
    def _():
        st_ref[...] = s0_ref[0]

    for d, (q_ref, v_ref, z_ref, o_ref) in enumerate(((qf_ref, vf_ref, zf_ref, of_ref),
                                                      (qb_ref, vb_ref, zb_ref, ob_ref))):
        lvl = lvl_ref[d]
        for hh in range(HEADS):
            cols = slice(hh * HEAD_W, (hh + 1) * HEAD_W)
            p0 = lbp_ref[d, 0:1, cols]
            p1 = lbp_ref[d, 1:2, cols]
            pm = jnp.maximum(p0, p1)
            e0 = jnp.exp(p0 - pm)
            lb = e0 / (e0 + jnp.exp(p1 - pm))
            o, st_new = _hgrn_chain(q_ref[0, :, cols], z_ref[0, :, cols], v_ref[0, :, cols], lb,
                                    st_ref[d, hh], lvl, reverse=(d == 1))
            o_ref[0, :, cols] = o
            st_ref[d, hh] = st_new

    @pl.when(i == pl.num_programs(1) - 1)
    def _():
        sfin_ref[0] = st_ref[...]


def _hgrn2_bidir(hq, hi, hff, hfb, lb_param, s0):
    B, T, _ = hq.shape
    C = HGRN_CHUNK
    nc = T // C
    fwd = pl.BlockSpec((1, C, BRANCH_W), lambda b, i: (b, i, 0))
    bwd = pl.BlockSpec((1, C, BRANCH_W), lambda b, i: (b, nc - 1 - i, 0))
    st_spec = pl.BlockSpec((1, 2, HEADS, HEAD_W, HEAD_W), lambda b, i: (b, 0, 0, 0, 0))
    lvl = jnp.asarray(_level_maps())
    return pl.pallas_call(
        _hgrn_kernel,
        grid=(B, nc),
        in_specs=[fwd, fwd, fwd, bwd, bwd, bwd,
                  pl.BlockSpec(lb_param.shape, lambda b, i: (0, 0, 0)),
                  pl.BlockSpec(lvl.shape, lambda b, i: (0, 0, 0)),
                  st_spec],
        out_specs=[fwd, bwd, st_spec],
        out_shape=[jax.ShapeDtypeStruct((B, T, BRANCH_W), jnp.float32),
                   jax.ShapeDtypeStruct((B, T, BRANCH_W), jnp.float32),
                   jax.ShapeDtypeStruct((B, 2, HEADS, HEAD_W, HEAD_W), jnp.float32)],
        scratch_shapes=[pltpu.VMEM((2, HEADS, HEAD_W, HEAD_W), jnp.float32)],
        compiler_params=pltpu.CompilerParams(
            dimension_semantics=("parallel", "arbitrary"), vmem_limit_bytes=V7X_VMEM_LIMIT_BYTES),
        name="hgrn2_bidir",
    )(hq, hi, hff, hq, hi, hfb, lb_param, lvl, s0)


def _merge_kernel(att_ref, ag_ref, of_ref, ob_ref, hg_ref, x_ref, gate_ref, w_ref, hgain_ref, lng_ref, lnb_ref,
                  o_ref):
    att = att_ref[0] * _silu(ag_ref[0])
    o = of_ref[0] + ob_ref[0]
    parts = []
    for hh in range(HEADS):
        oh = o[:, hh * HEAD_W:(hh + 1) * HEAD_W]
        parts.append(oh * lax.rsqrt(jnp.mean(oh * oh, axis=-1, keepdims=True) + EPS) * hgain_ref[...])
    hg = jnp.concatenate(parts, axis=1) * _silu(hg_ref[0])
    y_in = jnp.concatenate([att, hg], axis=1).astype(jnp.bfloat16)
    y = jnp.dot(y_in, w_ref[...], preferred_element_type=jnp.float32)
    u = ALPHA * x_ref[0] + gate_ref[0] * y
    mu = jnp.mean(u, axis=-1, keepdims=True)
    uc = u - mu
    var = jnp.mean(uc * uc, axis=-1, keepdims=True)
    o_ref[0] = uc * lax.rsqrt(var + EPS) * lng_ref[...] + lnb_ref[...]


def _merge(att, ag, o_f, o_b, hg, x, gate, w_out_bf, hgain, ln_g, ln_b, tm):
    B, N, D = x.shape
    half = pl.BlockSpec((1, tm, BRANCH_W), lambda b, i: (b, i, 0))
    full = pl.BlockSpec((1, tm, D), lambda b, i: (b, i, 0))
    row = lambda w: pl.BlockSpec((1, w), lambda b, i: (0, 0))
    return pl.pallas_call(
        _merge_kernel,
        grid=(B, N // tm),
        in_specs=[half, half, half, half, half, full,
                  pl.BlockSpec((1, 1, D), lambda b, i: (b, 0, 0)),
                  pl.BlockSpec(w_out_bf.shape, lambda b, i: (0, 0)),
                  row(HEAD_W), row(D), row(D)],
        out_specs=full,
        out_shape=jax.ShapeDtypeStruct((B, N, D), jnp.float32),
        compiler_params=pltpu.CompilerParams(
            dimension_semantics=("parallel", "parallel"), vmem_limit_bytes=V7X_VMEM_LIMIT_BYTES),
        name="merge_out_proj_ln",
    )(att, ag, o_f, o_b, hg, x, gate, w_out_bf, hgain, ln_g, ln_b)


def kernel(x, c, ctx, c_ctx, w_ada, b_ada, w_in, w_out, diff_lambda, diff_subln_gain, hgrn_lower_bound,
           hgrn_norm_gain, ln_gain, ln_bias):
    B, N, D = x.shape
    assert DEPTH == 1 and w_ada.shape[0] == 1
    cvec = jnp.concatenate([c, c_ctx[None, :], jnp.zeros((SUBLANES - B - 1, D), c.dtype)], axis=0)
    mod = _modulation(cvec, w_ada[0], b_ada[0][None, :])
    shift, scale, gate = mod[:, :D], mod[:, D:2 * D], mod[:, 2 * D:]
    s1p = (1.0 + scale)[:, None, :]
    shift = shift[:, None, :]
    ctx_rows = jnp.full((B,), B, jnp.int32)

    w_bf = w_in[0].astype(jnp.bfloat16)
    k_c, v_c, hq_c, hi_c, hff_c, hfb_c = _in_projection(
        ctx, s1p[ctx_rows], shift[ctx_rows], w_bf, _CTX_GROUPS, tm=ctx.shape[1])
    q, k, v, ag, hq, hi, hff, hfb, hg = _in_projection(
        x, s1p[:B], shift[:B], w_bf, _LATENT_GROUPS, tm=512, rope_tables=_rope_tables(N))

    att = _diff_attention(q, k_c, v_c, k, v, diff_lambda, diff_subln_gain, tq=256, tk=512)

    s_zero = jnp.zeros((B, 2, HEADS, HEAD_W, HEAD_W), jnp.float32)
    _, _, s_ctx = _hgrn2_bidir(hq_c, hi_c, hff_c, hfb_c, hgrn_lower_bound, s_zero)
    o_f, o_b, _ = _hgrn2_bidir(hq, hi, hff, hfb, hgrn_lower_bound, s_ctx)

    return _merge(att, ag, o_f, o_b, hg, x, gate[:B, None, :], w_out[0].astype(jnp.bfloat16),
                  hgrn_norm_gain, ln_gain, ln_bias, tm=512)
```

```python
import functools
import math

import numpy as np
import jax
import jax.numpy as jnp
from jax import lax
from jax.experimental import pallas as pl
from jax.experimental.pallas import tpu as pltpu

D_MODEL = 1024
DEPTH = 1
GRID_W = 64
HEADS = 4
HEAD_W = 128
MAP_D = 64
BRANCH_W = HEADS * HEAD_W
N_GROUPS = 9
ROPE_BASE = 10000.0
ROPE_FREQS = MAP_D // 4
EPS = 1e-5
LAMBDA_INIT = 0.8 - 0.6 * math.exp(-0.3 * 0)
ALPHA = (2.0 * DEPTH) ** 0.25

V7X_VMEM_LIMIT_BYTES = 56 * 1024 * 1024
SUBLANES = 8
HGRN_CHUNK = 128
HGRN_LEVELS = 7

_NT = (((1,), (1,)), ((), ()))
_TN = (((0,), (0,)), ((), ()))


def _silu(x):
    return x * jax.nn.sigmoid(x)


def _mod_kernel(c_ref, w_ref, b_ref, o_ref):
    a = _silu(c_ref[...])
    o_ref[...] = jnp.dot(a, w_ref[...], preferred_element_type=jnp.float32,
                         precision=lax.Precision.HIGHEST) + b_ref[...]


def _modulation(cvec, w_ada, b_ada):
    rows, d = cvec.shape
    n_out = w_ada.shape[1]
    bn = 1024
    return pl.pallas_call(
        _mod_kernel,
        grid=(n_out // bn,),
        in_specs=[pl.BlockSpec((rows, d), lambda j: (0, 0)),
                  pl.BlockSpec((d, bn), lambda j: (0, j)),
                  pl.BlockSpec((1, bn), lambda j: (0, j))],
        out_specs=pl.BlockSpec((rows, bn), lambda j: (0, j)),
        out_shape=jax.ShapeDtypeStruct((rows, n_out), jnp.float32),
        compiler_params=pltpu.CompilerParams(vmem_limit_bytes=V7X_VMEM_LIMIT_BYTES),
        name="modulation",
    )(cvec, w_ada, b_ada)


_LATENT_GROUPS = ((0, "rope_q", jnp.bfloat16), (1, "rope", jnp.bfloat16), (2, "transposed", jnp.bfloat16),
                  (3, "plain", jnp.float32), (4, "silu", jnp.float32), (5, "plain", jnp.bfloat16),
                  (6, "plain", jnp.float32), (7, "plain", jnp.float32), (8, "plain", jnp.float32))
_CTX_GROUPS = ((1, "plain", jnp.bfloat16), (2, "transposed", jnp.bfloat16), (4, "silu", jnp.float32),
               (5, "plain", jnp.bfloat16), (6, "plain", jnp.float32), (7, "plain", jnp.float32))
Q_SCALE = math.log2(math.e) / math.sqrt(MAP_D)


def _proj_kernel(groups, use_rope, *refs):
    if use_rope:
        x_ref, s1p_ref, sh_ref, w_ref, cos_ref, sin_ref = refs[:6]
        out_refs = refs[6:]
    else:
        x_ref, s1p_ref, sh_ref, w_ref = refs[:4]
        out_refs = refs[4:]
    h = (x_ref[0] * s1p_ref[0] + sh_ref[0]).astype(jnp.bfloat16)
    if use_rope:
        cos = jnp.concatenate([cos_ref[...]] * HEADS, axis=1)
        sin = jnp.concatenate([sin_ref[...]] * HEADS, axis=1)
        lane = lax.broadcasted_iota(jnp.int32, cos.shape, 1)
        first_half = (lane & (2 * ROPE_FREQS - 1)) < ROPE_FREQS
    for (g, kind, dt), o_ref in zip(groups, out_refs):
        p = jnp.dot(h, w_ref[:, g * BRANCH_W:(g + 1) * BRANCH_W], preferred_element_type=jnp.float32)
        if kind in ("rope", "rope_q"):
            partner = jnp.where(first_half,
                                pltpu.roll(p, BRANCH_W - ROPE_FREQS, axis=1),
                                pltpu.roll(p, ROPE_FREQS, axis=1))
            p = p * cos + partner * sin
            if kind == "rope_q":
                p = p * Q_SCALE
        elif kind == "silu":
            p = _silu(p)
        elif kind == "transposed":
            p = p.T
        o_ref[0] = p.astype(dt)


def _in_projection(x, s1p, sh, w_bf, groups, tm, rope_tables=None):
    B, T, D = x.shape
    use_rope = rope_tables is not None
    in_specs = [pl.BlockSpec((1, tm, D), lambda b, i: (b, i, 0)),
                pl.BlockSpec((1, 1, D), lambda b, i: (b, 0, 0)),
                pl.BlockSpec((1, 1, D), lambda b, i: (b, 0, 0)),
                pl.BlockSpec(w_bf.shape, lambda b, i: (0, 0), pipeline_mode=pl.Buffered(1))]
    args = [x, s1p, sh, w_bf]
    if use_rope:
        in_specs += [pl.BlockSpec((tm, HEAD_W), lambda b, i: (i, 0))] * 2
        args += list(rope_tables)
    out_specs = [pl.BlockSpec((1, BRANCH_W, tm), lambda b, i: (b, 0, i)) if kind == "transposed"
                 else pl.BlockSpec((1, tm, BRANCH_W), lambda b, i: (b, i, 0)) for _, kind, _ in groups]
    out_shape = [jax.ShapeDtypeStruct((B, BRANCH_W, T) if kind == "transposed" else (B, T, BRANCH_W), dt)
                 for _, kind, dt in groups]
    return pl.pallas_call(
        functools.partial(_proj_kernel, groups, use_rope),
        grid=(B, T // tm),
        in_specs=in_specs, out_specs=out_specs, out_shape=out_shape,
        compiler_params=pltpu.CompilerParams(
            dimension_semantics=("parallel", "parallel"), vmem_limit_bytes=V7X_VMEM_LIMIT_BYTES),
        name="in_proj_rope" if use_rope else "in_proj_ctx",
    )(*args)


def _rope_tables(n_tokens):
    t = np.arange(n_tokens)
    pos = np.stack([t // GRID_W, t % GRID_W], axis=-1).astype(np.float32)
    inv_freq = jnp.asarray(ROPE_BASE, jnp.float32) ** (-jnp.arange(ROPE_FREQS, dtype=jnp.float32) / ROPE_FREQS)
    ang = jnp.asarray(pos)[:, :, None] * inv_freq
    ang = jnp.stack([ang, ang], axis=2).reshape(n_tokens, MAP_D)
    sign = np.where((np.arange(MAP_D) % (2 * ROPE_FREQS)) < ROPE_FREQS, -1.0, 1.0).astype(np.float32)
    cos = jnp.cos(ang)
    sin = jnp.sin(ang) * sign
    return jnp.concatenate([cos, cos], axis=1), jnp.concatenate([sin, sin], axis=1)


def _attn_kernel(tk, q_ref, k_ref, vt_ref, lam_ref, gain_ref, o_ref, s_ref, p_ref, acc_ref):
    qt = q_ref[0].astype(jnp.float32).T
    row = lax.broadcasted_iota(jnp.int32, qt.shape, 0)
    qm = tuple(jnp.where(sel, qt, 0.0).astype(jnp.bfloat16) for sel in (row < MAP_D, row >= MAP_D))
    tq = qt.shape[1]
    n_chunks = k_ref.shape[1] // tk

    def scores(j):
        kb = k_ref[0, pl.ds(pl.multiple_of(j * tk, tk), tk), :]
        smax = []
        for m in range(2):
            s = jnp.dot(kb, qm[m], preferred_element_type=jnp.float32)
            s_ref[m] = s
            smax.append(jnp.max(s, axis=0, keepdims=True))
        return tuple(smax)

    def weights(smax, ml):
        new_ml, corrs = [], []
        for m in range(2):
            mx, l = ml[m]
            mx_new = jnp.maximum(mx, smax[m])
            corr = jnp.exp2(mx - mx_new)
            p = jnp.exp2(s_ref[m] - mx_new)
            p_ref[m] = p.astype(jnp.bfloat16)
            new_ml.append((mx_new, l * corr + jnp.sum(p, axis=0, keepdims=True)))
            corrs.append(corr)
        return tuple(new_ml), tuple(corrs)

    def values(j, corr):
        vtb = vt_ref[0, :, pl.ds(pl.multiple_of(j * tk, tk), tk)]
        for m in range(2):
            acc_ref[m] = acc_ref[m] * corr[m] + jnp.dot(vtb, p_ref[m], preferred_element_type=jnp.float32)

    ml = tuple((jnp.full((1, tq), -jnp.inf, jnp.float32), jnp.zeros((1, tq), jnp.float32)) for _ in range(2))
    acc_ref[...] = jnp.zeros_like(acc_ref)
    ml, corr = weights(scores(0), ml)
    smax = scores(1)

    def body(j, carry):
        ml, smax, corr_prev = carry
        values(j - 1, corr_prev)
        ml, corr = weights(smax, ml)
        return ml, scores(j + 1), corr

    ml, smax, corr = lax.fori_loop(1, n_chunks - 1, body, (ml, smax, corr))
    values(n_chunks - 2, corr)
    ml, corr = weights(smax, ml)
    values(n_chunks - 1, corr)

    lp = lam_ref[0]
    lam = (jnp.exp(jnp.sum(lp[0:1] * lp[1:2], axis=-1, keepdims=True))
           - jnp.exp(jnp.sum(lp[2:3] * lp[3:4], axis=-1, keepdims=True)) + LAMBDA_INIT)
    (_, l0), (_, l1) = ml
    ot = acc_ref[0] / l0 - lam * (acc_ref[1] / l1)
    ot = ot * lax.rsqrt(jnp.mean(ot * ot, axis=0, keepdims=True) + EPS)
    o_ref[0] = ot.T * gain_ref[...] * (1.0 - LAMBDA_INIT)


def _diff_attention(q, k, vt, diff_lambda, gain, tq, tk):
    B, N, _ = q.shape
    Tk = k.shape[1]
    assert Tk % tk == 0 and Tk // tk >= 3
    return pl.pallas_call(
        functools.partial(_attn_kernel, tk),
        grid=(B, HEADS, N // tq),
        in_specs=[pl.BlockSpec((1, tq, HEAD_W), lambda b, h, i: (b, i, h)),
                  pl.BlockSpec((1, Tk, HEAD_W), lambda b, h, i: (b, 0, h)),
                  pl.BlockSpec((1, HEAD_W, Tk), lambda b, h, i: (b, h, 0)),
                  pl.BlockSpec((1, 4, MAP_D), lambda b, h, i: (0, 0, 0)),
                  pl.BlockSpec((1, HEAD_W), lambda b, h, i: (0, 0))],
        out_specs=pl.BlockSpec((1, tq, HEAD_W), lambda b, h, i: (b, i, h)),
        out_shape=jax.ShapeDtypeStruct((B, N, BRANCH_W), jnp.float32),
        scratch_shapes=[pltpu.VMEM((2, tk, tq), jnp.float32), pltpu.VMEM((2, tk, tq), jnp.bfloat16),
                        pltpu.VMEM((2, HEAD_W, tq), jnp.float32)],
        compiler_params=pltpu.CompilerParams(
            dimension_semantics=("parallel", "parallel", "parallel"),
            vmem_limit_bytes=V7X_VMEM_LIMIT_BYTES),
        name="diff_attention",
    )(q, k, vt, diff_lambda, gain)


def _level_maps():
    t = np.arange(HGRN_CHUNK)[:, None]
    s = np.arange(HGRN_CHUNK)[None, :]
    x = t ^ s
    lvl = np.where(x > 0, np.floor(np.log2(np.maximum(x, 1))).astype(np.int32) + 1, 0)
    lvl = np.where(s > t, -1, lvl).astype(np.int32)
    return np.stack([lvl, lvl.T])


def _shift_down(x, s):
    return pltpu.roll(x, s, axis=0)


def _shift_up(x, s):
    return pltpu.roll(x, HGRN_CHUNK - s, axis=0)


def _hgrn_chain(q, z, v_bf, lb, st, lvl, reverse):
    C = HGRN_CHUNK
    sig = jax.nn.sigmoid(z)
    g = jnp.log(lb + (1.0 - lb) * sig)
    kk = (1.0 - lb) * jax.nn.sigmoid(-z)
    pos = lax.broadcasted_iota(jnp.int32, (C, HEAD_W), 0)

    b = g
    s = 1
    while s < C:
        if reverse:
            b = b + jnp.where(pos < C - s, _shift_up(b, s), 0.0)
        else:
            b = b + jnp.where(pos >= s, _shift_down(b, s), 0.0)
        s *= 2

    q_bf = q.astype(jnp.bfloat16)
    a = jnp.where(lvl == 0, lax.dot_general(q_bf, kk.astype(jnp.bfloat16), _NT,
                                            preferred_element_type=jnp.float32), 0.0)
    own = b
    nbr = _shift_up(b, 1) if reverse else _shift_down(b, 1)
    for j in range(1, HGRN_LEVELS + 1):
        h = 1 << (j - 1)
        upper = (pos & h) != 0
        ref = jnp.where(upper, own, nbr) if reverse else jnp.where(upper, nbr, own)
        w = jnp.exp(-jnp.abs(b - ref))
        pm = lax.dot_general((q * w).astype(jnp.bfloat16), (kk * w).astype(jnp.bfloat16), _NT,
                             preferred_element_type=jnp.float32)
        a = jnp.where(lvl == j, pm, a)
        if j < HGRN_LEVELS:
            if reverse:
                own = jnp.where(upper, _shift_down(own, h), own)
                nbr = jnp.where(upper, nbr, _shift_up(nbr, h))
            else:
                own = jnp.where(upper, own, _shift_up(own, h))
                nbr = jnp.where(upper, _shift_down(nbr, h), nbr)

    o = jnp.dot(a.astype(jnp.bfloat16), v_bf, preferred_element_type=jnp.float32)
    o = o + lax.dot_general((q * jnp.exp(b)).astype(jnp.bfloat16), st.astype(jnp.bfloat16), _NT,
                            preferred_element_type=jnp.float32)
    b_tot = b[0:1, :] if reverse else b[C - 1:C, :]
    k_hat = (kk * jnp.exp(b_tot - b)).astype(jnp.bfloat16)
    st_new = st * jnp.exp(b_tot) + lax.dot_general(v_bf, k_hat, _TN, preferred_element_type=jnp.float32)
    return o, st_new


def _hgrn_kernel(qf_ref, vf_ref, zf_ref, qb_ref, vb_ref, zb_ref, lbp_ref, lvl_ref, s0_ref,
                 of_ref, ob_ref, sfin_ref, st_ref):
    i = pl.program_id(1)

    @pl.when(i == 0)
    def _():
        st_ref[...] = s0_ref[0]

    for d, (q_ref, v_ref, z_ref, o_ref) in enumerate(((qf_ref, vf_ref, zf_ref, of_ref),
                                                      (qb_ref, vb_ref, zb_ref, ob_ref))):
        lvl = lvl_ref[d]
        for hh in range(HEADS):
            cols = slice(hh * HEAD_W, (hh + 1) * HEAD_W)
            p0 = lbp_ref[d, 0:1, cols]
            p1 = lbp_ref[d, 1:2, cols]
            pm = jnp.maximum(p0, p1)
            e0 = jnp.exp(p0 - pm)
            lb = e0 / (e0 + jnp.exp(p1 - pm))
            o, st_new = _hgrn_chain(q_ref[0, :, cols], z_ref[0, :, cols], v_ref[0, :, cols], lb,
                                    st_ref[d, hh], lvl, reverse=(d == 1))
            o_ref[0, :, cols] = o
            st_ref[d, hh] = st_new

    @pl.when(i == pl.num_programs(1) - 1)
    def _():
        sfin_ref[0] = st_ref[...]


def _hgrn2_bidir(hq, hi, hff, hfb, lb_param, s0):
    B, T, _ = hq.shape
    C = HGRN_CHUNK
    nc = T // C
    fwd = pl.BlockSpec((1, C, BRANCH_W), lambda b, i: (b, i, 0))
    bwd = pl.BlockSpec((1, C, BRANCH_W), lambda b, i: (b, nc - 1 - i, 0))
    st_spec = pl.BlockSpec((1, 2, HEADS, HEAD_W, HEAD_W), lambda b, i: (b, 0, 0, 0, 0))
    lvl = jnp.asarray(_level_maps())
    return pl.pallas_call(
        _hgrn_kernel,
        grid=(B, nc),
        in_specs=[fwd, fwd, fwd, bwd, bwd, bwd,
                  pl.BlockSpec(lb_param.shape, lambda b, i: (0, 0, 0)),
                  pl.BlockSpec(lvl.shape, lambda b, i: (0, 0, 0)),
                  st_spec],
        out_specs=[fwd, bwd, st_spec],
        out_shape=[jax.ShapeDtypeStruct((B, T, BRANCH_W), jnp.float32),
                   jax.ShapeDtypeStruct((B, T, BRANCH_W), jnp.float32),
                   jax.ShapeDtypeStruct((B, 2, HEADS, HEAD_W, HEAD_W), jnp.float32)],
        scratch_shapes=[pltpu.VMEM((2, HEADS, HEAD_W, HEAD_W), jnp.float32)],
        compiler_params=pltpu.CompilerParams(
            dimension_semantics=("parallel", "arbitrary"), vmem_limit_bytes=V7X_VMEM_LIMIT_BYTES),
        name="hgrn2_bidir",
    )(hq, hi, hff, hq, hi, hfb, lb_param, lvl, s0)


def _merge_kernel(att_ref, ag_ref, of_ref, ob_ref, hg_ref, x_ref, gate_ref, w_ref, hgain_ref, lng_ref, lnb_ref,
                  o_ref):
    att = att_ref[0] * _silu(ag_ref[0])
    o = of_ref[0] + ob_ref[0]
    parts = []
    for hh in range(HEADS):
        oh = o[:, hh * HEAD_W:(hh + 1) * HEAD_W]
        parts.append(oh * lax.rsqrt(jnp.mean(oh * oh, axis=-1, keepdims=True) + EPS) * hgain_ref[...])
    hg = jnp.concatenate(parts, axis=1) * _silu(hg_ref[0])
    y_in = jnp.concatenate([att, hg], axis=1).astype(jnp.bfloat16)
    y = jnp.dot(y_in, w_ref[...], preferred_element_type=jnp.float32)
    u = ALPHA * x_ref[0] + gate_ref[0] * y
    mu = jnp.mean(u, axis=-1, keepdims=True)
    uc = u - mu
    var = jnp.mean(uc * uc, axis=-1, keepdims=True)
    o_ref[0] = uc * lax.rsqrt(var + EPS) * lng_ref[...] + lnb_ref[...]


def _merge(att, ag, o_f, o_b, hg, x, gate, w_out_bf, hgain, ln_g, ln_b, tm):
    B, N, D = x.shape
    half = pl.BlockSpec((1, tm, BRANCH_W), lambda b, i: (b, i, 0))
    full = pl.BlockSpec((1, tm, D), lambda b, i: (b, i, 0))
    row = lambda w: pl.BlockSpec((1, w), lambda b, i: (0, 0))
    return pl.pallas_call(
        _merge_kernel,
        grid=(B, N // tm),
        in_specs=[half, half, half, half, half, full,
                  pl.BlockSpec((1, 1, D), lambda b, i: (b, 0, 0)),
                  pl.BlockSpec(w_out_bf.shape, lambda b, i: (0, 0)),
                  row(HEAD_W), row(D), row(D)],
        out_specs=full,
        out_shape=jax.ShapeDtypeStruct((B, N, D), jnp.float32),
        compiler_params=pltpu.CompilerParams(
            dimension_semantics=("parallel", "parallel"), vmem_limit_bytes=V7X_VMEM_LIMIT_BYTES),
        name="merge_out_proj_ln",
    )(att, ag, o_f, o_b, hg, x, gate, w_out_bf, hgain, ln_g, ln_b)


def kernel(x, c, ctx, c_ctx, w_ada, b_ada, w_in, w_out, diff_lambda, diff_subln_gain, hgrn_lower_bound,
           hgrn_norm_gain, ln_gain, ln_bias):
    B, N, D = x.shape
    assert DEPTH == 1 and w_ada.shape[0] == 1
    cvec = jnp.concatenate([c, c_ctx[None, :], jnp.zeros((SUBLANES - B - 1, D), c.dtype)], axis=0)
    mod = _modulation(cvec, w_ada[0], b_ada[0][None, :])
    shift, scale, gate = mod[:, :D], mod[:, D:2 * D], mod[:, 2 * D:]
    s1p = (1.0 + scale)[:, None, :]
    shift = shift[:, None, :]
    ctx_rows = jnp.full((B,), B, jnp.int32)

    w_bf = w_in[0].astype(jnp.bfloat16)
    k_c, v_c, hq_c, hi_c, hff_c, hfb_c = _in_projection(
        ctx, s1p[ctx_rows], shift[ctx_rows], w_bf, _CTX_GROUPS, tm=ctx.shape[1])
    q, k, v, ag, hq, hi, hff, hfb, hg = _in_projection(
        x, s1p[:B], shift[:B], w_bf, _LATENT_GROUPS, tm=512, rope_tables=_rope_tables(N))

    k_all = jnp.concatenate([k_c, k], axis=1)
    vt_all = jnp.concatenate([v_c, v], axis=2)
    att = _diff_attention(q, k_all, vt_all, diff_lambda, diff_subln_gain, tq=512, tk=1408)

    s_zero = jnp.zeros((B, 2, HEADS, HEAD_W, HEAD_W), jnp.float32)
    _, _, s_ctx = _hgrn2_bidir(hq_c, hi_c, hff_c, hfb_c, hgrn_lower_bound, s_zero)
    o_f, o_b, _ = _hgrn2_bidir(hq, hi, hff, hfb, hgrn_lower_bound, s_ctx)

    return _merge(att, ag, o_f, o_b, hg, x, gate[:B, None, :], w_out[0].astype(jnp.bfloat16),
                  hgrn_norm_gain, ln_gain, ln_bias, tm=512)
```

```python
import functools
import math

import numpy as np
import jax
import jax.numpy as jnp
from jax import lax
from jax.experimental import pallas as pl
from jax.experimental.pallas import tpu as pltpu

D_MODEL = 1024
DEPTH = 1
GRID_W = 64
HEADS = 4
HEAD_W = 128
MAP_D = 64
BRANCH_W = HEADS * HEAD_W
N_GROUPS = 9
ROPE_BASE = 10000.0
ROPE_FREQS = MAP_D // 4
EPS = 1e-5
LAMBDA_INIT = 0.8 - 0.6 * math.exp(-0.3 * 0)
ALPHA = (2.0 * DEPTH) ** 0.25

V7X_VMEM_LIMIT_BYTES = 56 * 1024 * 1024
SUBLANES = 8
HGRN_CHUNK = 128
HGRN_LEVELS = 7

_NT = (((1,), (1,)), ((), ()))
_TN = (((0,), (0,)), ((), ()))


def _silu(x):
    return x * jax.nn.sigmoid(x)


def _mod_kernel(c_ref, w_ref, b_ref, o_ref):
    a = _silu(c_ref[...])
    o_ref[...] = jnp.dot(a, w_ref[...], preferred_element_type=jnp.float32,
                         precision=lax.Precision.HIGHEST) + b_ref[...]


def _modulation(cvec, w_ada, b_ada):
    rows, d = cvec.shape
    n_out = w_ada.shape[1]
    bn = 1024
    return pl.pallas_call(
        _mod_kernel,
        grid=(n_out // bn,),
        in_specs=[pl.BlockSpec((rows, d), lambda j: (0, 0)),
                  pl.BlockSpec((d, bn), lambda j: (0, j)),
                  pl.BlockSpec((1, bn), lambda j: (0, j))],
        out_specs=pl.BlockSpec((rows, bn), lambda j: (0, j)),
        out_shape=jax.ShapeDtypeStruct((rows, n_out), jnp.float32),
        compiler_params=pltpu.CompilerParams(vmem_limit_bytes=V7X_VMEM_LIMIT_BYTES),
        name="modulation",
    )(cvec, w_ada, b_ada)


_LATENT_GROUPS = ((0, "rope_q", jnp.bfloat16), (1, "rope", jnp.bfloat16), (2, "transposed", jnp.bfloat16),
                  (3, "plain", jnp.float32), (4, "silu", jnp.float32), (5, "plain", jnp.bfloat16),
                  (6, "plain", jnp.float32), (7, "plain", jnp.float32), (8, "plain", jnp.float32))
_CTX_GROUPS = ((1, "plain", jnp.bfloat16), (2, "transposed", jnp.bfloat16), (4, "silu", jnp.float32),
               (5, "plain", jnp.bfloat16), (6, "plain", jnp.float32), (7, "plain", jnp.float32))
Q_SCALE = math.log2(math.e) / math.sqrt(MAP_D)


def _proj_kernel(groups, use_rope, *refs):
    if use_rope:
        x_ref, s1p_ref, sh_ref, w_ref, cos_ref, sin_ref = refs[:6]
        out_refs = refs[6:]
    else:
        x_ref, s1p_ref, sh_ref, w_ref = refs[:4]
        out_refs = refs[4:]
    h = (x_ref[0] * s1p_ref[0] + sh_ref[0]).astype(jnp.bfloat16)
    if use_rope:
        cos = jnp.concatenate([cos_ref[...]] * HEADS, axis=1)
        sin = jnp.concatenate([sin_ref[...]] * HEADS, axis=1)
        lane = lax.broadcasted_iota(jnp.int32, cos.shape, 1)
        first_half = (lane & (2 * ROPE_FREQS - 1)) < ROPE_FREQS
    for (g, kind, dt), o_ref in zip(groups, out_refs):
        p = jnp.dot(h, w_ref[:, g * BRANCH_W:(g + 1) * BRANCH_W], preferred_element_type=jnp.float32)
        if kind in ("rope", "rope_q"):
            partner = jnp.where(first_half,
                                pltpu.roll(p, BRANCH_W - ROPE_FREQS, axis=1),
                                pltpu.roll(p, ROPE_FREQS, axis=1))
            p = p * cos + partner * sin
            if kind == "rope_q":
                p = p * Q_SCALE
        elif kind == "silu":
            p = _silu(p)
        elif kind == "transposed":
            p = p.T
        o_ref[0] = p.astype(dt)


def _in_projection(x, s1p, sh, w_bf, groups, tm, rope_tables=None):
    B, T, D = x.shape
    use_rope = rope_tables is not None
    in_specs = [pl.BlockSpec((1, tm, D), lambda b, i: (b, i, 0)),
                pl.BlockSpec((1, 1, D), lambda b, i: (b, 0, 0)),
                pl.BlockSpec((1, 1, D), lambda b, i: (b, 0, 0)),
                pl.BlockSpec(w_bf.shape, lambda b, i: (0, 0), pipeline_mode=pl.Buffered(1))]
    args = [x, s1p, sh, w_bf]
    if use_rope:
        in_specs += [pl.BlockSpec((tm, HEAD_W), lambda b, i: (i, 0))] * 2
        args += list(rope_tables)
    out_specs = [pl.BlockSpec((1, BRANCH_W, tm), lambda b, i: (b, 0, i)) if kind == "transposed"
                 else pl.BlockSpec((1, tm, BRANCH_W), lambda b, i: (b, i, 0)) for _, kind, _ in groups]
    out_shape = [jax.ShapeDtypeStruct((B, BRANCH_W, T) if kind == "transposed" else (B, T, BRANCH_W), dt)
                 for _, kind, dt in groups]
    return pl.pallas_call(
        functools.partial(_proj_kernel, groups, use_rope),
        grid=(B, T // tm),
        in_specs=in_specs, out_specs=out_specs, out_shape=out_shape,
        compiler_params=pltpu.CompilerParams(
            dimension_semantics=("parallel", "parallel"), vmem_limit_bytes=V7X_VMEM_LIMIT_BYTES),
        name="in_proj_rope" if use_rope else "in_proj_ctx",
    )(*args)


def _rope_tables(n_tokens):
    t = np.arange(n_tokens)
    pos = np.stack([t // GRID_W, t % GRID_W], axis=-1).astype(np.float32)
    inv_freq = jnp.asarray(ROPE_BASE, jnp.float32) ** (-jnp.arange(ROPE_FREQS, dtype=jnp.float32) / ROPE_FREQS)
    ang = jnp.asarray(pos)[:, :, None] * inv_freq
    ang = jnp.stack([ang, ang], axis=2).reshape(n_tokens, MAP_D)
    sign = np.where((np.arange(MAP_D) % (2 * ROPE_FREQS)) < ROPE_FREQS, -1.0, 1.0).astype(np.float32)
    cos = jnp.cos(ang)
    sin = jnp.sin(ang) * sign
    return jnp.concatenate([cos, cos], axis=1), jnp.concatenate([sin, sin], axis=1)


def _attn_kernel(tk, q_ref, k_ref, vt_ref, lam_ref, gain_ref, o_ref, s_ref, p_ref, acc_ref):
    qt = q_ref[0].astype(jnp.float32).T
    row = lax.broadcasted_iota(jnp.int32, qt.shape, 0)
    qm = tuple(jnp.where(sel, qt, 0.0).astype(jnp.bfloat16) for sel in (row < MAP_D, row >= MAP_D))
    tq = qt.shape[1]
    n_chunks = k_ref.shape[1] // tk

    def scores(j):
        kb = k_ref[0, pl.ds(pl.multiple_of(j * tk, tk), tk), :]
        smax = []
        for m in range(2):
            s = jnp.dot(kb, qm[m], preferred_element_type=jnp.float32)
            s_ref[m] = s
            smax.append(jnp.max(s, axis=0, keepdims=True))
        return tuple(smax)

    def weights(smax, ml):
        new_ml, corrs = [], []
        for m in range(2):
            mx, l = ml[m]
            mx_new = jnp.maximum(mx, smax[m])
            corr = jnp.exp2(mx - mx_new)
            p = jnp.exp2(s_ref[m] - mx_new)
            p_ref[m] = p.astype(jnp.bfloat16)
            new_ml.append((mx_new, l * corr + jnp.sum(p, axis=0, keepdims=True)))
            corrs.append(corr)
        return tuple(new_ml), tuple(corrs)

    def values(j, corr):
        vtb = vt_ref[0, :, pl.ds(pl.multiple_of(j * tk, tk), tk)]
        for m in range(2):
            acc_ref[m] = acc_ref[m] * corr[m] + jnp.dot(vtb, p_ref[m], preferred_element_type=jnp.float32)

    ml = tuple((jnp.full((1, tq), -jnp.inf, jnp.float32), jnp.zeros((1, tq), jnp.float32)) for _ in range(2))
    acc_ref[...] = jnp.zeros_like(acc_ref)
    ml, corr = weights(scores(0), ml)
    smax = scores(1)

    def body(j, carry):
        ml, smax, corr_prev = carry
        values(j - 1, corr_prev)
        ml, corr = weights(smax, ml)
        return ml, scores(j + 1), corr

    ml, smax, corr = lax.fori_loop(1, n_chunks - 1, body, (ml, smax, corr))
    values(n_chunks - 2, corr)
    ml, corr = weights(smax, ml)
    values(n_chunks - 1, corr)

    lp = lam_ref[0]
    lam = (jnp.exp(jnp.sum(lp[0:1] * lp[1:2], axis=-1, keepdims=True))
           - jnp.exp(jnp.sum(lp[2:3] * lp[3:4], axis=-1, keepdims=True)) + LAMBDA_INIT)
    (_, l0), (_, l1) = ml
    ot = acc_ref[0] / l0 - lam * (acc_ref[1] / l1)
    ot = ot * lax.rsqrt(jnp.mean(ot * ot, axis=0, keepdims=True) + EPS)
    o_ref[0] = ot.T * gain_ref[...] * (1.0 - LAMBDA_INIT)


def _diff_attention(q, k, vt, diff_lambda, gain, tq, tk):
    B, N, _ = q.shape
    Tk = k.shape[1]
    assert Tk % tk == 0 and Tk // tk >= 3
    return pl.pallas_call(
        functools.partial(_attn_kernel, tk),
        grid=(B, HEADS, N // tq),
        in_specs=[pl.BlockSpec((1, tq, HEAD_W), lambda b, h, i: (b, i, h)),
                  pl.BlockSpec((1, Tk, HEAD_W), lambda b, h, i: (b, 0, h)),
                  pl.BlockSpec((1, HEAD_W, Tk), lambda b, h, i: (b, h, 0)),
                  pl.BlockSpec((1, 4, MAP_D), lambda b, h, i: (0, 0, 0)),
                  pl.BlockSpec((1, HEAD_W), lambda b, h, i: (0, 0))],
        out_specs=pl.BlockSpec((1, tq, HEAD_W), lambda b, h, i: (b, i, h)),
        out_shape=jax.ShapeDtypeStruct((B, N, BRANCH_W), jnp.float32),
        scratch_shapes=[pltpu.VMEM((2, tk, tq), jnp.float32), pltpu.VMEM((2, tk, tq), jnp.bfloat16),
                        pltpu.VMEM((2, HEAD_W, tq), jnp.float32)],
        compiler_params=pltpu.CompilerParams(
            dimension_semantics=("parallel", "parallel", "parallel"),
            vmem_limit_bytes=V7X_VMEM_LIMIT_BYTES),
        name="diff_attention",
    )(q, k, vt, diff_lambda, gain)


def _level_maps():
    t = np.arange(HGRN_CHUNK)[:, None]
    s = np.arange(HGRN_CHUNK)[None, :]
    x = t ^ s
    lvl = np.where(x > 0, np.floor(np.log2(np.maximum(x, 1))).astype(np.int32) + 1, 0)
    lvl = np.where(s > t, -1, lvl).astype(np.int32)
    return np.stack([lvl, lvl.T])


def _shift_down(x, s):
    return pltpu.roll(x, s, axis=1)


def _shift_up(x, s):
    return pltpu.roll(x, SUBLANES - s, axis=1)


def _neg_abs(x):
    bits = lax.bitcast_convert_type(x, jnp.uint32) | jnp.uint32(0x80000000)
    return lax.bitcast_convert_type(bits, jnp.float32)


def _hgrn_chain(q, z, v_bf, lb, st, masks, reverse):
    C = HGRN_CHUNK
    groups = C // SUBLANES
    sig = jax.nn.sigmoid(z)
    g = jnp.log2(lb + (1.0 - lb) * sig)
    kk = (1.0 - lb) * (1.0 - sig)
    grouped = (groups, SUBLANES, HEAD_W)
    sub = lax.broadcasted_iota(jnp.int32, grouped, 1)

    bg = g.reshape(grouped)
    for s in (1, 2, 4):
        if reverse:
            bg = bg + jnp.where(sub < SUBLANES - s, _shift_up(bg, s), 0.0)
        else:
            bg = bg + jnp.where(sub >= s, _shift_down(bg, s), 0.0)
    parts = [bg[r] for r in range(groups)]
    order = range(groups - 2, -1, -1) if reverse else range(1, groups)
    for r in order:
        prev = parts[r + 1][0:1, :] if reverse else parts[r - 1][SUBLANES - 1:SUBLANES, :]
        parts[r] = parts[r] + prev
    b = jnp.concatenate(parts, axis=0)
    bg = b.reshape(grouped)

    q_bf = q.astype(jnp.bfloat16)
    kk_bf = kk.astype(jnp.bfloat16)

    def rows(x, r):
        return x[r * SUBLANES:(r + 1) * SUBLANES, :]

    pm = lax.dot_general(q_bf, kk_bf, _NT, preferred_element_type=jnp.float32)
    a_parts = [jnp.where(rows(masks[0], r), rows(pm, r), 0.0) for r in range(groups)]

    own = bg
    nbr = _shift_up(bg, 1) if reverse else _shift_down(bg, 1)
    for j in range(1, HGRN_LEVELS + 1):
        h = 1 << (j - 1)
        if 2 * h <= SUBLANES:
            upper = (sub & h) != 0
            ref = jnp.where(upper, own, nbr) if reverse else jnp.where(upper, nbr, own)
            w = jnp.exp2(_neg_abs(bg - ref)).reshape(C, HEAD_W).astype(jnp.bfloat16)
            if 4 * h <= SUBLANES:
                if reverse:
                    own = jnp.where(upper, _shift_down(own, h), own)
                    nbr = jnp.where(upper, nbr, _shift_up(nbr, h))
                else:
                    own = jnp.where(upper, own, _shift_up(own, h))
                    nbr = jnp.where(upper, _shift_down(nbr, h), nbr)
            pm = lax.dot_general(q_bf * w, kk_bf * w, _NT, preferred_element_type=jnp.float32)
            a_parts = [jnp.where(rows(masks[j], r), rows(pm, r), a_parts[r]) for r in range(groups)]
        else:
            n_blk, half_groups = C // (2 * h), h // SUBLANES
            blocked = (n_blk, 2 * h, HEAD_W)
            b_blk = b.reshape(blocked)
            ref = b_blk[:, h:h + 1, :] if reverse else b_blk[:, h - 1:h, :]
            w_blk = jnp.exp2(_neg_abs(b_blk - ref))
            q_half, k_half = (slice(0, h), slice(h, 2 * h)) if reverse else (slice(h, 2 * h), slice(0, h))
            q_t = (q.reshape(blocked)[:, q_half] * w_blk[:, q_half]).reshape(C // 2, HEAD_W)
            k_t = kk.reshape(blocked)[:, k_half] * w_blk[:, k_half]
            zeros = jnp.zeros_like(k_t)
            k_t = jnp.concatenate([zeros, k_t] if reverse else [k_t, zeros], axis=1).reshape(C, HEAD_W)
            pm = lax.dot_general(q_t.astype(jnp.bfloat16), k_t.astype(jnp.bfloat16), _NT,
                                 preferred_element_type=jnp.float32)
            for blk in range(n_blk):
                for i in range(half_groups):
                    r = blk * 2 * half_groups + (0 if reverse else half_groups) + i
                    piece = rows(pm, blk * half_groups + i)
                    if n_blk == 1:
                        a_parts[r] = a_parts[r] + piece
                    else:
                        a_parts[r] = jnp.where(rows(masks[j], r), piece, a_parts[r])
    a = jnp.concatenate(a_parts, axis=0)

    o = jnp.dot(a.astype(jnp.bfloat16), v_bf, preferred_element_type=jnp.float32)
    o = o + lax.dot_general(q_bf * jnp.exp2(b).astype(jnp.bfloat16), st.astype(jnp.bfloat16), _NT,
                            preferred_element_type=jnp.float32)
    b_tot = b[0:1, :] if reverse else b[C - 1:C, :]
    k_hat = kk_bf * jnp.exp2(b_tot - b).astype(jnp.bfloat16)
    st_new = st * jnp.exp2(b_tot) + lax.dot_general(v_bf, k_hat, _TN, preferred_element_type=jnp.float32)
    return o, st_new


def _hgrn_kernel(n_sub, qf_ref, vf_ref, zf_ref, qb_ref, vb_ref, zb_ref, lbp_ref, lvl_ref, s0_ref,
                 of_ref, ob_ref, sfin_ref, st_ref):
    i = pl.program_id(1)
    C = HGRN_CHUNK

    @pl.when(i == 0)
    def _():
        st_ref[...] = s0_ref[0]

    dirs = ((qf_ref, vf_ref, zf_ref, of_ref), (qb_ref, vb_ref, zb_ref, ob_ref))
    masks, lbs = [], []
    for d in range(2):
        lvl = lvl_ref[d]
        masks.append([lvl == j for j in range(HGRN_LEVELS + 1)])
        p0, p1 = lbp_ref[d, 0:1, :], lbp_ref[d, 1:2, :]
        pm = jnp.maximum(p0, p1)
        e0 = jnp.exp(p0 - pm)
        lbs.append(e0 / (e0 + jnp.exp(p1 - pm)))
    states = [[st_ref[d, hh] for hh in range(HEADS)] for d in range(2)]
    for sub in range(n_sub):
        for d, (q_ref, v_ref, z_ref, o_ref) in enumerate(dirs):
            c = n_sub - 1 - sub if d == 1 else sub
            tok = slice(c * C, (c + 1) * C)
            for hh in range(HEADS):
                cols = slice(hh * HEAD_W, (hh + 1) * HEAD_W)
                o, states[d][hh] = _hgrn_chain(q_ref[0, tok, cols], z_ref[0, tok, cols], v_ref[0, tok, cols],
                                               lbs[d][:, cols], states[d][hh], masks[d], reverse=(d == 1))
                o_ref[0, tok, cols] = o
    for d in range(2):
        for hh in range(HEADS):
            st_ref[d, hh] = states[d][hh]

    @pl.when(i == pl.num_programs(1) - 1)
    def _():
        sfin_ref[0] = st_ref[...]


def _hgrn2_bidir(hq, hi, hff, hfb, lb_param, s0, n_sub):
    B, T, _ = hq.shape
    rows = HGRN_CHUNK * n_sub
    n = T // rows
    assert n * rows == T
    fwd = pl.BlockSpec((1, rows, BRANCH_W), lambda b, i: (b, i, 0))
    bwd = pl.BlockSpec((1, rows, BRANCH_W), lambda b, i: (b, n - 1 - i, 0))
    st_spec = pl.BlockSpec((1, 2, HEADS, HEAD_W, HEAD_W), lambda b, i: (b, 0, 0, 0, 0))
    lvl = jnp.asarray(_level_maps())
    return pl.pallas_call(
        functools.partial(_hgrn_kernel, n_sub),
        grid=(B, n),
        in_specs=[fwd, fwd, fwd, bwd, bwd, bwd,
                  pl.BlockSpec(lb_param.shape, lambda b, i: (0, 0, 0)),
                  pl.BlockSpec(lvl.shape, lambda b, i: (0, 0, 0)),
                  st_spec],
        out_specs=[fwd, bwd, st_spec],
        out_shape=[jax.ShapeDtypeStruct((B, T, BRANCH_W), jnp.float32),
                   jax.ShapeDtypeStruct((B, T, BRANCH_W), jnp.float32),
                   jax.ShapeDtypeStruct((B, 2, HEADS, HEAD_W, HEAD_W), jnp.float32)],
        scratch_shapes=[pltpu.VMEM((2, HEADS, HEAD_W, HEAD_W), jnp.float32)],
        compiler_params=pltpu.CompilerParams(
            dimension_semantics=("parallel", "arbitrary"), vmem_limit_bytes=V7X_VMEM_LIMIT_BYTES),
        name="hgrn2_bidir",
    )(hq, hi, hff, hq, hi, hfb, lb_param, lvl, s0)


def _merge_kernel(att_ref, ag_ref, of_ref, ob_ref, hg_ref, x_ref, gate_ref, w_ref, hgain_ref, lng_ref, lnb_ref,
                  o_ref):
    att = att_ref[0] * _silu(ag_ref[0])
    o = of_ref[0] + ob_ref[0]
    parts = []
    for hh in range(HEADS):
        oh = o[:, hh * HEAD_W:(hh + 1) * HEAD_W]
        parts.append(oh * lax.rsqrt(jnp.mean(oh * oh, axis=-1, keepdims=True) + EPS) * hgain_ref[...])
    hg = jnp.concatenate(parts, axis=1) * _silu(hg_ref[0])
    y_in = jnp.concatenate([att, hg], axis=1).astype(jnp.bfloat16)
    y = jnp.dot(y_in, w_ref[...], preferred_element_type=jnp.float32)
    u = ALPHA * x_ref[0] + gate_ref[0] * y
    mu = jnp.mean(u, axis=-1, keepdims=True)
    uc = u - mu
    var = jnp.mean(uc * uc, axis=-1, keepdims=True)
    o_ref[0] = uc * lax.rsqrt(var + EPS) * lng_ref[...] + lnb_ref[...]


def _merge(att, ag, o_f, o_b, hg, x, gate, w_out_bf, hgain, ln_g, ln_b, tm):
    B, N, D = x.shape
    half = pl.BlockSpec((1, tm, BRANCH_W), lambda b, i: (b, i, 0))
    full = pl.BlockSpec((1, tm, D), lambda b, i: (b, i, 0))
    row = lambda w: pl.BlockSpec((1, w), lambda b, i: (0, 0))
    return pl.pallas_call(
        _merge_kernel,
        grid=(B, N // tm),
        in_specs=[half, half, half, half, half, full,
                  pl.BlockSpec((1, 1, D), lambda b, i: (b, 0, 0)),
                  pl.BlockSpec(w_out_bf.shape, lambda b, i: (0, 0)),
                  row(HEAD_W), row(D), row(D)],
        out_specs=full,
        out_shape=jax.ShapeDtypeStruct((B, N, D), jnp.float32),
        compiler_params=pltpu.CompilerParams(
            dimension_semantics=("parallel", "parallel"), vmem_limit_bytes=V7X_VMEM_LIMIT_BYTES),
        name="merge_out_proj_ln",
    )(att, ag, o_f, o_b, hg, x, gate, w_out_bf, hgain, ln_g, ln_b)


def kernel(x, c, ctx, c_ctx, w_ada, b_ada, w_in, w_out, diff_lambda, diff_subln_gain, hgrn_lower_bound,
           hgrn_norm_gain, ln_gain, ln_bias):
    B, N, D = x.shape
    assert DEPTH == 1 and w_ada.shape[0] == 1
    cvec = jnp.concatenate([c, c_ctx[None, :], jnp.zeros((SUBLANES - B - 1, D), c.dtype)], axis=0)
    mod = _modulation(cvec, w_ada[0], b_ada[0][None, :])
    shift, scale, gate = mod[:, :D], mod[:, D:2 * D], mod[:, 2 * D:]
    s1p = (1.0 + scale)[:, None, :]
    shift = shift[:, None, :]
    ctx_rows = jnp.full((B,), B, jnp.int32)

    w_bf = w_in[0].astype(jnp.bfloat16)
    k_c, v_c, hq_c, hi_c, hff_c, hfb_c = _in_projection(
        ctx, s1p[ctx_rows], shift[ctx_rows], w_bf, _CTX_GROUPS, tm=ctx.shape[1])
    q, k, v, ag, hq, hi, hff, hfb, hg = _in_projection(
        x, s1p[:B], shift[:B], w_bf, _LATENT_GROUPS, tm=512, rope_tables=_rope_tables(N))

    k_all = jnp.concatenate([k_c, k], axis=1)
    vt_all = jnp.concatenate([v_c, v], axis=2)
    att = _diff_attention(q, k_all, vt_all, diff_lambda, diff_subln_gain, tq=512, tk=1408)

    s_zero = jnp.zeros((B, 2, HEADS, HEAD_W, HEAD_W), jnp.float32)
    _, _, s_ctx = _hgrn2_bidir(hq_c, hi_c, hff_c, hfb_c, hgrn_lower_bound, s_zero, n_sub=2)
    o_f, o_b, _ = _hgrn2_bidir(hq, hi, hff, hfb, hgrn_lower_bound, s_ctx, n_sub=4)

    return _merge(att, ag, o_f, o_b, hg, x, gate[:B, None, :], w_out[0].astype(jnp.bfloat16),
                  hgrn_norm_gain, ln_gain, ln_bias, tm=512)
```

```python
import functools
import math

import numpy as np
import jax
import jax.numpy as jnp
from jax import lax
from jax.experimental import pallas as pl
from jax.experimental.pallas import tpu as pltpu

D_MODEL = 1024
DEPTH = 1
GRID_W = 64
HEADS = 4
HEAD_W = 128
MAP_D = 64
BRANCH_W = HEADS * HEAD_W
N_GROUPS = 9
ROPE_BASE = 10000.0
ROPE_FREQS = MAP_D // 4
EPS = 1e-5
LAMBDA_INIT = 0.8 - 0.6 * math.exp(-0.3 * 0)
ALPHA = (2.0 * DEPTH) ** 0.25

V7X_VMEM_LIMIT_BYTES = 56 * 1024 * 1024
SUBLANES = 8
HGRN_CHUNK = 128
HGRN_LEVELS = 7

_NT = (((1,), (1,)), ((), ()))
_TN = (((0,), (0,)), ((), ()))


def _silu(x):
    return x * jax.nn.sigmoid(x)


def _mod_kernel(c_ref, w_ref, b_ref, o_ref):
    a = _silu(c_ref[...])
    o_ref[...] = jnp.dot(a, w_ref[...], preferred_element_type=jnp.float32,
                         precision=lax.Precision.HIGHEST) + b_ref[...]


def _modulation(cvec, w_ada, b_ada):
    rows, d = cvec.shape
    n_out = w_ada.shape[1]
    bn = 1024
    return pl.pallas_call(
        _mod_kernel,
        grid=(n_out // bn,),
        in_specs=[pl.BlockSpec((rows, d), lambda j: (0, 0)),
                  pl.BlockSpec((d, bn), lambda j: (0, j)),
                  pl.BlockSpec((1, bn), lambda j: (0, j))],
        out_specs=pl.BlockSpec((rows, bn), lambda j: (0, j)),
        out_shape=jax.ShapeDtypeStruct((rows, n_out), jnp.float32),
        compiler_params=pltpu.CompilerParams(vmem_limit_bytes=V7X_VMEM_LIMIT_BYTES),
        name="modulation",
    )(cvec, w_ada, b_ada)


_LATENT_GROUPS = ((0, "rope_q", jnp.bfloat16), (1, "rope", jnp.bfloat16), (2, "transposed", jnp.bfloat16),
                  (3, "plain", jnp.bfloat16), (4, "silu", jnp.float32), (5, "plain", jnp.bfloat16),
                  (6, "plain", jnp.float32), (7, "plain", jnp.float32), (8, "plain", jnp.bfloat16))
_CTX_GROUPS = ((1, "plain", jnp.bfloat16), (2, "transposed", jnp.bfloat16), (4, "silu", jnp.float32),
               (5, "plain", jnp.bfloat16), (6, "plain", jnp.float32), (7, "plain", jnp.float32))
Q_SCALE = math.log2(math.e) / math.sqrt(MAP_D)


def _proj_kernel(groups, use_rope, *refs):
    if use_rope:
        x_ref, s1p_ref, sh_ref, w_ref, cos_ref, sin_ref = refs[:6]
        out_refs = refs[6:]
    else:
        x_ref, s1p_ref, sh_ref, w_ref = refs[:4]
        out_refs = refs[4:]
    h = (x_ref[0] * s1p_ref[0] + sh_ref[0]).astype(jnp.bfloat16)
    if use_rope:
        cos = jnp.concatenate([cos_ref[...]] * HEADS, axis=1)
        sin = jnp.concatenate([sin_ref[...]] * HEADS, axis=1)
        lane = lax.broadcasted_iota(jnp.int32, cos.shape, 1)
        first_half = (lane & (2 * ROPE_FREQS - 1)) < ROPE_FREQS
    for (g, kind, dt), o_ref in zip(groups, out_refs):
        p = jnp.dot(h, w_ref[:, g * BRANCH_W:(g + 1) * BRANCH_W], preferred_element_type=jnp.float32)
        if kind in ("rope", "rope_q"):
            partner = jnp.where(first_half,
                                pltpu.roll(p, BRANCH_W - ROPE_FREQS, axis=1),
                                pltpu.roll(p, ROPE_FREQS, axis=1))
            p = p * cos + partner * sin
            if kind == "rope_q":
                p = p * Q_SCALE
        elif kind == "silu":
            p = _silu(p)
        elif kind == "transposed":
            p = p.T
        o_ref[0] = p.astype(dt)


def _in_projection(x, s1p, sh, w_bf, groups, tm, rope_tables=None):
    B, T, D = x.shape
    use_rope = rope_tables is not None
    in_specs = [pl.BlockSpec((1, tm, D), lambda b, i: (b, i, 0)),
                pl.BlockSpec((1, 1, D), lambda b, i: (b, 0, 0)),
                pl.BlockSpec((1, 1, D), lambda b, i: (b, 0, 0)),
                pl.BlockSpec(w_bf.shape, lambda b, i: (0, 0), pipeline_mode=pl.Buffered(1))]
    args = [x, s1p, sh, w_bf]
    if use_rope:
        in_specs += [pl.BlockSpec((tm, HEAD_W), lambda b, i: (i, 0))] * 2
        args += list(rope_tables)
    out_specs = [pl.BlockSpec((1, BRANCH_W, tm), lambda b, i: (b, 0, i)) if kind == "transposed"
                 else pl.BlockSpec((1, tm, BRANCH_W), lambda b, i: (b, i, 0)) for _, kind, _ in groups]
    out_shape = [jax.ShapeDtypeStruct((B, BRANCH_W, T) if kind == "transposed" else (B, T, BRANCH_W), dt)
                 for _, kind, dt in groups]
    return pl.pallas_call(
        functools.partial(_proj_kernel, groups, use_rope),
        grid=(B, T // tm),
        in_specs=in_specs, out_specs=out_specs, out_shape=out_shape,
        compiler_params=pltpu.CompilerParams(
            dimension_semantics=("parallel", "parallel"), vmem_limit_bytes=V7X_VMEM_LIMIT_BYTES),
        name="in_proj_rope" if use_rope else "in_proj_ctx",
    )(*args)


def _rope_tables(n_tokens):
    t = np.arange(n_tokens)
    pos = np.stack([t // GRID_W, t % GRID_W], axis=-1).astype(np.float32)
    inv_freq = jnp.asarray(ROPE_BASE, jnp.float32) ** (-jnp.arange(ROPE_FREQS, dtype=jnp.float32) / ROPE_FREQS)
    ang = jnp.asarray(pos)[:, :, None] * inv_freq
    ang = jnp.stack([ang, ang], axis=2).reshape(n_tokens, MAP_D)
    sign = np.where((np.arange(MAP_D) % (2 * ROPE_FREQS)) < ROPE_FREQS, -1.0, 1.0).astype(np.float32)
    cos = jnp.cos(ang)
    sin = jnp.sin(ang) * sign
    return jnp.concatenate([cos, cos], axis=1), jnp.concatenate([sin, sin], axis=1)


def _attn_kernel(tk, q_ref, k_ref, vt_ref, lam_ref, gain_ref, o_ref, s_ref, p_ref, acc_ref):
    qt = q_ref[0].astype(jnp.float32).T
    row = lax.broadcasted_iota(jnp.int32, qt.shape, 0)
    qm = tuple(jnp.where(sel, qt, 0.0).astype(jnp.bfloat16) for sel in (row < MAP_D, row >= MAP_D))
    tq = qt.shape[1]
    n_chunks = k_ref.shape[1] // tk

    def scores(j):
        kb = k_ref[0, pl.ds(pl.multiple_of(j * tk, tk), tk), :]
        smax = []
        for m in range(2):
            s = jnp.dot(kb, qm[m], preferred_element_type=jnp.float32)
            s_ref[m] = s
            smax.append(jnp.max(s, axis=0, keepdims=True))
        return tuple(smax)

    def weights(smax, ml):
        new_ml, corrs = [], []
        for m in range(2):
            mx, l = ml[m]
            mx_new = jnp.maximum(mx, smax[m])
            corr = jnp.exp2(mx - mx_new)
            p = jnp.exp2(s_ref[m] - mx_new)
            p_ref[m] = p.astype(jnp.bfloat16)
            new_ml.append((mx_new, l * corr + jnp.sum(p, axis=0, keepdims=True)))
            corrs.append(corr)
        return tuple(new_ml), tuple(corrs)

    def values(j, corr):
        vtb = vt_ref[0, :, pl.ds(pl.multiple_of(j * tk, tk), tk)]
        for m in range(2):
            acc_ref[m] = acc_ref[m] * corr[m] + jnp.dot(vtb, p_ref[m], preferred_element_type=jnp.float32)

    ml = tuple((jnp.full((1, tq), -jnp.inf, jnp.float32), jnp.zeros((1, tq), jnp.float32)) for _ in range(2))
    acc_ref[...] = jnp.zeros_like(acc_ref)
    ml, corr = weights(scores(0), ml)
    smax = scores(1)

    def body(j, carry):
        ml, smax, corr_prev = carry
        values(j - 1, corr_prev)
        ml, corr = weights(smax, ml)
        return ml, scores(j + 1), corr

    ml, smax, corr = lax.fori_loop(1, n_chunks - 1, body, (ml, smax, corr))
    values(n_chunks - 2, corr)
    ml, corr = weights(smax, ml)
    values(n_chunks - 1, corr)

    lp = lam_ref[0]
    lam = (jnp.exp(jnp.sum(lp[0:1] * lp[1:2], axis=-1, keepdims=True))
           - jnp.exp(jnp.sum(lp[2:3] * lp[3:4], axis=-1, keepdims=True)) + LAMBDA_INIT)
    (_, l0), (_, l1) = ml
    ot = acc_ref[0] / l0 - lam * (acc_ref[1] / l1)
    ot = ot * lax.rsqrt(jnp.mean(ot * ot, axis=0, keepdims=True) + EPS)
    o_ref[0] = (ot.T * gain_ref[...] * (1.0 - LAMBDA_INIT)).astype(o_ref.dtype)


def _diff_attention(q, k, vt, diff_lambda, gain, tq, tk):
    B, N, _ = q.shape
    Tk = k.shape[1]
    assert Tk % tk == 0 and Tk // tk >= 3
    return pl.pallas_call(
        functools.partial(_attn_kernel, tk),
        grid=(B, HEADS, N // tq),
        in_specs=[pl.BlockSpec((1, tq, HEAD_W), lambda b, h, i: (b, i, h)),
                  pl.BlockSpec((1, Tk, HEAD_W), lambda b, h, i: (b, 0, h)),
                  pl.BlockSpec((1, HEAD_W, Tk), lambda b, h, i: (b, h, 0)),
                  pl.BlockSpec((1, 4, MAP_D), lambda b, h, i: (0, 0, 0)),
                  pl.BlockSpec((1, HEAD_W), lambda b, h, i: (0, 0))],
        out_specs=pl.BlockSpec((1, tq, HEAD_W), lambda b, h, i: (b, i, h)),
        out_shape=jax.ShapeDtypeStruct((B, N, BRANCH_W), jnp.bfloat16),
        scratch_shapes=[pltpu.VMEM((2, tk, tq), jnp.float32), pltpu.VMEM((2, tk, tq), jnp.bfloat16),
                        pltpu.VMEM((2, HEAD_W, tq), jnp.float32)],
        compiler_params=pltpu.CompilerParams(
            dimension_semantics=("parallel", "parallel", "parallel"),
            vmem_limit_bytes=V7X_VMEM_LIMIT_BYTES),
        name="diff_attention",
    )(q, k, vt, diff_lambda, gain)


def _level_maps():
    t = np.arange(HGRN_CHUNK)[:, None]
    s = np.arange(HGRN_CHUNK)[None, :]
    x = t ^ s
    lvl = np.where(x > 0, np.floor(np.log2(np.maximum(x, 1))).astype(np.int32) + 1, 0)
    lvl = np.where(s > t, -1, lvl).astype(np.int32)
    return np.stack([lvl, lvl.T])


def _shift_down(x, s):
    return pltpu.roll(x, s, axis=1)


def _shift_up(x, s):
    return pltpu.roll(x, SUBLANES - s, axis=1)


def _neg_abs(x):
    bits = lax.bitcast_convert_type(x, jnp.uint32) | jnp.uint32(0x80000000)
    return lax.bitcast_convert_type(bits, jnp.float32)


def _hgrn_chain(q, z, v_bf, lb, st, masks, reverse):
    C = HGRN_CHUNK
    groups = C // SUBLANES
    sig = jax.nn.sigmoid(z)
    g = jnp.log2(lb + (1.0 - lb) * sig)
    kk = (1.0 - lb) * (1.0 - sig)
    grouped = (groups, SUBLANES, HEAD_W)
    sub = lax.broadcasted_iota(jnp.int32, grouped, 1)

    bg = g.reshape(grouped)
    for s in (1, 2, 4):
        if reverse:
            bg = bg + jnp.where(sub < SUBLANES - s, _shift_up(bg, s), 0.0)
        else:
            bg = bg + jnp.where(sub >= s, _shift_down(bg, s), 0.0)
    parts = [bg[r] for r in range(groups)]
    order = range(groups - 2, -1, -1) if reverse else range(1, groups)
    for r in order:
        prev = parts[r + 1][0:1, :] if reverse else parts[r - 1][SUBLANES - 1:SUBLANES, :]
        parts[r] = parts[r] + prev
    b = jnp.concatenate(parts, axis=0)
    bg = b.reshape(grouped)

    q_bf = q.astype(jnp.bfloat16)
    kk_bf = kk.astype(jnp.bfloat16)

    def rows(x, r):
        return x[r * SUBLANES:(r + 1) * SUBLANES, :]

    pm = lax.dot_general(q_bf, kk_bf, _NT, preferred_element_type=jnp.float32)
    a_parts = [jnp.where(rows(masks[0], r), rows(pm, r), 0.0) for r in range(groups)]

    own = bg
    nbr = _shift_up(bg, 1) if reverse else _shift_down(bg, 1)
    for j in range(1, HGRN_LEVELS + 1):
        h = 1 << (j - 1)
        if 2 * h <= SUBLANES:
            upper = (sub & h) != 0
            ref = jnp.where(upper, own, nbr) if reverse else jnp.where(upper, nbr, own)
            w = jnp.exp2(_neg_abs(bg - ref)).reshape(C, HEAD_W).astype(jnp.bfloat16)
            if 4 * h <= SUBLANES:
                if reverse:
                    own = jnp.where(upper, _shift_down(own, h), own)
                    nbr = jnp.where(upper, nbr, _shift_up(nbr, h))
                else:
                    own = jnp.where(upper, own, _shift_up(own, h))
                    nbr = jnp.where(upper, _shift_down(nbr, h), nbr)
            pm = lax.dot_general(q_bf * w, kk_bf * w, _NT, preferred_element_type=jnp.float32)
            a_parts = [jnp.where(rows(masks[j], r), rows(pm, r), a_parts[r]) for r in range(groups)]
        else:
            n_blk, half_groups = C // (2 * h), h // SUBLANES
            blocked = (n_blk, 2 * h, HEAD_W)
            b_blk = b.reshape(blocked)
            ref = b_blk[:, h:h + 1, :] if reverse else b_blk[:, h - 1:h, :]
            w_blk = jnp.exp2(_neg_abs(b_blk - ref))
            q_half, k_half = (slice(0, h), slice(h, 2 * h)) if reverse else (slice(h, 2 * h), slice(0, h))
            q_t = (q.reshape(blocked)[:, q_half] * w_blk[:, q_half]).reshape(C // 2, HEAD_W)
            k_t = kk.reshape(blocked)[:, k_half] * w_blk[:, k_half]
            zeros = jnp.zeros_like(k_t)
            k_t = jnp.concatenate([zeros, k_t] if reverse else [k_t, zeros], axis=1).reshape(C, HEAD_W)
            pm = lax.dot_general(q_t.astype(jnp.bfloat16), k_t.astype(jnp.bfloat16), _NT,
                                 preferred_element_type=jnp.float32)
            for blk in range(n_blk):
                for i in range(half_groups):
                    r = blk * 2 * half_groups + (0 if reverse else half_groups) + i
                    piece = rows(pm, blk * half_groups + i)
                    if n_blk == 1:
                        a_parts[r] = a_parts[r] + piece
                    else:
                        a_parts[r] = jnp.where(rows(masks[j], r), piece, a_parts[r])
    a = jnp.concatenate(a_parts, axis=0)

    o = jnp.dot(a.astype(jnp.bfloat16), v_bf, preferred_element_type=jnp.float32)
    o = o + lax.dot_general(q_bf * jnp.exp2(b).astype(jnp.bfloat16), st.astype(jnp.bfloat16), _NT,
                            preferred_element_type=jnp.float32)
    b_tot = b[0:1, :] if reverse else b[C - 1:C, :]
    k_hat = kk_bf * jnp.exp2(b_tot - b).astype(jnp.bfloat16)
    st_new = st * jnp.exp2(b_tot) + lax.dot_general(v_bf, k_hat, _TN, preferred_element_type=jnp.float32)
    return o, st_new


def _hgrn_kernel(n_sub, qf_ref, vf_ref, zf_ref, qb_ref, vb_ref, zb_ref, lbp_ref, lvl_ref, s0_ref,
                 of_ref, ob_ref, sfin_ref, st_ref):
    i = pl.program_id(1)
    C = HGRN_CHUNK

    @pl.when(i == 0)
    def _():
        st_ref[...] = s0_ref[0]

    dirs = ((qf_ref, vf_ref, zf_ref, of_ref), (qb_ref, vb_ref, zb_ref, ob_ref))
    masks, lbs = [], []
    for d in range(2):
        lvl = lvl_ref[d]
        masks.append([lvl == j for j in range(HGRN_LEVELS + 1)])
        p0, p1 = lbp_ref[d, 0:1, :], lbp_ref[d, 1:2, :]
        pm = jnp.maximum(p0, p1)
        e0 = jnp.exp(p0 - pm)
        lbs.append(e0 / (e0 + jnp.exp(p1 - pm)))
    states = [[st_ref[d, hh] for hh in range(HEADS)] for d in range(2)]
    for sub in range(n_sub):
        for d, (q_ref, v_ref, z_ref, o_ref) in enumerate(dirs):
            c = n_sub - 1 - sub if d == 1 else sub
            tok = slice(c * C, (c + 1) * C)
            for hh in range(HEADS):
                cols = slice(hh * HEAD_W, (hh + 1) * HEAD_W)
                o, states[d][hh] = _hgrn_chain(q_ref[0, tok, cols], z_ref[0, tok, cols], v_ref[0, tok, cols],
                                               lbs[d][:, cols], states[d][hh], masks[d], reverse=(d == 1))
                o_ref[0, tok, cols] = o.astype(o_ref.dtype)
    for d in range(2):
        for hh in range(HEADS):
            st_ref[d, hh] = states[d][hh]

    @pl.when(i == pl.num_programs(1) - 1)
    def _():
        sfin_ref[0] = st_ref[...]


def _hgrn2_bidir(hq, hi, hff, hfb, lb_param, s0, n_sub):
    B, T, _ = hq.shape
    rows = HGRN_CHUNK * n_sub
    n = T // rows
    assert n * rows == T
    fwd = pl.BlockSpec((1, rows, BRANCH_W), lambda b, i: (b, i, 0))
    bwd = pl.BlockSpec((1, rows, BRANCH_W), lambda b, i: (b, n - 1 - i, 0))
    st_spec = pl.BlockSpec((1, 2, HEADS, HEAD_W, HEAD_W), lambda b, i: (b, 0, 0, 0, 0))
    lvl = jnp.asarray(_level_maps())
    return pl.pallas_call(
        functools.partial(_hgrn_kernel, n_sub),
        grid=(B, n),
        in_specs=[fwd, fwd, fwd, bwd, bwd, bwd,
                  pl.BlockSpec(lb_param.shape, lambda b, i: (0, 0, 0)),
                  pl.BlockSpec(lvl.shape, lambda b, i: (0, 0, 0)),
                  st_spec],
        out_specs=[fwd, bwd, st_spec],
        out_shape=[jax.ShapeDtypeStruct((B, T, BRANCH_W), jnp.bfloat16),
                   jax.ShapeDtypeStruct((B, T, BRANCH_W), jnp.bfloat16),
                   jax.ShapeDtypeStruct((B, 2, HEADS, HEAD_W, HEAD_W), jnp.float32)],
        scratch_shapes=[pltpu.VMEM((2, HEADS, HEAD_W, HEAD_W), jnp.float32)],
        compiler_params=pltpu.CompilerParams(
            dimension_semantics=("parallel", "arbitrary"), vmem_limit_bytes=V7X_VMEM_LIMIT_BYTES),
        name="hgrn2_bidir",
    )(hq, hi, hff, hq, hi, hfb, lb_param, lvl, s0)


def _merge_kernel(att_ref, ag_ref, of_ref, ob_ref, hg_ref, x_ref, gate_ref, w_ref, hgain_ref, lng_ref, lnb_ref,
                  o_ref):
    f32 = jnp.float32
    att = att_ref[0].astype(f32) * _silu(ag_ref[0].astype(f32))
    o = of_ref[0].astype(f32) + ob_ref[0].astype(f32)
    parts = []
    for hh in range(HEADS):
        oh = o[:, hh * HEAD_W:(hh + 1) * HEAD_W]
        parts.append(oh * lax.rsqrt(jnp.mean(oh * oh, axis=-1, keepdims=True) + EPS) * hgain_ref[...])
    hg = jnp.concatenate(parts, axis=1) * _silu(hg_ref[0].astype(f32))
    y_in = jnp.concatenate([att, hg], axis=1).astype(jnp.bfloat16)
    y = jnp.dot(y_in, w_ref[...], preferred_element_type=jnp.float32)
    u = ALPHA * x_ref[0] + gate_ref[0] * y
    mu = jnp.mean(u, axis=-1, keepdims=True)
    uc = u - mu
    var = jnp.mean(uc * uc, axis=-1, keepdims=True)
    o_ref[0] = uc * lax.rsqrt(var + EPS) * lng_ref[...] + lnb_ref[...]


def _merge(att, ag, o_f, o_b, hg, x, gate, w_out_bf, hgain, ln_g, ln_b, tm):
    B, N, D = x.shape
    half = pl.BlockSpec((1, tm, BRANCH_W), lambda b, i: (b, i, 0))
    full = pl.BlockSpec((1, tm, D), lambda b, i: (b, i, 0))
    row = lambda w: pl.BlockSpec((1, w), lambda b, i: (0, 0))
    return pl.pallas_call(
        _merge_kernel,
        grid=(B, N // tm),
        in_specs=[half, half, half, half, half, full,
                  pl.BlockSpec((1, 1, D), lambda b, i: (b, 0, 0)),
                  pl.BlockSpec(w_out_bf.shape, lambda b, i: (0, 0)),
                  row(HEAD_W), row(D), row(D)],
        out_specs=full,
        out_shape=jax.ShapeDtypeStruct((B, N, D), jnp.float32),
        compiler_params=pltpu.CompilerParams(
            dimension_semantics=("parallel", "parallel"), vmem_limit_bytes=V7X_VMEM_LIMIT_BYTES),
        name="merge_out_proj_ln",
    )(att, ag, o_f, o_b, hg, x, gate, w_out_bf, hgain, ln_g, ln_b)


def kernel(x, c, ctx, c_ctx, w_ada, b_ada, w_in, w_out, diff_lambda, diff_subln_gain, hgrn_lower_bound,
           hgrn_norm_gain, ln_gain, ln_bias):
    B, N, D = x.shape
    assert DEPTH == 1 and w_ada.shape[0] == 1
    cvec = jnp.concatenate([c, c_ctx[None, :], jnp.zeros((SUBLANES - B - 1, D), c.dtype)], axis=0)
    mod = _modulation(cvec, w_ada[0], b_ada[0][None, :])
    shift, scale, gate = mod[:, :D], mod[:, D:2 * D], mod[:, 2 * D:]
    s1p = (1.0 + scale)[:, None, :]
    shift = shift[:, None, :]
    ctx_rows = jnp.full((B,), B, jnp.int32)

    w_bf = w_in[0].astype(jnp.bfloat16)
    k_c, v_c, hq_c, hi_c, hff_c, hfb_c = _in_projection(
        ctx, s1p[ctx_rows], shift[ctx_rows], w_bf, _CTX_GROUPS, tm=ctx.shape[1])
    q, k, v, ag, hq, hi, hff, hfb, hg = _in_projection(
        x, s1p[:B], shift[:B], w_bf, _LATENT_GROUPS, tm=512, rope_tables=_rope_tables(N))

    k_all = jnp.concatenate([k_c, k], axis=1)
    vt_all = jnp.concatenate([v_c, v], axis=2)
    att = _diff_attention(q, k_all, vt_all, diff_lambda, diff_subln_gain, tq=512, tk=1408)

    s_zero = jnp.zeros((B, 2, HEADS, HEAD_W, HEAD_W), jnp.float32)
    _, _, s_ctx = _hgrn2_bidir(hq_c, hi_c, hff_c, hfb_c, hgrn_lower_bound, s_zero, n_sub=2)
    o_f, o_b, _ = _hgrn2_bidir(hq, hi, hff, hfb, hgrn_lower_bound, s_ctx, n_sub=4)

    return _merge(att, ag, o_f, o_b, hg, x, gate[:B, None, :], w_out[0].astype(jnp.bfloat16),
                  hgrn_norm_gain, ln_gain, ln_bias, tm=512)
```

```python
import functools
import math

import numpy as np
import jax
import jax.numpy as jnp
from jax import lax
from jax.experimental import pallas as pl
from jax.experimental.pallas import tpu as pltpu

D_MODEL = 1024
DEPTH = 1
GRID_W = 64
HEADS = 4
HEAD_W = 128
MAP_D = 64
BRANCH_W = HEADS * HEAD_W
N_GROUPS = 9
ROPE_BASE = 10000.0
ROPE_FREQS = MAP_D // 4
EPS = 1e-5
LAMBDA_INIT = 0.8 - 0.6 * math.exp(-0.3 * 0)
ALPHA = (2.0 * DEPTH) ** 0.25

V7X_VMEM_LIMIT_BYTES = 56 * 1024 * 1024
SUBLANES = 8
HGRN_CHUNK = 128
HGRN_LEVELS = 7

_NT = (((1,), (1,)), ((), ()))
_TN = (((0,), (0,)), ((), ()))


def _silu(x):
    return x * jax.nn.sigmoid(x)


def _mod_kernel(c_ref, w_ref, b_ref, o_ref):
    a = _silu(c_ref[...])
    o_ref[...] = jnp.dot(a, w_ref[...], preferred_element_type=jnp.float32,
                         precision=lax.Precision.HIGHEST) + b_ref[...]


def _modulation(cvec, w_ada, b_ada):
    rows, d = cvec.shape
    n_out = w_ada.shape[1]
    bn = 1024
    return pl.pallas_call(
        _mod_kernel,
        grid=(n_out // bn,),
        in_specs=[pl.BlockSpec((rows, d), lambda j: (0, 0)),
                  pl.BlockSpec((d, bn), lambda j: (0, j)),
                  pl.BlockSpec((1, bn), lambda j: (0, j))],
        out_specs=pl.BlockSpec((rows, bn), lambda j: (0, j)),
        out_shape=jax.ShapeDtypeStruct((rows, n_out), jnp.float32),
        compiler_params=pltpu.CompilerParams(vmem_limit_bytes=V7X_VMEM_LIMIT_BYTES),
        name="modulation",
    )(cvec, w_ada, b_ada)


_LATENT_GROUPS = ((0, "rope_q", jnp.bfloat16), (1, "rope", jnp.bfloat16), (2, "transposed", jnp.bfloat16),
                  (3, "plain", jnp.bfloat16), (4, "silu", jnp.float32), (5, "plain", jnp.bfloat16),
                  (6, "plain", jnp.float32), (7, "plain", jnp.float32), (8, "plain", jnp.bfloat16))
_CTX_GROUPS = ((1, "plain", jnp.bfloat16), (2, "transposed", jnp.bfloat16), (4, "silu", jnp.float32),
               (5, "plain", jnp.bfloat16), (6, "plain", jnp.float32), (7, "plain", jnp.float32))
Q_SCALE = math.log2(math.e) / math.sqrt(MAP_D)


def _proj_kernel(groups, use_rope, *refs):
    if use_rope:
        x_ref, s1p_ref, sh_ref, w_ref, cos_ref, sin_ref = refs[:6]
        out_refs = refs[6:]
    else:
        x_ref, s1p_ref, sh_ref, w_ref = refs[:4]
        out_refs = refs[4:]
    h = (x_ref[0] * s1p_ref[0] + sh_ref[0]).astype(jnp.bfloat16)
    if use_rope:
        cos = jnp.concatenate([cos_ref[...]] * HEADS, axis=1)
        sin = jnp.concatenate([sin_ref[...]] * HEADS, axis=1)
        lane = lax.broadcasted_iota(jnp.int32, cos.shape, 1)
        first_half = (lane & (2 * ROPE_FREQS - 1)) < ROPE_FREQS
    for (g, kind, dt), o_ref in zip(groups, out_refs):
        p = jnp.dot(h, w_ref[:, g * BRANCH_W:(g + 1) * BRANCH_W], preferred_element_type=jnp.float32)
        if kind in ("rope", "rope_q"):
            partner = jnp.where(first_half,
                                pltpu.roll(p, BRANCH_W - ROPE_FREQS, axis=1),
                                pltpu.roll(p, ROPE_FREQS, axis=1))
            p = p * cos + partner * sin
            if kind == "rope_q":
                p = p * Q_SCALE
        elif kind == "silu":
            p = _silu(p)
        elif kind == "transposed":
            p = p.T
        o_ref[0] = p.astype(dt)


def _in_projection(x, s1p, sh, w_bf, groups, tm, rope_tables=None):
    B, T, D = x.shape
    use_rope = rope_tables is not None
    in_specs = [pl.BlockSpec((1, tm, D), lambda b, i: (b, i, 0)),
                pl.BlockSpec((1, 1, D), lambda b, i: (b, 0, 0)),
                pl.BlockSpec((1, 1, D), lambda b, i: (b, 0, 0)),
                pl.BlockSpec(w_bf.shape, lambda b, i: (0, 0), pipeline_mode=pl.Buffered(1))]
    args = [x, s1p, sh, w_bf]
    if use_rope:
        in_specs += [pl.BlockSpec((tm, HEAD_W), lambda b, i: (i, 0))] * 2
        args += list(rope_tables)
    out_specs = [pl.BlockSpec((1, BRANCH_W, tm), lambda b, i: (b, 0, i)) if kind == "transposed"
                 else pl.BlockSpec((1, tm, BRANCH_W), lambda b, i: (b, i, 0)) for _, kind, _ in groups]
    out_shape = [jax.ShapeDtypeStruct((B, BRANCH_W, T) if kind == "transposed" else (B, T, BRANCH_W), dt)
                 for _, kind, dt in groups]
    return pl.pallas_call(
        functools.partial(_proj_kernel, groups, use_rope),
        grid=(B, T // tm),
        in_specs=in_specs, out_specs=out_specs, out_shape=out_shape,
        compiler_params=pltpu.CompilerParams(
            dimension_semantics=("parallel", "parallel"), vmem_limit_bytes=V7X_VMEM_LIMIT_BYTES),
        name="in_proj_rope" if use_rope else "in_proj_ctx",
    )(*args)


def _rope_tables(n_tokens):
    t = np.arange(n_tokens)
    pos = np.stack([t // GRID_W, t % GRID_W], axis=-1).astype(np.float32)
    inv_freq = jnp.asarray(ROPE_BASE, jnp.float32) ** (-jnp.arange(ROPE_FREQS, dtype=jnp.float32) / ROPE_FREQS)
    ang = jnp.asarray(pos)[:, :, None] * inv_freq
    ang = jnp.stack([ang, ang], axis=2).reshape(n_tokens, MAP_D)
    sign = np.where((np.arange(MAP_D) % (2 * ROPE_FREQS)) < ROPE_FREQS, -1.0, 1.0).astype(np.float32)
    cos = jnp.cos(ang)
    sin = jnp.sin(ang) * sign
    return jnp.concatenate([cos, cos], axis=1), jnp.concatenate([sin, sin], axis=1)


def _attn_kernel(tq, tk, q_ref, k_ref, vt_ref, lam_ref, gain_ref, o_ref, s_ref, p_ref, acc_ref):
    n_chunks = k_ref.shape[1] // tk
    lp = lam_ref[0]
    lam = (jnp.exp(jnp.sum(lp[0:1] * lp[1:2], axis=-1, keepdims=True))
           - jnp.exp(jnp.sum(lp[2:3] * lp[3:4], axis=-1, keepdims=True)) + LAMBDA_INIT)

    def query_block(qi, _):
        q_rows = pl.ds(pl.multiple_of(qi * tq, tq), tq)
        qt = q_ref[0, q_rows, :].astype(jnp.float32).T
        row = lax.broadcasted_iota(jnp.int32, qt.shape, 0)
        qm = tuple(jnp.where(sel, qt, 0.0).astype(jnp.bfloat16) for sel in (row < MAP_D, row >= MAP_D))

        def scores(j):
            kb = k_ref[0, pl.ds(pl.multiple_of(j * tk, tk), tk), :]
            smax = []
            for m in range(2):
                s = jnp.dot(kb, qm[m], preferred_element_type=jnp.float32)
                s_ref[m] = s
                smax.append(jnp.max(s, axis=0, keepdims=True))
            return tuple(smax)

        def weights(smax, ml):
            new_ml, corrs = [], []
            for m in range(2):
                mx, l = ml[m]
                mx_new = jnp.maximum(mx, smax[m])
                corr = jnp.exp2(mx - mx_new)
                p = jnp.exp2(s_ref[m] - mx_new)
                p_ref[m] = p.astype(jnp.bfloat16)
                new_ml.append((mx_new, l * corr + jnp.sum(p, axis=0, keepdims=True)))
                corrs.append(corr)
            return tuple(new_ml), tuple(corrs)

        def values(j, corr):
            vtb = vt_ref[0, :, pl.ds(pl.multiple_of(j * tk, tk), tk)]
            for m in range(2):
                acc_ref[m] = acc_ref[m] * corr[m] + jnp.dot(vtb, p_ref[m], preferred_element_type=jnp.float32)

        ml = tuple((jnp.full((1, tq), -jnp.inf, jnp.float32), jnp.zeros((1, tq), jnp.float32)) for _ in range(2))
        acc_ref[...] = jnp.zeros_like(acc_ref)
        ml, corr = weights(scores(0), ml)
        smax = scores(1)

        def body(j, carry):
            ml, smax, corr_prev = carry
            values(j - 1, corr_prev)
            ml, corr = weights(smax, ml)
            return ml, scores(j + 1), corr

        ml, smax, corr = lax.fori_loop(1, n_chunks - 1, body, (ml, smax, corr))
        values(n_chunks - 2, corr)
        ml, corr = weights(smax, ml)
        values(n_chunks - 1, corr)

        (_, l0), (_, l1) = ml
        ot = acc_ref[0] / l0 - lam * (acc_ref[1] / l1)
        ot = ot * lax.rsqrt(jnp.mean(ot * ot, axis=0, keepdims=True) + EPS)
        o_ref[0, q_rows, :] = (ot.T * gain_ref[...] * (1.0 - LAMBDA_INIT)).astype(o_ref.dtype)
        return 0

    lax.fori_loop(0, q_ref.shape[1] // tq, query_block, 0)


def _diff_attention(q, k, vt, diff_lambda, gain, tq, tk):
    B, N, _ = q.shape
    Tk = k.shape[1]
    assert Tk % tk == 0 and Tk // tk >= 3 and N % tq == 0
    return pl.pallas_call(
        functools.partial(_attn_kernel, tq, tk),
        grid=(B, HEADS),
        in_specs=[pl.BlockSpec((1, N, HEAD_W), lambda b, h: (b, 0, h)),
                  pl.BlockSpec((1, Tk, HEAD_W), lambda b, h: (b, 0, h)),
                  pl.BlockSpec((1, HEAD_W, Tk), lambda b, h: (b, h, 0)),
                  pl.BlockSpec((1, 4, MAP_D), lambda b, h: (0, 0, 0)),
                  pl.BlockSpec((1, HEAD_W), lambda b, h: (0, 0))],
        out_specs=pl.BlockSpec((1, N, HEAD_W), lambda b, h: (b, 0, h)),
        out_shape=jax.ShapeDtypeStruct((B, N, BRANCH_W), jnp.bfloat16),
        scratch_shapes=[pltpu.VMEM((2, tk, tq), jnp.float32), pltpu.VMEM((2, tk, tq), jnp.bfloat16),
                        pltpu.VMEM((2, HEAD_W, tq), jnp.float32)],
        compiler_params=pltpu.CompilerParams(
            dimension_semantics=("parallel", "parallel"),
            vmem_limit_bytes=V7X_VMEM_LIMIT_BYTES),
        name="diff_attention",
    )(q, k, vt, diff_lambda, gain)


def _level_maps():
    t = np.arange(HGRN_CHUNK)[:, None]
    s = np.arange(HGRN_CHUNK)[None, :]
    x = t ^ s
    lvl = np.where(x > 0, np.floor(np.log2(np.maximum(x, 1))).astype(np.int32) + 1, 0)
    lvl = np.where(s > t, -1, lvl).astype(np.int32)
    return np.stack([lvl, lvl.T])


def _shift_down(x, s):
    return pltpu.roll(x, s, axis=1)


def _shift_up(x, s):
    return pltpu.roll(x, SUBLANES - s, axis=1)


def _neg_abs(x):
    bits = lax.bitcast_convert_type(x, jnp.uint32) | jnp.uint32(0x80000000)
    return lax.bitcast_convert_type(bits, jnp.float32)


def _hgrn_chain(q, z, v_bf, lb, st, masks, reverse):
    C = HGRN_CHUNK
    groups = C // SUBLANES
    sig = jax.nn.sigmoid(z)
    g = jnp.log2(lb + (1.0 - lb) * sig)
    kk = (1.0 - lb) * (1.0 - sig)
    grouped = (groups, SUBLANES, HEAD_W)
    sub = lax.broadcasted_iota(jnp.int32, grouped, 1)

    bg = g.reshape(grouped)
    for s in (1, 2, 4):
        if reverse:
            bg = bg + jnp.where(sub < SUBLANES - s, _shift_up(bg, s), 0.0)
        else:
            bg = bg + jnp.where(sub >= s, _shift_down(bg, s), 0.0)
    parts = [bg[r] for r in range(groups)]
    order = range(groups - 2, -1, -1) if reverse else range(1, groups)
    for r in order:
        prev = parts[r + 1][0:1, :] if reverse else parts[r - 1][SUBLANES - 1:SUBLANES, :]
        parts[r] = parts[r] + prev
    b = jnp.concatenate(parts, axis=0)
    bg = b.reshape(grouped)

    q_bf = q.astype(jnp.bfloat16)
    kk_bf = kk.astype(jnp.bfloat16)

    def rows(x, r):
        return x[r * SUBLANES:(r + 1) * SUBLANES, :]

    pm = lax.dot_general(q_bf, kk_bf, _NT, preferred_element_type=jnp.float32)
    a_parts = [jnp.where(rows(masks[0], r), rows(pm, r), 0.0) for r in range(groups)]

    own = bg
    nbr = _shift_up(bg, 1) if reverse else _shift_down(bg, 1)
    for j in range(1, HGRN_LEVELS + 1):
        h = 1 << (j - 1)
        if 2 * h <= SUBLANES:
            upper = (sub & h) != 0
            ref = jnp.where(upper, own, nbr) if reverse else jnp.where(upper, nbr, own)
            w = jnp.exp2(_neg_abs(bg - ref)).reshape(C, HEAD_W).astype(jnp.bfloat16)
            if 4 * h <= SUBLANES:
                if reverse:
                    own = jnp.where(upper, _shift_down(own, h), own)
                    nbr = jnp.where(upper, nbr, _shift_up(nbr, h))
                else:
                    own = jnp.where(upper, own, _shift_up(own, h))
                    nbr = jnp.where(upper, _shift_down(nbr, h), nbr)
            pm = lax.dot_general(q_bf * w, kk_bf * w, _NT, preferred_element_type=jnp.float32)
            a_parts = [jnp.where(rows(masks[j], r), rows(pm, r), a_parts[r]) for r in range(groups)]
        else:
            n_blk, half_groups = C // (2 * h), h // SUBLANES
            blocked = (n_blk, 2 * h, HEAD_W)
            b_blk = b.reshape(blocked)
            ref = b_blk[:, h:h + 1, :] if reverse else b_blk[:, h - 1:h, :]
            w_blk = jnp.exp2(_neg_abs(b_blk - ref))
            q_half, k_half = (slice(0, h), slice(h, 2 * h)) if reverse else (slice(h, 2 * h), slice(0, h))
            q_t = (q.reshape(blocked)[:, q_half] * w_blk[:, q_half]).reshape(C // 2, HEAD_W)
            k_t = kk.reshape(blocked)[:, k_half] * w_blk[:, k_half]
            zeros = jnp.zeros_like(k_t)
            k_t = jnp.concatenate([zeros, k_t] if reverse else [k_t, zeros], axis=1).reshape(C, HEAD_W)
            pm = lax.dot_general(q_t.astype(jnp.bfloat16), k_t.astype(jnp.bfloat16), _NT,
                                 preferred_element_type=jnp.float32)
            for blk in range(n_blk):
                for i in range(half_groups):
                    r = blk * 2 * half_groups + (0 if reverse else half_groups) + i
                    piece = rows(pm, blk * half_groups + i)
                    if n_blk == 1:
                        a_parts[r] = a_parts[r] + piece
                    else:
                        a_parts[r] = jnp.where(rows(masks[j], r), piece, a_parts[r])
    a = jnp.concatenate(a_parts, axis=0)

    o = jnp.dot(a.astype(jnp.bfloat16), v_bf, preferred_element_type=jnp.float32)
    o = o + lax.dot_general(q_bf * jnp.exp2(b).astype(jnp.bfloat16), st.astype(jnp.bfloat16), _NT,
                            preferred_element_type=jnp.float32)
    b_tot = b[0:1, :] if reverse else b[C - 1:C, :]
    k_hat = kk_bf * jnp.exp2(b_tot - b).astype(jnp.bfloat16)
    st_new = st * jnp.exp2(b_tot) + lax.dot_general(v_bf, k_hat, _TN, preferred_element_type=jnp.float32)
    return o, st_new


def _hgrn_kernel(n_sub, qf_ref, vf_ref, zf_ref, qb_ref, vb_ref, zb_ref, lbp_ref, lvl_ref, s0_ref,
                 of_ref, ob_ref, sfin_ref, st_ref):
    i = pl.program_id(1)
    C = HGRN_CHUNK

    @pl.when(i == 0)
    def _():
        st_ref[...] = s0_ref[0]

    dirs = ((qf_ref, vf_ref, zf_ref, of_ref), (qb_ref, vb_ref, zb_ref, ob_ref))
    masks, lbs = [], []
    for d in range(2):
        lvl = lvl_ref[d]
        masks.append([lvl == j for j in range(HGRN_LEVELS + 1)])
        p0, p1 = lbp_ref[d, 0:1, :], lbp_ref[d, 1:2, :]
        pm = jnp.maximum(p0, p1)
        e0 = jnp.exp(p0 - pm)
        lbs.append(e0 / (e0 + jnp.exp(p1 - pm)))
    states = [[st_ref[d, hh] for hh in range(HEADS)] for d in range(2)]
    for sub in range(n_sub):
        for d, (q_ref, v_ref, z_ref, o_ref) in enumerate(dirs):
            c = n_sub - 1 - sub if d == 1 else sub
            tok = slice(c * C, (c + 1) * C)
            for hh in range(HEADS):
                cols = slice(hh * HEAD_W, (hh + 1) * HEAD_W)
                o, states[d][hh] = _hgrn_chain(q_ref[0, tok, cols], z_ref[0, tok, cols], v_ref[0, tok, cols],
                                               lbs[d][:, cols], states[d][hh], masks[d], reverse=(d == 1))
                o_ref[0, tok, cols] = o.astype(o_ref.dtype)
    for d in range(2):
        for hh in range(HEADS):
            st_ref[d, hh] = states[d][hh]

    @pl.when(i == pl.num_programs(1) - 1)
    def _():
        sfin_ref[0] = st_ref[...]


def _hgrn2_bidir(hq, hi, hff, hfb, lb_param, s0, n_sub):
    B, T, _ = hq.shape
    rows = HGRN_CHUNK * n_sub
    n = T // rows
    assert n * rows == T
    fwd = pl.BlockSpec((1, rows, BRANCH_W), lambda b, i: (b, i, 0))
    bwd = pl.BlockSpec((1, rows, BRANCH_W), lambda b, i: (b, n - 1 - i, 0))
    st_spec = pl.BlockSpec((1, 2, HEADS, HEAD_W, HEAD_W), lambda b, i: (b, 0, 0, 0, 0))
    lvl = jnp.asarray(_level_maps())
    return pl.pallas_call(
        functools.partial(_hgrn_kernel, n_sub),
        grid=(B, n),
        in_specs=[fwd, fwd, fwd, bwd, bwd, bwd,
                  pl.BlockSpec(lb_param.shape, lambda b, i: (0, 0, 0)),
                  pl.BlockSpec(lvl.shape, lambda b, i: (0, 0, 0)),
                  st_spec],
        out_specs=[fwd, bwd, st_spec],
        out_shape=[jax.ShapeDtypeStruct((B, T, BRANCH_W), jnp.bfloat16),
                   jax.ShapeDtypeStruct((B, T, BRANCH_W), jnp.bfloat16),
                   jax.ShapeDtypeStruct((B, 2, HEADS, HEAD_W, HEAD_W), jnp.float32)],
        scratch_shapes=[pltpu.VMEM((2, HEADS, HEAD_W, HEAD_W), jnp.float32)],
        compiler_params=pltpu.CompilerParams(
            dimension_semantics=("parallel", "arbitrary"), vmem_limit_bytes=V7X_VMEM_LIMIT_BYTES),
        name="hgrn2_bidir",
    )(hq, hi, hff, hq, hi, hfb, lb_param, lvl, s0)


def _merge_kernel(att_ref, ag_ref, of_ref, ob_ref, hg_ref, x_ref, gate_ref, w_ref, hgain_ref, lng_ref, lnb_ref,
                  o_ref):
    f32 = jnp.float32
    att = att_ref[0].astype(f32) * _silu(ag_ref[0].astype(f32))
    o = of_ref[0].astype(f32) + ob_ref[0].astype(f32)
    parts = []
    for hh in range(HEADS):
        oh = o[:, hh * HEAD_W:(hh + 1) * HEAD_W]
        parts.append(oh * lax.rsqrt(jnp.mean(oh * oh, axis=-1, keepdims=True) + EPS) * hgain_ref[...])
    hg = jnp.concatenate(parts, axis=1) * _silu(hg_ref[0].astype(f32))
    y_in = jnp.concatenate([att, hg], axis=1).astype(jnp.bfloat16)
    y = jnp.dot(y_in, w_ref[...], preferred_element_type=jnp.float32)
    u = ALPHA * x_ref[0] + gate_ref[0] * y
    mu = jnp.mean(u, axis=-1, keepdims=True)
    uc = u - mu
    var = jnp.mean(uc * uc, axis=-1, keepdims=True)
    o_ref[0] = uc * lax.rsqrt(var + EPS) * lng_ref[...] + lnb_ref[...]


def _merge(att, ag, o_f, o_b, hg, x, gate, w_out_bf, hgain, ln_g, ln_b, tm):
    B, N, D = x.shape
    half = pl.BlockSpec((1, tm, BRANCH_W), lambda b, i: (b, i, 0))
    full = pl.BlockSpec((1, tm, D), lambda b, i: (b, i, 0))
    row = lambda w: pl.BlockSpec((1, w), lambda b, i: (0, 0))
    return pl.pallas_call(
        _merge_kernel,
        grid=(B, N // tm),
        in_specs=[half, half, half, half, half, full,
                  pl.BlockSpec((1, 1, D), lambda b, i: (b, 0, 0)),
                  pl.BlockSpec(w_out_bf.shape, lambda b, i: (0, 0)),
                  row(HEAD_W), row(D), row(D)],
        out_specs=full,
        out_shape=jax.ShapeDtypeStruct((B, N, D), jnp.float32),
        compiler_params=pltpu.CompilerParams(
            dimension_semantics=("parallel", "parallel"), vmem_limit_bytes=V7X_VMEM_LIMIT_BYTES),
        name="merge_out_proj_ln",
    )(att, ag, o_f, o_b, hg, x, gate, w_out_bf, hgain, ln_g, ln_b)


def kernel(x, c, ctx, c_ctx, w_ada, b_ada, w_in, w_out, diff_lambda, diff_subln_gain, hgrn_lower_bound,
           hgrn_norm_gain, ln_gain, ln_bias):
    B, N, D = x.shape
    assert DEPTH == 1 and w_ada.shape[0] == 1
    cvec = jnp.concatenate([c, c_ctx[None, :], jnp.zeros((SUBLANES - B - 1, D), c.dtype)], axis=0)
    mod = _modulation(cvec, w_ada[0], b_ada[0][None, :])
    shift, scale, gate = mod[:, :D], mod[:, D:2 * D], mod[:, 2 * D:]
    s1p = (1.0 + scale)[:, None, :]
    shift = shift[:, None, :]
    ctx_rows = jnp.full((B,), B, jnp.int32)

    w_bf = w_in[0].astype(jnp.bfloat16)
    k_c, v_c, hq_c, hi_c, hff_c, hfb_c = _in_projection(
        ctx, s1p[ctx_rows], shift[ctx_rows], w_bf, _CTX_GROUPS, tm=ctx.shape[1])
    q, k, v, ag, hq, hi, hff, hfb, hg = _in_projection(
        x, s1p[:B], shift[:B], w_bf, _LATENT_GROUPS, tm=512, rope_tables=_rope_tables(N))

    k_all = jnp.concatenate([k_c, k], axis=1)
    vt_all = jnp.concatenate([v_c, v], axis=2)
    att = _diff_attention(q, k_all, vt_all, diff_lambda, diff_subln_gain, tq=512, tk=1408)

    s_zero = jnp.zeros((B, 2, HEADS, HEAD_W, HEAD_W), jnp.float32)
    _, _, s_ctx = _hgrn2_bidir(hq_c, hi_c, hff_c, hfb_c, hgrn_lower_bound, s_zero, n_sub=2)
    o_f, o_b, _ = _hgrn2_bidir(hq, hi, hff, hfb, hgrn_lower_bound, s_ctx, n_sub=4)

    return _merge(att, ag, o_f, o_b, hg, x, gate[:B, None, :], w_out[0].astype(jnp.bfloat16),
                  hgrn_norm_gain, ln_gain, ln_bias, tm=512)
```

```python
import functools
import math

import numpy as np
import jax
import jax.numpy as jnp
from jax import lax
from jax.experimental import pallas as pl
from jax.experimental.pallas import tpu as pltpu

D_MODEL = 1024
DEPTH = 1
GRID_W = 64
HEADS = 4
HEAD_W = 128
MAP_D = 64
BRANCH_W = HEADS * HEAD_W
ROPE_BASE = 10000.0
ROPE_FREQS = MAP_D // 4
EPS = 1e-5
LAMBDA_INIT = 0.8 - 0.6 * math.exp(-0.3 * 0)
ALPHA = (2.0 * DEPTH) ** 0.25

V7X_VMEM_LIMIT_BYTES = 56 * 1024 * 1024
SUBLANES = 8
LANES = 128
ROW_TILE = 512
ATTN_TQ = 512
ATTN_MAX_TK = 1408
HGRN_CHUNKS_PER_STEP = 4
HGRN_CHUNK = 128
HGRN_LEVELS = 7

_NT = (((1,), (1,)), ((), ()))
_TN = (((0,), (0,)), ((), ()))


def _silu(x):
    return x * jax.nn.sigmoid(x)


def _mod_kernel(c_ref, w_ref, b_ref, o_ref):
    a = _silu(c_ref[...])
    o_ref[...] = jnp.dot(a, w_ref[...], preferred_element_type=jnp.float32,
                         precision=lax.Precision.HIGHEST) + b_ref[...]


def _modulation(cvec, w_ada, b_ada):
    rows, d = cvec.shape
    n_out = w_ada.shape[1]
    bn = 1024
    return pl.pallas_call(
        _mod_kernel,
        grid=(n_out // bn,),
        in_specs=[pl.BlockSpec((rows, d), lambda j: (0, 0)),
                  pl.BlockSpec((d, bn), lambda j: (0, j)),
                  pl.BlockSpec((1, bn), lambda j: (0, j))],
        out_specs=pl.BlockSpec((rows, bn), lambda j: (0, j)),
        out_shape=jax.ShapeDtypeStruct((rows, n_out), jnp.float32),
        compiler_params=pltpu.CompilerParams(vmem_limit_bytes=V7X_VMEM_LIMIT_BYTES),
        name="modulation",
    )(cvec, w_ada, b_ada)


_LATENT_GROUPS = ((0, "rope_q", jnp.bfloat16), (1, "rope", jnp.bfloat16), (2, "transposed", jnp.bfloat16),
                  (3, "plain", jnp.bfloat16), (4, "silu", jnp.float32), (5, "plain", jnp.bfloat16),
                  (6, "plain", jnp.float32), (7, "plain", jnp.float32), (8, "plain", jnp.bfloat16))
_CTX_GROUPS = ((1, "plain", jnp.bfloat16), (2, "transposed", jnp.bfloat16), (4, "silu", jnp.float32),
               (5, "plain", jnp.bfloat16), (6, "plain", jnp.float32), (7, "plain", jnp.float32))
Q_SCALE = math.log2(math.e) / math.sqrt(MAP_D)


def _proj_kernel(groups, use_rope, *refs):
    if use_rope:
        x_ref, s1p_ref, sh_ref, w_ref, cos_ref, sin_ref = refs[:6]
        out_refs = refs[6:]
    else:
        x_ref, s1p_ref, sh_ref, w_ref = refs[:4]
        out_refs = refs[4:]
    h = (x_ref[0] * s1p_ref[0] + sh_ref[0]).astype(jnp.bfloat16)
    if use_rope:
        cos = jnp.concatenate([cos_ref[...]] * HEADS, axis=1)
        sin = jnp.concatenate([sin_ref[...]] * HEADS, axis=1)
        lane = lax.broadcasted_iota(jnp.int32, cos.shape, 1)
        first_half = (lane & (2 * ROPE_FREQS - 1)) < ROPE_FREQS
    for (g, kind, dt), o_ref in zip(groups, out_refs):
        p = jnp.dot(h, w_ref[:, g * BRANCH_W:(g + 1) * BRANCH_W], preferred_element_type=jnp.float32)
        if kind in ("rope", "rope_q"):
            partner = jnp.where(first_half,
                                pltpu.roll(p, BRANCH_W - ROPE_FREQS, axis=1),
                                pltpu.roll(p, ROPE_FREQS, axis=1))
            p = p * cos + partner * sin
            if kind == "rope_q":
                p = p * Q_SCALE
        elif kind == "silu":
            p = _silu(p)
        elif kind == "transposed":
            p = p.T
        o_ref[0] = p.astype(dt)


def _in_projection(x, s1p, sh, w_bf, groups, tm, rope_tables=None):
    B, T, D = x.shape
    use_rope = rope_tables is not None
    in_specs = [pl.BlockSpec((1, tm, D), lambda b, i: (b, i, 0)),
                pl.BlockSpec((1, 1, D), lambda b, i: (b, 0, 0)),
                pl.BlockSpec((1, 1, D), lambda b, i: (b, 0, 0)),
                pl.BlockSpec(w_bf.shape, lambda b, i: (0, 0), pipeline_mode=pl.Buffered(1))]
    args = [x, s1p, sh, w_bf]
    if use_rope:
        in_specs += [pl.BlockSpec((tm, HEAD_W), lambda b, i: (i, 0))] * 2
        args += list(rope_tables)
    out_specs = [pl.BlockSpec((1, BRANCH_W, tm), lambda b, i: (b, 0, i)) if kind == "transposed"
                 else pl.BlockSpec((1, tm, BRANCH_W), lambda b, i: (b, i, 0)) for _, kind, _ in groups]
    out_shape = [jax.ShapeDtypeStruct((B, BRANCH_W, T) if kind == "transposed" else (B, T, BRANCH_W), dt)
                 for _, kind, dt in groups]
    return pl.pallas_call(
        functools.partial(_proj_kernel, groups, use_rope),
        grid=(B, T // tm),
        in_specs=in_specs, out_specs=out_specs, out_shape=out_shape,
        compiler_params=pltpu.CompilerParams(
            dimension_semantics=("parallel", "parallel"), vmem_limit_bytes=V7X_VMEM_LIMIT_BYTES),
        name="in_proj_rope" if use_rope else "in_proj_ctx",
    )(*args)


def _rope_tables(n_tokens):
    t = np.arange(n_tokens)
    pos = np.stack([t // GRID_W, t % GRID_W], axis=-1).astype(np.float32)
    inv_freq = jnp.asarray(ROPE_BASE, jnp.float32) ** (-jnp.arange(ROPE_FREQS, dtype=jnp.float32) / ROPE_FREQS)
    ang = jnp.asarray(pos)[:, :, None] * inv_freq
    ang = jnp.stack([ang, ang], axis=2).reshape(n_tokens, MAP_D)
    sign = np.where((np.arange(MAP_D) % (2 * ROPE_FREQS)) < ROPE_FREQS, -1.0, 1.0).astype(np.float32)
    cos = jnp.cos(ang)
    sin = jnp.sin(ang) * sign
    return jnp.concatenate([cos, cos], axis=1), jnp.concatenate([sin, sin], axis=1)


def _attn_kernel(tk, q_ref, k_ref, vt_ref, lam_ref, gain_ref, o_ref, s_ref, p_ref, acc_ref):
    qt = q_ref[0].astype(jnp.float32).T
    row = lax.broadcasted_iota(jnp.int32, qt.shape, 0)
    qm = tuple(jnp.where(sel, qt, 0.0).astype(jnp.bfloat16) for sel in (row < MAP_D, row >= MAP_D))
    tq = qt.shape[1]
    n_chunks = k_ref.shape[1] // tk

    def scores(j):
        kb = k_ref[0, pl.ds(pl.multiple_of(j * tk, tk), tk), :]
        smax = []
        for m in range(2):
            s = jnp.dot(kb, qm[m], preferred_element_type=jnp.float32)
            s_ref[m] = s
            smax.append(jnp.max(s, axis=0, keepdims=True))
        return tuple(smax)

    def weights(smax, ml):
        new_ml, corrs = [], []
        for m in range(2):
            mx, l = ml[m]
            mx_new = jnp.maximum(mx, smax[m])
            corr = jnp.exp2(mx - mx_new)
            p = jnp.exp2(s_ref[m] - mx_new)
            p_ref[m] = p.astype(jnp.bfloat16)
            new_ml.append((mx_new, l * corr + jnp.sum(p, axis=0, keepdims=True)))
            corrs.append(corr)
        return tuple(new_ml), tuple(corrs)

    def values(j, corr):
        vtb = vt_ref[0, :, pl.ds(pl.multiple_of(j * tk, tk), tk)]
        for m in range(2):
            acc_ref[m] = acc_ref[m] * corr[m] + jnp.dot(vtb, p_ref[m], preferred_element_type=jnp.float32)

    ml = tuple((jnp.full((1, tq), -jnp.inf, jnp.float32), jnp.zeros((1, tq), jnp.float32)) for _ in range(2))
    acc_ref[...] = jnp.zeros_like(acc_ref)
    ml, corr = weights(scores(0), ml)
    smax = scores(1)

    def body(j, carry):
        ml, smax, corr_prev = carry
        values(j - 1, corr_prev)
        ml, corr = weights(smax, ml)
        return ml, scores(j + 1), corr

    ml, smax, corr = lax.fori_loop(1, n_chunks - 1, body, (ml, smax, corr))
    values(n_chunks - 2, corr)
    ml, corr = weights(smax, ml)
    values(n_chunks - 1, corr)

    lp = lam_ref[0]
    lam = (jnp.exp(jnp.sum(lp[0:1] * lp[1:2], axis=-1, keepdims=True))
           - jnp.exp(jnp.sum(lp[2:3] * lp[3:4], axis=-1, keepdims=True)) + LAMBDA_INIT)
    (_, l0), (_, l1) = ml
    ot = acc_ref[0] / l0 - lam * (acc_ref[1] / l1)
    ot = ot * lax.rsqrt(jnp.mean(ot * ot, axis=0, keepdims=True) + EPS)
    o_ref[0] = (ot.T * gain_ref[...] * (1.0 - LAMBDA_INIT)).astype(o_ref.dtype)


def _key_chunk(n_keys):
    return max(t for t in range(LANES, ATTN_MAX_TK + 1, LANES) if n_keys % t == 0)


def _diff_attention(q, k, vt, diff_lambda, gain):
    B, N, _ = q.shape
    Tk = k.shape[1]
    tq, tk = ATTN_TQ, _key_chunk(Tk)
    assert N % tq == 0 and Tk // tk >= 3
    return pl.pallas_call(
        functools.partial(_attn_kernel, tk),
        grid=(B, HEADS, N // tq),
        in_specs=[pl.BlockSpec((1, tq, HEAD_W), lambda b, h, i: (b, i, h)),
                  pl.BlockSpec((1, Tk, HEAD_W), lambda b, h, i: (b, 0, h)),
                  pl.BlockSpec((1, HEAD_W, Tk), lambda b, h, i: (b, h, 0)),
                  pl.BlockSpec((1, 4, MAP_D), lambda b, h, i: (0, 0, 0)),
                  pl.BlockSpec((1, HEAD_W), lambda b, h, i: (0, 0))],
        out_specs=pl.BlockSpec((1, tq, HEAD_W), lambda b, h, i: (b, i, h)),
        out_shape=jax.ShapeDtypeStruct((B, N, BRANCH_W), jnp.bfloat16),
        scratch_shapes=[pltpu.VMEM((2, tk, tq), jnp.float32), pltpu.VMEM((2, tk, tq), jnp.bfloat16),
                        pltpu.VMEM((2, HEAD_W, tq), jnp.float32)],
        compiler_params=pltpu.CompilerParams(
            dimension_semantics=("parallel", "parallel", "parallel"),
            vmem_limit_bytes=V7X_VMEM_LIMIT_BYTES),
        name="diff_attention",
    )(q, k, vt, diff_lambda, gain)


def _level_maps():
    t = np.arange(HGRN_CHUNK)[:, None]
    s = np.arange(HGRN_CHUNK)[None, :]
    x = t ^ s
    lvl = np.where(x > 0, np.floor(np.log2(np.maximum(x, 1))).astype(np.int32) + 1, 0)
    lvl = np.where(s > t, -1, lvl).astype(np.int32)
    return np.stack([lvl, lvl.T])


def _shift_down(x, s):
    return pltpu.roll(x, s, axis=1)


def _shift_up(x, s):
    return pltpu.roll(x, SUBLANES - s, axis=1)


def _neg_abs(x):
    bits = lax.bitcast_convert_type(x, jnp.uint32) | jnp.uint32(0x80000000)
    return lax.bitcast_convert_type(bits, jnp.float32)


def _hgrn_chain(q, z, v_bf, lb, st, masks, reverse):
    C = HGRN_CHUNK
    groups = C // SUBLANES
    sig = jax.nn.sigmoid(z)
    g = jnp.log2(lb + (1.0 - lb) * sig)
    kk = (1.0 - lb) * (1.0 - sig)
    grouped = (groups, SUBLANES, HEAD_W)
    sub = lax.broadcasted_iota(jnp.int32, grouped, 1)

    bg = g.reshape(grouped)
    for s in (1, 2, 4):
        if reverse:
            bg = bg + jnp.where(sub < SUBLANES - s, _shift_up(bg, s), 0.0)
        else:
            bg = bg + jnp.where(sub >= s, _shift_down(bg, s), 0.0)
    parts = [bg[r] for r in range(groups)]
    order = range(groups - 2, -1, -1) if reverse else range(1, groups)
    for r in order:
        prev = parts[r + 1][0:1, :] if reverse else parts[r - 1][SUBLANES - 1:SUBLANES, :]
        parts[r] = parts[r] + prev
    b = jnp.concatenate(parts, axis=0)
    bg = b.reshape(grouped)

    q_bf = q.astype(jnp.bfloat16)
    kk_bf = kk.astype(jnp.bfloat16)

    def rows(x, r):
        return x[r * SUBLANES:(r + 1) * SUBLANES, :]

    a_parts = [None] * groups

    own = bg
    nbr = _shift_up(bg, 1) if reverse else _shift_down(bg, 1)
    for j in range(1, HGRN_LEVELS + 1):
        h = 1 << (j - 1)
        if 2 * h <= SUBLANES:
            upper = (sub & h) != 0
            ref = jnp.where(upper, own, nbr) if reverse else jnp.where(upper, nbr, own)
            w = jnp.exp2(_neg_abs(bg - ref)).reshape(C, HEAD_W).astype(jnp.bfloat16)
            if 4 * h <= SUBLANES:
                if reverse:
                    own = jnp.where(upper, _shift_down(own, h), own)
                    nbr = jnp.where(upper, nbr, _shift_up(nbr, h))
                else:
                    own = jnp.where(upper, own, _shift_up(own, h))
                    nbr = jnp.where(upper, _shift_down(nbr, h), nbr)
            pm = lax.dot_general(q_bf * w, kk_bf * w, _NT, preferred_element_type=jnp.float32)
            a_parts = [jnp.where(rows(masks[j], r), rows(pm, r), 0.0 if a_parts[r] is None else a_parts[r])
                       for r in range(groups)]
        else:
            n_blk, half_groups = C // (2 * h), h // SUBLANES
            blocked = (n_blk, 2 * h, HEAD_W)
            b_blk = b.reshape(blocked)
            ref = b_blk[:, h:h + 1, :] if reverse else b_blk[:, h - 1:h, :]
            w_blk = jnp.exp2(_neg_abs(b_blk - ref))
            q_half, k_half = (slice(0, h), slice(h, 2 * h)) if reverse else (slice(h, 2 * h), slice(0, h))
            q_t = (q.reshape(blocked)[:, q_half] * w_blk[:, q_half]).reshape(C // 2, HEAD_W)
            k_t = kk.reshape(blocked)[:, k_half] * w_blk[:, k_half]
            zeros = jnp.zeros_like(k_t)
            k_t = jnp.concatenate([zeros, k_t] if reverse else [k_t, zeros], axis=1).reshape(C, HEAD_W)
            pm = lax.dot_general(q_t.astype(jnp.bfloat16), k_t.astype(jnp.bfloat16), _NT,
                                 preferred_element_type=jnp.float32)
            for blk in range(n_blk):
                for i in range(half_groups):
                    r = blk * 2 * half_groups + (0 if reverse else half_groups) + i
                    piece = rows(pm, blk * half_groups + i)
                    if n_blk == 1:
                        a_parts[r] = a_parts[r] + piece
                    else:
                        a_parts[r] = jnp.where(rows(masks[j], r), piece, a_parts[r])
    a = jnp.concatenate(a_parts, axis=0)

    o = jnp.dot(a.astype(jnp.bfloat16), v_bf, preferred_element_type=jnp.float32)
    o = o + lax.dot_general(q_bf * jnp.exp2(b).astype(jnp.bfloat16), st.astype(jnp.bfloat16), _NT,
                            preferred_element_type=jnp.float32)
    o = o + jnp.sum(q * kk, axis=-1, keepdims=True) * v_bf.astype(jnp.float32)
    b_tot = b[0:1, :] if reverse else b[C - 1:C, :]
    k_hat = kk_bf * jnp.exp2(b_tot - b).astype(jnp.bfloat16)
    st_new = st * jnp.exp2(b_tot) + lax.dot_general(v_bf, k_hat, _TN, preferred_element_type=jnp.float32)
    return o, st_new


def _hgrn_kernel(n_sub, qf_ref, vf_ref, zf_ref, qb_ref, vb_ref, zb_ref, lbp_ref, lvl_ref, s0_ref,
                 of_ref, ob_ref, sfin_ref, st_ref):
    i = pl.program_id(1)
    C = HGRN_CHUNK

    @pl.when(i == 0)
    def _():
        st_ref[...] = s0_ref[0]

    dirs = ((qf_ref, vf_ref, zf_ref, of_ref), (qb_ref, vb_ref, zb_ref, ob_ref))
    masks, lbs = [], []
    for d in range(2):
        lvl = lvl_ref[d]
        masks.append([lvl == j for j in range(HGRN_LEVELS + 1)])
        p0, p1 = lbp_ref[d, 0:1, :], lbp_ref[d, 1:2, :]
        pm = jnp.maximum(p0, p1)
        e0 = jnp.exp(p0 - pm)
        lbs.append(e0 / (e0 + jnp.exp(p1 - pm)))
    states = [[st_ref[d, hh] for hh in range(HEADS)] for d in range(2)]
    for sub in range(n_sub):
        for d, (q_ref, v_ref, z_ref, o_ref) in enumerate(dirs):
            c = n_sub - 1 - sub if d == 1 else sub
            tok = slice(c * C, (c + 1) * C)
            for hh in range(HEADS):
                cols = slice(hh * HEAD_W, (hh + 1) * HEAD_W)
                o, states[d][hh] = _hgrn_chain(q_ref[0, tok, cols], z_ref[0, tok, cols], v_ref[0, tok, cols],
                                               lbs[d][:, cols], states[d][hh], masks[d], reverse=(d == 1))
                o_ref[0, tok, cols] = o.astype(o_ref.dtype)
    for d in range(2):
        for hh in range(HEADS):
            st_ref[d, hh] = states[d][hh]

    @pl.when(i == pl.num_programs(1) - 1)
    def _():
        sfin_ref[0] = st_ref[...]


def _hgrn2_bidir(hq, hi, hff, hfb, lb_param, s0, n_sub):
    B, T, _ = hq.shape
    rows = HGRN_CHUNK * n_sub
    n = T // rows
    assert n * rows == T
    fwd = pl.BlockSpec((1, rows, BRANCH_W), lambda b, i: (b, i, 0))
    bwd = pl.BlockSpec((1, rows, BRANCH_W), lambda b, i: (b, n - 1 - i, 0))
    st_spec = pl.BlockSpec((1, 2, HEADS, HEAD_W, HEAD_W), lambda b, i: (b, 0, 0, 0, 0))
    lvl = jnp.asarray(_level_maps())
    return pl.pallas_call(
        functools.partial(_hgrn_kernel, n_sub),
        grid=(B, n),
        in_specs=[fwd, fwd, fwd, bwd, bwd, bwd,
                  pl.BlockSpec(lb_param.shape, lambda b, i: (0, 0, 0)),
                  pl.BlockSpec(lvl.shape, lambda b, i: (0, 0, 0)),
                  st_spec],
        out_specs=[fwd, bwd, st_spec],
        out_shape=[jax.ShapeDtypeStruct((B, T, BRANCH_W), jnp.bfloat16),
                   jax.ShapeDtypeStruct((B, T, BRANCH_W), jnp.bfloat16),
                   jax.ShapeDtypeStruct((B, 2, HEADS, HEAD_W, HEAD_W), jnp.float32)],
        scratch_shapes=[pltpu.VMEM((2, HEADS, HEAD_W, HEAD_W), jnp.float32)],
        compiler_params=pltpu.CompilerParams(
            dimension_semantics=("parallel", "arbitrary"), vmem_limit_bytes=V7X_VMEM_LIMIT_BYTES),
        name="hgrn2_bidir",
    )(hq, hi, hff, hq, hi, hfb, lb_param, lvl, s0)


def _merge_kernel(att_ref, ag_ref, of_ref, ob_ref, hg_ref, x_ref, gate_ref, w_ref, hgain_ref, lng_ref, lnb_ref,
                  o_ref):
    f32 = jnp.float32
    att = att_ref[0].astype(f32) * _silu(ag_ref[0].astype(f32))
    o = of_ref[0].astype(f32) + ob_ref[0].astype(f32)
    parts = []
    for hh in range(HEADS):
        oh = o[:, hh * HEAD_W:(hh + 1) * HEAD_W]
        parts.append(oh * lax.rsqrt(jnp.mean(oh * oh, axis=-1, keepdims=True) + EPS) * hgain_ref[...])
    hg = jnp.concatenate(parts, axis=1) * _silu(hg_ref[0].astype(f32))
    y_in = jnp.concatenate([att, hg], axis=1).astype(jnp.bfloat16)
    y = jnp.dot(y_in, w_ref[...], preferred_element_type=jnp.float32)
    u = ALPHA * x_ref[0] + gate_ref[0] * y
    mu = jnp.mean(u, axis=-1, keepdims=True)
    uc = u - mu
    var = jnp.mean(uc * uc, axis=-1, keepdims=True)
    o_ref[0] = uc * lax.rsqrt(var + EPS) * lng_ref[...] + lnb_ref[...]


def _merge(att, ag, o_f, o_b, hg, x, gate, w_out_bf, hgain, ln_g, ln_b, tm):
    B, N, D = x.shape
    half = pl.BlockSpec((1, tm, BRANCH_W), lambda b, i: (b, i, 0))
    full = pl.BlockSpec((1, tm, D), lambda b, i: (b, i, 0))
    row = lambda w: pl.BlockSpec((1, w), lambda b, i: (0, 0))
    return pl.pallas_call(
        _merge_kernel,
        grid=(B, N // tm),
        in_specs=[half, half, half, half, half, full,
                  pl.BlockSpec((1, 1, D), lambda b, i: (b, 0, 0)),
                  pl.BlockSpec(w_out_bf.shape, lambda b, i: (0, 0)),
                  row(HEAD_W), row(D), row(D)],
        out_specs=full,
        out_shape=jax.ShapeDtypeStruct((B, N, D), jnp.float32),
        compiler_params=pltpu.CompilerParams(
            dimension_semantics=("parallel", "parallel"), vmem_limit_bytes=V7X_VMEM_LIMIT_BYTES),
        name="merge_out_proj_ln",
    )(att, ag, o_f, o_b, hg, x, gate, w_out_bf, hgain, ln_g, ln_b)


def kernel(x, c, ctx, c_ctx, w_ada, b_ada, w_in, w_out, diff_lambda, diff_subln_gain, hgrn_lower_bound,
           hgrn_norm_gain, ln_gain, ln_bias):
    B, N, D = x.shape
    assert DEPTH == 1 and w_ada.shape[0] == 1
    cvec = jnp.concatenate([c, c_ctx[None, :], jnp.zeros((SUBLANES - B - 1, D), c.dtype)], axis=0)
    mod = _modulation(cvec, w_ada[0], b_ada[0][None, :])
    shift, scale, gate = mod[:, :D], mod[:, D:2 * D], mod[:, 2 * D:]
    s1p = (1.0 + scale)[:, None, :]
    shift = shift[:, None, :]
    ctx_rows = jnp.full((B,), B, jnp.int32)

    w_bf = w_in[0].astype(jnp.bfloat16)
    k_c, v_c, hq_c, hi_c, hff_c, hfb_c = _in_projection(
        ctx, s1p[ctx_rows], shift[ctx_rows], w_bf, _CTX_GROUPS, tm=ctx.shape[1])
    q, k, v, ag, hq, hi, hff, hfb, hg = _in_projection(
        x, s1p[:B], shift[:B], w_bf, _LATENT_GROUPS, tm=ROW_TILE, rope_tables=_rope_tables(N))

    k_all = jnp.concatenate([k_c, k], axis=1)
    vt_all = jnp.concatenate([v_c, v], axis=2)
    att = _diff_attention(q, k_all, vt_all, diff_lambda, diff_subln_gain)

    s_zero = jnp.zeros((B, 2, HEADS, HEAD_W, HEAD_W), jnp.float32)
    _, _, s_ctx = _hgrn2_bidir(hq_c, hi_c, hff_c, hfb_c, hgrn_lower_bound, s_zero, n_sub=ctx.shape[1] // HGRN_CHUNK)
    o_f, o_b, _ = _hgrn2_bidir(hq, hi, hff, hfb, hgrn_lower_bound, s_ctx, n_sub=HGRN_CHUNKS_PER_STEP)

    return _merge(att, ag, o_f, o_b, hg, x, gate[:B, None, :], w_out[0].astype(jnp.bfloat16),
                  hgrn_norm_gain, ln_gain, ln_bias, tm=ROW_TILE)
```

```python
import functools
import math

import numpy as np
import jax
import jax.numpy as jnp
from jax import lax
from jax.experimental import pallas as pl
from jax.experimental.pallas import tpu as pltpu

D_MODEL = 1024
DEPTH = 1
GRID_W = 64
HEADS = 4
HEAD_W = 128
MAP_D = 64
BRANCH_W = HEADS * HEAD_W
ROPE_BASE = 10000.0
ROPE_FREQS = MAP_D // 4
EPS = 1e-5
LAMBDA_INIT = 0.8 - 0.6 * math.exp(-0.3 * 0)
ALPHA = (2.0 * DEPTH) ** 0.25

V7X_VMEM_LIMIT_BYTES = 56 * 1024 * 1024
SUBLANES = 8
LANES = 128
ROW_TILE = 512
ATTN_TQ = 512
ATTN_MAX_TK = 1408
HGRN_CHUNKS_PER_STEP = 4
HGRN_CHUNK = 128
HGRN_LEVELS = 7

_NT = (((1,), (1,)), ((), ()))
_TN = (((0,), (0,)), ((), ()))


def _silu(x):
    return x * jax.nn.sigmoid(x)


def _mod_kernel(c_ref, w_ref, b_ref, o_ref):
    a = _silu(c_ref[...])
    o_ref[...] = jnp.dot(a, w_ref[...], preferred_element_type=jnp.float32,
                         precision=lax.Precision.HIGHEST) + b_ref[...]


def _modulation(cvec, w_ada, b_ada):
    rows, d = cvec.shape
    n_out = w_ada.shape[1]
    bn = 1024
    return pl.pallas_call(
        _mod_kernel,
        grid=(n_out // bn,),
        in_specs=[pl.BlockSpec((rows, d), lambda j: (0, 0)),
                  pl.BlockSpec((d, bn), lambda j: (0, j)),
                  pl.BlockSpec((1, bn), lambda j: (0, j))],
        out_specs=pl.BlockSpec((rows, bn), lambda j: (0, j)),
        out_shape=jax.ShapeDtypeStruct((rows, n_out), jnp.float32),
        compiler_params=pltpu.CompilerParams(vmem_limit_bytes=V7X_VMEM_LIMIT_BYTES),
        name="modulation",
    )(cvec, w_ada, b_ada)


_LATENT_GROUPS = ((0, "rope_q", jnp.bfloat16), (1, "rope", jnp.bfloat16), (2, "transposed", jnp.bfloat16),
                  (3, "plain", jnp.bfloat16), (4, "silu", jnp.float32), (5, "plain", jnp.bfloat16),
                  (6, "plain", jnp.float32), (7, "plain", jnp.float32), (8, "plain", jnp.bfloat16))
_CTX_GROUPS = ((1, "plain", jnp.bfloat16), (2, "transposed", jnp.bfloat16), (4, "silu", jnp.float32),
               (5, "plain", jnp.bfloat16), (6, "plain", jnp.float32), (7, "plain", jnp.float32))
Q_SCALE = math.log2(math.e) / math.sqrt(MAP_D)


def _proj_kernel(groups, use_rope, *refs):
    if use_rope:
        x_ref, s1p_ref, sh_ref, w_ref, cos_ref, sin_ref = refs[:6]
        out_refs = refs[6:]
    else:
        x_ref, s1p_ref, sh_ref, w_ref = refs[:4]
        out_refs = refs[4:]
    h = (x_ref[0] * s1p_ref[0] + sh_ref[0]).astype(jnp.bfloat16)
    if use_rope:
        cos = jnp.concatenate([cos_ref[...]] * HEADS, axis=1)
        sin = jnp.concatenate([sin_ref[...]] * HEADS, axis=1)
        lane = lax.broadcasted_iota(jnp.int32, cos.shape, 1)
        first_half = (lane & (2 * ROPE_FREQS - 1)) < ROPE_FREQS
    for (g, kind, dt), o_ref in zip(groups, out_refs):
        p = jnp.dot(h, w_ref[:, g * BRANCH_W:(g + 1) * BRANCH_W], preferred_element_type=jnp.float32)
        if kind in ("rope", "rope_q"):
            partner = jnp.where(first_half,
                                pltpu.roll(p, BRANCH_W - ROPE_FREQS, axis=1),
                                pltpu.roll(p, ROPE_FREQS, axis=1))
            p = p * cos + partner * sin
            if kind == "rope_q":
                p = p * Q_SCALE
        elif kind == "silu":
            p = _silu(p)
        elif kind == "transposed":
            p = p.T
        o_ref[0] = p.astype(dt)


def _in_projection(x, s1p, sh, w_bf, groups, tm, rope_tables=None):
    B, T, D = x.shape
    use_rope = rope_tables is not None
    in_specs = [pl.BlockSpec((1, tm, D), lambda b, i: (b, i, 0)),
                pl.BlockSpec((1, 1, D), lambda b, i: (b, 0, 0)),
                pl.BlockSpec((1, 1, D), lambda b, i: (b, 0, 0)),
                pl.BlockSpec(w_bf.shape, lambda b, i: (0, 0), pipeline_mode=pl.Buffered(1))]
    args = [x, s1p, sh, w_bf]
    if use_rope:
        in_specs += [pl.BlockSpec((tm, HEAD_W), lambda b, i: (i, 0))] * 2
        args += list(rope_tables)
    out_specs = [pl.BlockSpec((1, BRANCH_W, tm), lambda b, i: (b, 0, i)) if kind == "transposed"
                 else pl.BlockSpec((1, tm, BRANCH_W), lambda b, i: (b, i, 0)) for _, kind, _ in groups]
    out_shape = [jax.ShapeDtypeStruct((B, BRANCH_W, T) if kind == "transposed" else (B, T, BRANCH_W), dt)
                 for _, kind, dt in groups]
    return pl.pallas_call(
        functools.partial(_proj_kernel, groups, use_rope),
        grid=(B, T // tm),
        in_specs=in_specs, out_specs=out_specs, out_shape=out_shape,
        compiler_params=pltpu.CompilerParams(
            dimension_semantics=("parallel", "parallel"), vmem_limit_bytes=V7X_VMEM_LIMIT_BYTES),
        name="in_proj_rope" if use_rope else "in_proj_ctx",
    )(*args)


def _rope_tables(n_tokens):
    t = np.arange(n_tokens)
    pos = np.stack([t // GRID_W, t % GRID_W], axis=-1).astype(np.float32)
    inv_freq = jnp.asarray(ROPE_BASE, jnp.float32) ** (-jnp.arange(ROPE_FREQS, dtype=jnp.float32) / ROPE_FREQS)
    ang = jnp.asarray(pos)[:, :, None] * inv_freq
    ang = jnp.stack([ang, ang], axis=2).reshape(n_tokens, MAP_D)
    sign = np.where((np.arange(MAP_D) % (2 * ROPE_FREQS)) < ROPE_FREQS, -1.0, 1.0).astype(np.float32)
    cos = jnp.cos(ang)
    sin = jnp.sin(ang) * sign
    return jnp.concatenate([cos, cos], axis=1), jnp.concatenate([sin, sin], axis=1)


def _attn_kernel(tk, q_ref, kc_ref, vtc_ref, kl_ref, vtl_ref, lam_ref, gain_ref, o_ref,
                 k_ref, vt_ref, s_ref, p_ref, acc_ref):
    t_ctx = kc_ref.shape[1]

    @pl.when(pl.program_id(2) == 0)
    def _():
        k_ref[:t_ctx, :] = kc_ref[0]
        k_ref[t_ctx:, :] = kl_ref[0]
        vt_ref[:, :t_ctx] = vtc_ref[0]
        vt_ref[:, t_ctx:] = vtl_ref[0]

    qt = q_ref[0].astype(jnp.float32).T
    row = lax.broadcasted_iota(jnp.int32, qt.shape, 0)
    qm = tuple(jnp.where(sel, qt, 0.0).astype(jnp.bfloat16) for sel in (row < MAP_D, row >= MAP_D))
    tq = qt.shape[1]
    n_chunks = k_ref.shape[0] // tk

    def scores(j):
        kb = k_ref[pl.ds(pl.multiple_of(j * tk, tk), tk), :]
        smax = []
        for m in range(2):
            s = jnp.dot(kb, qm[m], preferred_element_type=jnp.float32)
            s_ref[m] = s
            smax.append(jnp.max(s, axis=0, keepdims=True))
        return tuple(smax)

    def weights(smax, ml):
        new_ml, corrs = [], []
        for m in range(2):
            mx, l = ml[m]
            mx_new = jnp.maximum(mx, smax[m])
            corr = jnp.exp2(mx - mx_new)
            p = jnp.exp2(s_ref[m] - mx_new)
            p_ref[m] = p.astype(jnp.bfloat16)
            new_ml.append((mx_new, l * corr + jnp.sum(p, axis=0, keepdims=True)))
            corrs.append(corr)
        return tuple(new_ml), tuple(corrs)

    def values(j, corr):
        vtb = vt_ref[:, pl.ds(pl.multiple_of(j * tk, tk), tk)]
        for m in range(2):
            acc_ref[m] = acc_ref[m] * corr[m] + jnp.dot(vtb, p_ref[m], preferred_element_type=jnp.float32)

    ml = tuple((jnp.full((1, tq), -jnp.inf, jnp.float32), jnp.zeros((1, tq), jnp.float32)) for _ in range(2))
    acc_ref[...] = jnp.zeros_like(acc_ref)
    ml, corr = weights(scores(0), ml)
    smax = scores(1)

    def body(j, carry):
        ml, smax, corr_prev = carry
        values(j - 1, corr_prev)
        ml, corr = weights(smax, ml)
        return ml, scores(j + 1), corr

    ml, smax, corr = lax.fori_loop(1, n_chunks - 1, body, (ml, smax, corr))
    values(n_chunks - 2, corr)
    ml, corr = weights(smax, ml)
    values(n_chunks - 1, corr)

    lp = lam_ref[0]
    lam = (jnp.exp(jnp.sum(lp[0:1] * lp[1:2], axis=-1, keepdims=True))
           - jnp.exp(jnp.sum(lp[2:3] * lp[3:4], axis=-1, keepdims=True)) + LAMBDA_INIT)
    (_, l0), (_, l1) = ml
    ot = acc_ref[0] / l0 - lam * (acc_ref[1] / l1)
    ot = ot * lax.rsqrt(jnp.mean(ot * ot, axis=0, keepdims=True) + EPS)
    o_ref[0] = (ot.T * gain_ref[...] * (1.0 - LAMBDA_INIT)).astype(o_ref.dtype)


def _key_chunk(n_keys):
    return max(t for t in range(LANES, ATTN_MAX_TK + 1, LANES) if n_keys % t == 0)


def _diff_attention(q, k_ctx, vt_ctx, k, vt, diff_lambda, gain):
    B, N, _ = q.shape
    T_ctx = k_ctx.shape[1]
    Tk = T_ctx + N
    tq, tk = ATTN_TQ, _key_chunk(Tk)
    assert N % tq == 0 and Tk // tk >= 3 and T_ctx % LANES == 0
    return pl.pallas_call(
        functools.partial(_attn_kernel, tk),
        grid=(B, HEADS, N // tq),
        in_specs=[pl.BlockSpec((1, tq, HEAD_W), lambda b, h, i: (b, i, h)),
                  pl.BlockSpec((1, T_ctx, HEAD_W), lambda b, h, i: (b, 0, h)),
                  pl.BlockSpec((1, HEAD_W, T_ctx), lambda b, h, i: (b, h, 0)),
                  pl.BlockSpec((1, N, HEAD_W), lambda b, h, i: (b, 0, h)),
                  pl.BlockSpec((1, HEAD_W, N), lambda b, h, i: (b, h, 0)),
                  pl.BlockSpec((1, 4, MAP_D), lambda b, h, i: (0, 0, 0)),
                  pl.BlockSpec((1, HEAD_W), lambda b, h, i: (0, 0))],
        out_specs=pl.BlockSpec((1, tq, HEAD_W), lambda b, h, i: (b, i, h)),
        out_shape=jax.ShapeDtypeStruct((B, N, BRANCH_W), jnp.bfloat16),
        scratch_shapes=[pltpu.VMEM((Tk, HEAD_W), jnp.bfloat16), pltpu.VMEM((HEAD_W, Tk), jnp.bfloat16),
                        pltpu.VMEM((2, tk, tq), jnp.float32), pltpu.VMEM((2, tk, tq), jnp.bfloat16),
                        pltpu.VMEM((2, HEAD_W, tq), jnp.float32)],
        compiler_params=pltpu.CompilerParams(
            dimension_semantics=("parallel", "parallel", "arbitrary"),
            vmem_limit_bytes=V7X_VMEM_LIMIT_BYTES),
        name="diff_attention",
    )(q, k_ctx, vt_ctx, k, vt, diff_lambda, gain)


def _level_maps():
    t = np.arange(HGRN_CHUNK)[:, None]
    s = np.arange(HGRN_CHUNK)[None, :]
    x = t ^ s
    lvl = np.where(x > 0, np.floor(np.log2(np.maximum(x, 1))).astype(np.int32) + 1, 0)
    lvl = np.where(s > t, -1, lvl).astype(np.int32)
    return np.stack([lvl, lvl.T])


def _shift_down(x, s):
    return pltpu.roll(x, s, axis=1)


def _shift_up(x, s):
    return pltpu.roll(x, SUBLANES - s, axis=1)


def _neg_abs(x):
    bits = lax.bitcast_convert_type(x, jnp.uint32) | jnp.uint32(0x80000000)
    return lax.bitcast_convert_type(bits, jnp.float32)


def _hgrn_chain(q, z, v_bf, lb, st, masks, reverse):
    C = HGRN_CHUNK
    groups = C // SUBLANES
    sig = jax.nn.sigmoid(z)
    g = jnp.log2(lb + (1.0 - lb) * sig)
    kk = (1.0 - lb) * (1.0 - sig)
    grouped = (groups, SUBLANES, HEAD_W)
    sub = lax.broadcasted_iota(jnp.int32, grouped, 1)

    bg = g.reshape(grouped)
    for s in (1, 2, 4):
        if reverse:
            bg = bg + jnp.where(sub < SUBLANES - s, _shift_up(bg, s), 0.0)
        else:
            bg = bg + jnp.where(sub >= s, _shift_down(bg, s), 0.0)
    parts = [bg[r] for r in range(groups)]
    order = range(groups - 2, -1, -1) if reverse else range(1, groups)
    for r in order:
        prev = parts[r + 1][0:1, :] if reverse else parts[r - 1][SUBLANES - 1:SUBLANES, :]
        parts[r] = parts[r] + prev
    b = jnp.concatenate(parts, axis=0)
    bg = b.reshape(grouped)

    q_bf = q.astype(jnp.bfloat16)
    kk_bf = kk.astype(jnp.bfloat16)

    def rows(x, r):
        return x[r * SUBLANES:(r + 1) * SUBLANES, :]

    a_parts = [None] * groups

    own = bg
    nbr = _shift_up(bg, 1) if reverse else _shift_down(bg, 1)
    for j in range(1, HGRN_LEVELS + 1):
        h = 1 << (j - 1)
        if 2 * h <= SUBLANES:
            upper = (sub & h) != 0
            ref = jnp.where(upper, own, nbr) if reverse else jnp.where(upper, nbr, own)
            w = jnp.exp2(_neg_abs(bg - ref)).reshape(C, HEAD_W).astype(jnp.bfloat16)
            if 4 * h <= SUBLANES:
                if reverse:
                    own = jnp.where(upper, _shift_down(own, h), own)
                    nbr = jnp.where(upper, nbr, _shift_up(nbr, h))
                else:
                    own = jnp.where(upper, own, _shift_up(own, h))
                    nbr = jnp.where(upper, _shift_down(nbr, h), nbr)
            pm = lax.dot_general(q_bf * w, kk_bf * w, _NT, preferred_element_type=jnp.float32)
            a_parts = [jnp.where(rows(masks[j], r), rows(pm, r), 0.0 if a_parts[r] is None else a_parts[r])
                       for r in range(groups)]
        else:
            n_blk, half_groups = C // (2 * h), h // SUBLANES
            blocked = (n_blk, 2 * h, HEAD_W)
            b_blk = b.reshape(blocked)
            ref = b_blk[:, h:h + 1, :] if reverse else b_blk[:, h - 1:h, :]
            w_blk = jnp.exp2(_neg_abs(b_blk - ref))
            q_half, k_half = (slice(0, h), slice(h, 2 * h)) if reverse else (slice(h, 2 * h), slice(0, h))
            q_t = (q.reshape(blocked)[:, q_half] * w_blk[:, q_half]).reshape(C // 2, HEAD_W)
            k_t = kk.reshape(blocked)[:, k_half] * w_blk[:, k_half]
            zeros = jnp.zeros_like(k_t)
            k_t = jnp.concatenate([zeros, k_t] if reverse else [k_t, zeros], axis=1).reshape(C, HEAD_W)
            pm = lax.dot_general(q_t.astype(jnp.bfloat16), k_t.astype(jnp.bfloat16), _NT,
                                 preferred_element_type=jnp.float32)
            for blk in range(n_blk):
                for i in range(half_groups):
                    r = blk * 2 * half_groups + (0 if reverse else half_groups) + i
                    piece = rows(pm, blk * half_groups + i)
                    if n_blk == 1:
                        a_parts[r] = a_parts[r] + piece
                    else:
                        a_parts[r] = jnp.where(rows(masks[j], r), piece, a_parts[r])
    a = jnp.concatenate(a_parts, axis=0)

    o = jnp.dot(a.astype(jnp.bfloat16), v_bf, preferred_element_type=jnp.float32)
    o = o + lax.dot_general(q_bf * jnp.exp2(b).astype(jnp.bfloat16), st.astype(jnp.bfloat16), _NT,
                            preferred_element_type=jnp.float32)
    o = o + jnp.sum(q * kk, axis=-1, keepdims=True) * v_bf.astype(jnp.float32)
    b_tot = b[0:1, :] if reverse else b[C - 1:C, :]
    k_hat = kk_bf * jnp.exp2(b_tot - b).astype(jnp.bfloat16)
    st_new = st * jnp.exp2(b_tot) + lax.dot_general(v_bf, k_hat, _TN, preferred_element_type=jnp.float32)
    return o, st_new


def _hgrn_kernel(n_sub, qf_ref, vf_ref, zf_ref, qb_ref, vb_ref, zb_ref, lbp_ref, lvl_ref, s0_ref,
                 of_ref, ob_ref, sfin_ref, st_ref):
    i = pl.program_id(1)
    C = HGRN_CHUNK

    @pl.when(i == 0)
    def _():
        st_ref[...] = s0_ref[0]

    dirs = ((qf_ref, vf_ref, zf_ref, of_ref), (qb_ref, vb_ref, zb_ref, ob_ref))
    masks, lbs = [], []
    for d in range(2):
        lvl = lvl_ref[d]
        masks.append([lvl == j for j in range(HGRN_LEVELS + 1)])
        p0, p1 = lbp_ref[d, 0:1, :], lbp_ref[d, 1:2, :]
        pm = jnp.maximum(p0, p1)
        e0 = jnp.exp(p0 - pm)
        lbs.append(e0 / (e0 + jnp.exp(p1 - pm)))
    states = [[st_ref[d, hh] for hh in range(HEADS)] for d in range(2)]
    for sub in range(n_sub):
        for d, (q_ref, v_ref, z_ref, o_ref) in enumerate(dirs):
            c = n_sub - 1 - sub if d == 1 else sub
            tok = slice(c * C, (c + 1) * C)
            for hh in range(HEADS):
                cols = slice(hh * HEAD_W, (hh + 1) * HEAD_W)
                o, states[d][hh] = _hgrn_chain(q_ref[0, tok, cols], z_ref[0, tok, cols], v_ref[0, tok, cols],
                                               lbs[d][:, cols], states[d][hh], masks[d], reverse=(d == 1))
                o_ref[0, tok, cols] = o.astype(o_ref.dtype)
    for d in range(2):
        for hh in range(HEADS):
            st_ref[d, hh] = states[d][hh]

    @pl.when(i == pl.num_programs(1) - 1)
    def _():
        sfin_ref[0] = st_ref[...]


def _hgrn2_bidir(hq, hi, hff, hfb, lb_param, s0, n_sub):
    B, T, _ = hq.shape
    rows = HGRN_CHUNK * n_sub
    n = T // rows
    assert n * rows == T
    fwd = pl.BlockSpec((1, rows, BRANCH_W), lambda b, i: (b, i, 0))
    bwd = pl.BlockSpec((1, rows, BRANCH_W), lambda b, i: (b, n - 1 - i, 0))
    st_spec = pl.BlockSpec((1, 2, HEADS, HEAD_W, HEAD_W), lambda b, i: (b, 0, 0, 0, 0))
    lvl = jnp.asarray(_level_maps())
    return pl.pallas_call(
        functools.partial(_hgrn_kernel, n_sub),
        grid=(B, n),
        in_specs=[fwd, fwd, fwd, bwd, bwd, bwd,
                  pl.BlockSpec(lb_param.shape, lambda b, i: (0, 0, 0)),
                  pl.BlockSpec(lvl.shape, lambda b, i: (0, 0, 0)),
                  st_spec],
        out_specs=[fwd, bwd, st_spec],
        out_shape=[jax.ShapeDtypeStruct((B, T, BRANCH_W), jnp.bfloat16),
                   jax.ShapeDtypeStruct((B, T, BRANCH_W), jnp.bfloat16),
                   jax.ShapeDtypeStruct((B, 2, HEADS, HEAD_W, HEAD_W), jnp.float32)],
        scratch_shapes=[pltpu.VMEM((2, HEADS, HEAD_W, HEAD_W), jnp.float32)],
        compiler_params=pltpu.CompilerParams(
            dimension_semantics=("parallel", "arbitrary"), vmem_limit_bytes=V7X_VMEM_LIMIT_BYTES),
        name="hgrn2_bidir",
    )(hq, hi, hff, hq, hi, hfb, lb_param, lvl, s0)


def _merge_kernel(att_ref, ag_ref, of_ref, ob_ref, hg_ref, x_ref, gate_ref, w_ref, hgain_ref, lng_ref, lnb_ref,
                  o_ref):
    f32 = jnp.float32
    att = att_ref[0].astype(f32) * _silu(ag_ref[0].astype(f32))
    o = of_ref[0].astype(f32) + ob_ref[0].astype(f32)
    parts = []
    for hh in range(HEADS):
        oh = o[:, hh * HEAD_W:(hh + 1) * HEAD_W]
        parts.append(oh * lax.rsqrt(jnp.mean(oh * oh, axis=-1, keepdims=True) + EPS) * hgain_ref[...])
    hg = jnp.concatenate(parts, axis=1) * _silu(hg_ref[0].astype(f32))
    y_in = jnp.concatenate([att, hg], axis=1).astype(jnp.bfloat16)
    y = jnp.dot(y_in, w_ref[...], preferred_element_type=jnp.float32)
    u = ALPHA * x_ref[0] + gate_ref[0] * y
    mu = jnp.mean(u, axis=-1, keepdims=True)
    uc = u - mu
    var = jnp.mean(uc * uc, axis=-1, keepdims=True)
    o_ref[0] = uc * lax.rsqrt(var + EPS) * lng_ref[...] + lnb_ref[...]


def _merge(att, ag, o_f, o_b, hg, x, gate, w_out_bf, hgain, ln_g, ln_b, tm):
    B, N, D = x.shape
    half = pl.BlockSpec((1, tm, BRANCH_W), lambda b, i: (b, i, 0))
    full = pl.BlockSpec((1, tm, D), lambda b, i: (b, i, 0))
    row = lambda w: pl.BlockSpec((1, w), lambda b, i: (0, 0))
    return pl.pallas_call(
        _merge_kernel,
        grid=(B, N // tm),
        in_specs=[half, half, half, half, half, full,
                  pl.BlockSpec((1, 1, D), lambda b, i: (b, 0, 0)),
                  pl.BlockSpec(w_out_bf.shape, lambda b, i: (0, 0)),
                  row(HEAD_W), row(D), row(D)],
        out_specs=full,
        out_shape=jax.ShapeDtypeStruct((B, N, D), jnp.float32),
        compiler_params=pltpu.CompilerParams(
            dimension_semantics=("parallel", "parallel"), vmem_limit_bytes=V7X_VMEM_LIMIT_BYTES),
        name="merge_out_proj_ln",
    )(att, ag, o_f, o_b, hg, x, gate, w_out_bf, hgain, ln_g, ln_b)


def kernel(x, c, ctx, c_ctx, w_ada, b_ada, w_in, w_out, diff_lambda, diff_subln_gain, hgrn_lower_bound,
           hgrn_norm_gain, ln_gain, ln_bias):
    B, N, D = x.shape
    assert DEPTH == 1 and w_ada.shape[0] == 1
    cvec = jnp.concatenate([c, c_ctx[None, :], jnp.zeros((SUBLANES - B - 1, D), c.dtype)], axis=0)
    mod = _modulation(cvec, w_ada[0], b_ada[0][None, :])
    shift, scale, gate = mod[:, :D], mod[:, D:2 * D], mod[:, 2 * D:]
    s1p = (1.0 + scale)[:, None, :]
    shift = shift[:, None, :]
    ctx_rows = jnp.full((B,), B, jnp.int32)

    w_bf = w_in[0].astype(jnp.bfloat16)
    k_c, v_c, hq_c, hi_c, hff_c, hfb_c = _in_projection(
        ctx, s1p[ctx_rows], shift[ctx_rows], w_bf, _CTX_GROUPS, tm=ctx.shape[1])
    q, k, v, ag, hq, hi, hff, hfb, hg = _in_projection(
        x, s1p[:B], shift[:B], w_bf, _LATENT_GROUPS, tm=ROW_TILE, rope_tables=_rope_tables(N))

    att = _diff_attention(q, k_c, v_c, k, v, diff_lambda, diff_subln_gain)

    s_zero = jnp.zeros((B, 2, HEADS, HEAD_W, HEAD_W), jnp.float32)
    _, _, s_ctx = _hgrn2_bidir(hq_c, hi_c, hff_c, hfb_c, hgrn_lower_bound, s_zero, n_sub=ctx.shape[1] // HGRN_CHUNK)
    o_f, o_b, _ = _hgrn2_bidir(hq, hi, hff, hfb, hgrn_lower_bound, s_ctx, n_sub=HGRN_CHUNKS_PER_STEP)

    return _merge(att, ag, o_f, o_b, hg, x, gate[:B, None, :], w_out[0].astype(jnp.bfloat16),
                  hgrn_norm_gain, ln_gain, ln_bias, tm=ROW_TILE)
```

```python
import functools
import math

import numpy as np
import jax
import jax.numpy as jnp
from jax import lax
from jax.experimental import pallas as pl
from jax.experimental.pallas import tpu as pltpu

D_MODEL = 1024
DEPTH = 1
GRID_W = 64
HEADS = 4
HEAD_W = 128
MAP_D = 64
BRANCH_W = HEADS * HEAD_W
ROPE_BASE = 10000.0
ROPE_FREQS = MAP_D // 4
EPS = 1e-5
LAMBDA_INIT = 0.8 - 0.6 * math.exp(-0.3 * 0)
ALPHA = (2.0 * DEPTH) ** 0.25

V7X_VMEM_LIMIT_BYTES = 56 * 1024 * 1024
SUBLANES = 8
LANES = 128
ROW_TILE = 512
ATTN_TQ = 512
ATTN_MAX_TK = 1408
HGRN_CHUNKS_PER_STEP = 4
HGRN_CHUNK = 128
HGRN_LEVELS = 7

_NT = (((1,), (1,)), ((), ()))
_TN = (((0,), (0,)), ((), ()))


def _silu(x):
    return x * jax.nn.sigmoid(x)


def _mod_kernel(c_ref, w_ref, b_ref, o_ref):
    a = _silu(c_ref[...]).astype(jnp.bfloat16)
    o_ref[...] = jnp.dot(a, w_ref[...].astype(jnp.bfloat16), preferred_element_type=jnp.float32) + b_ref[...]


def _modulation(cvec, w_ada, b_ada):
    rows, d = cvec.shape
    n_out = w_ada.shape[1]
    bn = 1024
    return pl.pallas_call(
        _mod_kernel,
        grid=(n_out // bn,),
        in_specs=[pl.BlockSpec((rows, d), lambda j: (0, 0)),
                  pl.BlockSpec((d, bn), lambda j: (0, j)),
                  pl.BlockSpec((1, bn), lambda j: (0, j))],
        out_specs=pl.BlockSpec((rows, bn), lambda j: (0, j)),
        out_shape=jax.ShapeDtypeStruct((rows, n_out), jnp.float32),
        compiler_params=pltpu.CompilerParams(vmem_limit_bytes=V7X_VMEM_LIMIT_BYTES),
        name="modulation",
    )(cvec, w_ada, b_ada)


_LATENT_GROUPS = ((0, "rope_q", jnp.bfloat16), (1, "rope", jnp.bfloat16), (2, "transposed", jnp.bfloat16),
                  (3, "plain", jnp.bfloat16), (4, "silu", jnp.float32), (5, "plain", jnp.bfloat16),
                  (6, "plain", jnp.float32), (7, "plain", jnp.float32), (8, "plain", jnp.bfloat16))
_CTX_GROUPS = ((1, "plain", jnp.bfloat16), (2, "transposed", jnp.bfloat16), (4, "silu", jnp.float32),
               (5, "plain", jnp.bfloat16), (6, "plain", jnp.float32), (7, "plain", jnp.float32))
Q_SCALE = math.log2(math.e) / math.sqrt(MAP_D)


def _proj_kernel(groups, use_rope, *refs):
    if use_rope:
        x_ref, s1p_ref, sh_ref, w_ref, cos_ref, sin_ref = refs[:6]
        out_refs = refs[6:]
    else:
        x_ref, s1p_ref, sh_ref, w_ref = refs[:4]
        out_refs = refs[4:]
    h = (x_ref[0] * s1p_ref[0] + sh_ref[0]).astype(jnp.bfloat16)
    if use_rope:
        cos = jnp.concatenate([cos_ref[...]] * HEADS, axis=1)
        sin = jnp.concatenate([sin_ref[...]] * HEADS, axis=1)
        lane = lax.broadcasted_iota(jnp.int32, cos.shape, 1)
        first_half = (lane & (2 * ROPE_FREQS - 1)) < ROPE_FREQS
    for (g, kind, dt), o_ref in zip(groups, out_refs):
        p = jnp.dot(h, w_ref[:, g * BRANCH_W:(g + 1) * BRANCH_W], preferred_element_type=jnp.float32)
        if kind in ("rope", "rope_q"):
            partner = jnp.where(first_half,
                                pltpu.roll(p, BRANCH_W - ROPE_FREQS, axis=1),
                                pltpu.roll(p, ROPE_FREQS, axis=1))
            p = p * cos + partner * sin
            if kind == "rope_q":
                p = p * Q_SCALE
        elif kind == "silu":
            p = _silu(p)
        elif kind == "transposed":
            p = p.T
        o_ref[0] = p.astype(dt)


def _in_projection(x, s1p, sh, w_bf, groups, tm, rope_tables=None):
    B, T, D = x.shape
    use_rope = rope_tables is not None
    in_specs = [pl.BlockSpec((1, tm, D), lambda b, i: (b, i, 0)),
                pl.BlockSpec((1, 1, D), lambda b, i: (b, 0, 0)),
                pl.BlockSpec((1, 1, D), lambda b, i: (b, 0, 0)),
                pl.BlockSpec(w_bf.shape, lambda b, i: (0, 0), pipeline_mode=pl.Buffered(1))]
    args = [x, s1p, sh, w_bf]
    if use_rope:
        in_specs += [pl.BlockSpec((tm, HEAD_W), lambda b, i: (i, 0))] * 2
        args += list(rope_tables)
    out_specs = [pl.BlockSpec((1, BRANCH_W, tm), lambda b, i: (b, 0, i)) if kind == "transposed"
                 else pl.BlockSpec((1, tm, BRANCH_W), lambda b, i: (b, i, 0)) for _, kind, _ in groups]
    out_shape = [jax.ShapeDtypeStruct((B, BRANCH_W, T) if kind == "transposed" else (B, T, BRANCH_W), dt)
                 for _, kind, dt in groups]
    return pl.pallas_call(
        functools.partial(_proj_kernel, groups, use_rope),
        grid=(B, T // tm),
        in_specs=in_specs, out_specs=out_specs, out_shape=out_shape,
        compiler_params=pltpu.CompilerParams(
            dimension_semantics=("parallel", "parallel"), vmem_limit_bytes=V7X_VMEM_LIMIT_BYTES),
        name="in_proj_rope" if use_rope else "in_proj_ctx",
    )(*args)


def _rope_tables(n_tokens):
    n_rows = n_tokens // GRID_W
    inv_freq = jnp.asarray(ROPE_BASE, jnp.float32) ** (-jnp.arange(ROPE_FREQS, dtype=jnp.float32) / ROPE_FREQS)
    sign = np.where((np.arange(MAP_D) % (2 * ROPE_FREQS)) < ROPE_FREQS, -1.0, 1.0).astype(np.float32)

    def table(fn):
        by_row = fn(jnp.arange(n_rows, dtype=jnp.float32)[:, None] * inv_freq)
        by_col = fn(jnp.arange(GRID_W, dtype=jnp.float32)[:, None] * inv_freq)
        by_row = jnp.broadcast_to(by_row[:, None, None, :], (n_rows, GRID_W, 2, ROPE_FREQS))
        by_col = jnp.broadcast_to(by_col[None, :, None, :], (n_rows, GRID_W, 2, ROPE_FREQS))
        return jnp.stack([by_row, by_col], axis=2).reshape(n_tokens, MAP_D)

    cos, sin = table(jnp.cos), table(jnp.sin) * sign
    return jnp.concatenate([cos, cos], axis=1), jnp.concatenate([sin, sin], axis=1)


def _attn_kernel(tk, q_ref, kc_ref, vtc_ref, kl_ref, vtl_ref, lam_ref, gain_ref, o_ref,
                 k_ref, vt_ref, s_ref, p_ref, acc_ref):
    t_ctx = kc_ref.shape[1]

    @pl.when(pl.program_id(2) == 0)
    def _():
        k_ref[:t_ctx, :] = kc_ref[0]
        k_ref[t_ctx:, :] = kl_ref[0]
        vt_ref[:, :t_ctx] = vtc_ref[0]
        vt_ref[:, t_ctx:] = vtl_ref[0]

    qt = q_ref[0].astype(jnp.float32).T
    row = lax.broadcasted_iota(jnp.int32, qt.shape, 0)
    qm = tuple(jnp.where(sel, qt, 0.0).astype(jnp.bfloat16) for sel in (row < MAP_D, row >= MAP_D))
    tq = qt.shape[1]
    n_chunks = k_ref.shape[0] // tk

    def scores(j):
        kb = k_ref[pl.ds(pl.multiple_of(j * tk, tk), tk), :]
        smax = []
        for m in range(2):
            s = jnp.dot(kb, qm[m], preferred_element_type=jnp.float32)
            s_ref[m] = s
            smax.append(jnp.max(s, axis=0, keepdims=True))
        return tuple(smax)

    def weights(smax, ml):
        new_ml, corrs = [], []
        for m in range(2):
            mx, l = ml[m]
            mx_new = jnp.maximum(mx, smax[m])
            corr = jnp.exp2(mx - mx_new)
            p = jnp.exp2(s_ref[m] - mx_new)
            p_ref[m] = p.astype(jnp.bfloat16)
            new_ml.append((mx_new, l * corr + jnp.sum(p, axis=0, keepdims=True)))
            corrs.append(corr)
        return tuple(new_ml), tuple(corrs)

    def values(j, corr):
        vtb = vt_ref[:, pl.ds(pl.multiple_of(j * tk, tk), tk)]
        for m in range(2):
            acc_ref[m] = acc_ref[m] * corr[m] + jnp.dot(vtb, p_ref[m], preferred_element_type=jnp.float32)

    ml = tuple((jnp.full((1, tq), -jnp.inf, jnp.float32), jnp.zeros((1, tq), jnp.float32)) for _ in range(2))
    acc_ref[...] = jnp.zeros_like(acc_ref)
    ml, corr = weights(scores(0), ml)
    smax = scores(1)

    def body(j, carry):
        ml, smax, corr_prev = carry
        values(j - 1, corr_prev)
        ml, corr = weights(smax, ml)
        return ml, scores(j + 1), corr

    ml, smax, corr = lax.fori_loop(1, n_chunks - 1, body, (ml, smax, corr))
    values(n_chunks - 2, corr)
    ml, corr = weights(smax, ml)
    values(n_chunks - 1, corr)

    lp = lam_ref[0]
    lam = (jnp.exp(jnp.sum(lp[0:1] * lp[1:2], axis=-1, keepdims=True))
           - jnp.exp(jnp.sum(lp[2:3] * lp[3:4], axis=-1, keepdims=True)) + LAMBDA_INIT)
    (_, l0), (_, l1) = ml
    ot = acc_ref[0] / l0 - lam * (acc_ref[1] / l1)
    ot = ot * lax.rsqrt(jnp.mean(ot * ot, axis=0, keepdims=True) + EPS)
    o_ref[0] = (ot.T * gain_ref[...] * (1.0 - LAMBDA_INIT)).astype(o_ref.dtype)


def _key_chunk(n_keys):
    return max(t for t in range(LANES, ATTN_MAX_TK + 1, LANES) if n_keys % t == 0)


def _diff_attention(q, k_ctx, vt_ctx, k, vt, diff_lambda, gain):
    B, N, _ = q.shape
    T_ctx = k_ctx.shape[1]
    Tk = T_ctx + N
    tq, tk = ATTN_TQ, _key_chunk(Tk)
    assert N % tq == 0 and Tk // tk >= 3 and T_ctx % LANES == 0
    return pl.pallas_call(
        functools.partial(_attn_kernel, tk),
        grid=(B, HEADS, N // tq),
        in_specs=[pl.BlockSpec((1, tq, HEAD_W), lambda b, h, i: (b, i, h)),
                  pl.BlockSpec((1, T_ctx, HEAD_W), lambda b, h, i: (b, 0, h)),
                  pl.BlockSpec((1, HEAD_W, T_ctx), lambda b, h, i: (b, h, 0)),
                  pl.BlockSpec((1, N, HEAD_W), lambda b, h, i: (b, 0, h)),
                  pl.BlockSpec((1, HEAD_W, N), lambda b, h, i: (b, h, 0)),
                  pl.BlockSpec((1, 4, MAP_D), lambda b, h, i: (0, 0, 0)),
                  pl.BlockSpec((1, HEAD_W), lambda b, h, i: (0, 0))],
        out_specs=pl.BlockSpec((1, tq, HEAD_W), lambda b, h, i: (b, i, h)),
        out_shape=jax.ShapeDtypeStruct((B, N, BRANCH_W), jnp.bfloat16),
        scratch_shapes=[pltpu.VMEM((Tk, HEAD_W), jnp.bfloat16), pltpu.VMEM((HEAD_W, Tk), jnp.bfloat16),
                        pltpu.VMEM((2, tk, tq), jnp.float32), pltpu.VMEM((2, tk, tq), jnp.bfloat16),
                        pltpu.VMEM((2, HEAD_W, tq), jnp.float32)],
        compiler_params=pltpu.CompilerParams(
            dimension_semantics=("parallel", "parallel", "arbitrary"),
            vmem_limit_bytes=V7X_VMEM_LIMIT_BYTES),
        name="diff_attention",
    )(q, k_ctx, vt_ctx, k, vt, diff_lambda, gain)


def _level_maps():
    t = np.arange(HGRN_CHUNK)[:, None]
    s = np.arange(HGRN_CHUNK)[None, :]
    x = t ^ s
    lvl = np.where(x > 0, np.floor(np.log2(np.maximum(x, 1))).astype(np.int32) + 1, 0)
    lvl = np.where(s > t, -1, lvl).astype(np.int32)
    return np.stack([lvl, lvl.T])


def _shift_down(x, s):
    return pltpu.roll(x, s, axis=1)


def _shift_up(x, s):
    return pltpu.roll(x, SUBLANES - s, axis=1)


def _neg_abs(x):
    bits = lax.bitcast_convert_type(x, jnp.uint32) | jnp.uint32(0x80000000)
    return lax.bitcast_convert_type(bits, jnp.float32)


def _hgrn_chain(q, z, v_bf, lb, st, masks, reverse):
    C = HGRN_CHUNK
    groups = C // SUBLANES
    sig = jax.nn.sigmoid(z)
    g = jnp.log2(lb + (1.0 - lb) * sig)
    kk = (1.0 - lb) * (1.0 - sig)
    grouped = (groups, SUBLANES, HEAD_W)
    sub = lax.broadcasted_iota(jnp.int32, grouped, 1)

    bg = g.reshape(grouped)
    for s in (1, 2, 4):
        if reverse:
            bg = bg + jnp.where(sub < SUBLANES - s, _shift_up(bg, s), 0.0)
        else:
            bg = bg + jnp.where(sub >= s, _shift_down(bg, s), 0.0)
    parts = [bg[r] for r in range(groups)]
    order = range(groups - 2, -1, -1) if reverse else range(1, groups)
    for r in order:
        prev = parts[r + 1][0:1, :] if reverse else parts[r - 1][SUBLANES - 1:SUBLANES, :]
        parts[r] = parts[r] + prev
    b = jnp.concatenate(parts, axis=0)
    bg = b.reshape(grouped)

    q_bf = q.astype(jnp.bfloat16)
    kk_bf = kk.astype(jnp.bfloat16)

    def rows(x, r):
        return x[r * SUBLANES:(r + 1) * SUBLANES, :]

    a_parts = [None] * groups

    own = bg
    nbr = _shift_up(bg, 1) if reverse else _shift_down(bg, 1)
    for j in range(1, HGRN_LEVELS + 1):
        h = 1 << (j - 1)
        if 2 * h <= SUBLANES:
            upper = (sub & h) != 0
            ref = jnp.where(upper, own, nbr) if reverse else jnp.where(upper, nbr, own)
            w = jnp.exp2(_neg_abs(bg - ref)).reshape(C, HEAD_W).astype(jnp.bfloat16)
            if 4 * h <= SUBLANES:
                if reverse:
                    own = jnp.where(upper, _shift_down(own, h), own)
                    nbr = jnp.where(upper, nbr, _shift_up(nbr, h))
                else:
                    own = jnp.where(upper, own, _shift_up(own, h))
                    nbr = jnp.where(upper, _shift_down(nbr, h), nbr)
            pm = lax.dot_general(q_bf * w, kk_bf * w, _NT, preferred_element_type=jnp.float32)
            a_parts = [jnp.where(rows(masks[j], r), rows(pm, r), 0.0 if a_parts[r] is None else a_parts[r])
                       for r in range(groups)]
        else:
            n_blk, half_groups = C // (2 * h), h // SUBLANES
            blocked = (n_blk, 2 * h, HEAD_W)
            b_blk = b.reshape(blocked)
            ref = b_blk[:, h:h + 1, :] if reverse else b_blk[:, h - 1:h, :]
            w_blk = jnp.exp2(_neg_abs(b_blk - ref))
            q_half, k_half = (slice(0, h), slice(h, 2 * h)) if reverse else (slice(h, 2 * h), slice(0, h))
            q_t = (q.reshape(blocked)[:, q_half] * w_blk[:, q_half]).reshape(C // 2, HEAD_W)
            k_t = kk.reshape(blocked)[:, k_half] * w_blk[:, k_half]
            zeros = jnp.zeros_like(k_t)
            k_t = jnp.concatenate([zeros, k_t] if reverse else [k_t, zeros], axis=1).reshape(C, HEAD_W)
            pm = lax.dot_general(q_t.astype(jnp.bfloat16), k_t.astype(jnp.bfloat16), _NT,
                                 preferred_element_type=jnp.float32)
            for blk in range(n_blk):
                for i in range(half_groups):
                    r = blk * 2 * half_groups + (0 if reverse else half_groups) + i
                    piece = rows(pm, blk * half_groups + i)
                    if n_blk == 1:
                        a_parts[r] = a_parts[r] + piece
                    else:
                        a_parts[r] = jnp.where(rows(masks[j], r), piece, a_parts[r])
    a = jnp.concatenate(a_parts, axis=0)

    o = jnp.dot(a.astype(jnp.bfloat16), v_bf, preferred_element_type=jnp.float32)
    o = o + lax.dot_general(q_bf * jnp.exp2(b).astype(jnp.bfloat16), st.astype(jnp.bfloat16), _NT,
                            preferred_element_type=jnp.float32)
    o = o + jnp.sum(q * kk, axis=-1, keepdims=True) * v_bf.astype(jnp.float32)
    b_tot = b[0:1, :] if reverse else b[C - 1:C, :]
    k_hat = kk_bf * jnp.exp2(b_tot - b).astype(jnp.bfloat16)
    st_new = st * jnp.exp2(b_tot) + lax.dot_general(v_bf, k_hat, _TN, preferred_element_type=jnp.float32)
    return o, st_new


def _hgrn_kernel(n_sub, qf_ref, vf_ref, zf_ref, qb_ref, vb_ref, zb_ref, lbp_ref, lvl_ref, s0_ref,
                 of_ref, ob_ref, sfin_ref, st_ref):
    i = pl.program_id(1)
    C = HGRN_CHUNK

    @pl.when(i == 0)
    def _():
        st_ref[...] = s0_ref[0]

    dirs = ((qf_ref, vf_ref, zf_ref, of_ref), (qb_ref, vb_ref, zb_ref, ob_ref))
    masks, lbs = [], []
    for d in range(2):
        lvl = lvl_ref[d]
        masks.append([lvl == j for j in range(HGRN_LEVELS + 1)])
        p0, p1 = lbp_ref[d, 0:1, :], lbp_ref[d, 1:2, :]
        pm = jnp.maximum(p0, p1)
        e0 = jnp.exp(p0 - pm)
        lbs.append(e0 / (e0 + jnp.exp(p1 - pm)))
    states = [[st_ref[d, hh] for hh in range(HEADS)] for d in range(2)]
    for sub in range(n_sub):
        for d, (q_ref, v_ref, z_ref, o_ref) in enumerate(dirs):
            c = n_sub - 1 - sub if d == 1 else sub
            tok = slice(c * C, (c + 1) * C)
            for hh in range(HEADS):
                cols = slice(hh * HEAD_W, (hh + 1) * HEAD_W)
                o, states[d][hh] = _hgrn_chain(q_ref[0, tok, cols], z_ref[0, tok, cols], v_ref[0, tok, cols],
                                               lbs[d][:, cols], states[d][hh], masks[d], reverse=(d == 1))
                o_ref[0, tok, cols] = o.astype(o_ref.dtype)
    for d in range(2):
        for hh in range(HEADS):
            st_ref[d, hh] = states[d][hh]

    @pl.when(i == pl.num_programs(1) - 1)
    def _():
        sfin_ref[0] = st_ref[...]


def _hgrn2_bidir(hq, hi, hff, hfb, lb_param, s0, n_sub):
    B, T, _ = hq.shape
    rows = HGRN_CHUNK * n_sub
    n = T // rows
    assert n * rows == T
    fwd = pl.BlockSpec((1, rows, BRANCH_W), lambda b, i: (b, i, 0))
    bwd = pl.BlockSpec((1, rows, BRANCH_W), lambda b, i: (b, n - 1 - i, 0))
    st_spec = pl.BlockSpec((1, 2, HEADS, HEAD_W, HEAD_W), lambda b, i: (b, 0, 0, 0, 0))
    lvl = jnp.asarray(_level_maps())
    return pl.pallas_call(
        functools.partial(_hgrn_kernel, n_sub),
        grid=(B, n),
        in_specs=[fwd, fwd, fwd, bwd, bwd, bwd,
                  pl.BlockSpec(lb_param.shape, lambda b, i: (0, 0, 0)),
                  pl.BlockSpec(lvl.shape, lambda b, i: (0, 0, 0)),
                  st_spec],
        out_specs=[fwd, bwd, st_spec],
        out_shape=[jax.ShapeDtypeStruct((B, T, BRANCH_W), jnp.bfloat16),
                   jax.ShapeDtypeStruct((B, T, BRANCH_W), jnp.bfloat16),
                   jax.ShapeDtypeStruct((B, 2, HEADS, HEAD_W, HEAD_W), jnp.float32)],
        scratch_shapes=[pltpu.VMEM((2, HEADS, HEAD_W, HEAD_W), jnp.float32)],
        compiler_params=pltpu.CompilerParams(
            dimension_semantics=("parallel", "arbitrary"), vmem_limit_bytes=V7X_VMEM_LIMIT_BYTES),
        name="hgrn2_bidir",
    )(hq, hi, hff, hq, hi, hfb, lb_param, lvl, s0)


def _merge_kernel(att_ref, ag_ref, of_ref, ob_ref, hg_ref, x_ref, gate_ref, w_ref, hgain_ref, lng_ref, lnb_ref,
                  o_ref):
    f32 = jnp.float32
    att = att_ref[0].astype(f32) * _silu(ag_ref[0].astype(f32))
    o = of_ref[0].astype(f32) + ob_ref[0].astype(f32)
    parts = []
    for hh in range(HEADS):
        oh = o[:, hh * HEAD_W:(hh + 1) * HEAD_W]
        parts.append(oh * lax.rsqrt(jnp.mean(oh * oh, axis=-1, keepdims=True) + EPS) * hgain_ref[...])
    hg = jnp.concatenate(parts, axis=1) * _silu(hg_ref[0].astype(f32))
    y_in = jnp.concatenate([att, hg], axis=1).astype(jnp.bfloat16)
    y = jnp.dot(y_in, w_ref[...], preferred_element_type=jnp.float32)
    u = ALPHA * x_ref[0] + gate_ref[0] * y
    mu = jnp.mean(u, axis=-1, keepdims=True)
    uc = u - mu
    var = jnp.mean(uc * uc, axis=-1, keepdims=True)
    o_ref[0] = uc * lax.rsqrt(var + EPS) * lng_ref[...] + lnb_ref[...]


def _merge(att, ag, o_f, o_b, hg, x, gate, w_out_bf, hgain, ln_g, ln_b, tm):
    B, N, D = x.shape
    half = pl.BlockSpec((1, tm, BRANCH_W), lambda b, i: (b, i, 0))
    full = pl.BlockSpec((1, tm, D), lambda b, i: (b, i, 0))
    row = lambda w: pl.BlockSpec((1, w), lambda b, i: (0, 0))
    return pl.pallas_call(
        _merge_kernel,
        grid=(B, N // tm),
        in_specs=[half, half, half, half, half, full,
                  pl.BlockSpec((1, 1, D), lambda b, i: (b, 0, 0)),
                  pl.BlockSpec(w_out_bf.shape, lambda b, i: (0, 0)),
                  row(HEAD_W), row(D), row(D)],
        out_specs=full,
        out_shape=jax.ShapeDtypeStruct((B, N, D), jnp.float32),
        compiler_params=pltpu.CompilerParams(
            dimension_semantics=("parallel", "parallel"), vmem_limit_bytes=V7X_VMEM_LIMIT_BYTES),
        name="merge_out_proj_ln",
    )(att, ag, o_f, o_b, hg, x, gate, w_out_bf, hgain, ln_g, ln_b)


def kernel(x, c, ctx, c_ctx, w_ada, b_ada, w_in, w_out, diff_lambda, diff_subln_gain, hgrn_lower_bound,
           hgrn_norm_gain, ln_gain, ln_bias):
    B, N, D = x.shape
    assert DEPTH == 1 and w_ada.shape[0] == 1
    cvec = jnp.concatenate([c, c_ctx[None, :], jnp.zeros((SUBLANES - B - 1, D), c.dtype)], axis=0)
    mod = _modulation(cvec, w_ada[0], b_ada[0][None, :])
    shift, scale, gate = mod[:, :D], mod[:, D:2 * D], mod[:, 2 * D:]
    s1p = (1.0 + scale)[:, None, :]
    shift = shift[:, None, :]
    ctx_rows = jnp.full((B,), B, jnp.int32)

    w_bf = w_in[0].astype(jnp.bfloat16)
    k_c, v_c, hq_c, hi_c, hff_c, hfb_c = _in_projection(
        ctx, s1p[ctx_rows], shift[ctx_rows], w_bf, _CTX_GROUPS, tm=ctx.shape[1])
    q, k, v, ag, hq, hi, hff, hfb, hg = _in_projection(
        x, s1p[:B], shift[:B], w_bf, _LATENT_GROUPS, tm=ROW_TILE, rope_tables=_rope_tables(N))

    att = _diff_attention(q, k_c, v_c, k, v, diff_lambda, diff_subln_gain)

    s_zero = jnp.zeros((B, 2, HEADS, HEAD_W, HEAD_W), jnp.float32)
    _, _, s_ctx = _hgrn2_bidir(hq_c, hi_c, hff_c, hfb_c, hgrn_lower_bound, s_zero, n_sub=ctx.shape[1] // HGRN_CHUNK)
    o_f, o_b, _ = _hgrn2_bidir(hq, hi, hff, hfb, hgrn_lower_bound, s_ctx, n_sub=HGRN_CHUNKS_PER_STEP)

    return _merge(att, ag, o_f, o_b, hg, x, gate[:B, None, :], w_out[0].astype(jnp.bfloat16),
                  hgrn_norm_gain, ln_gain, ln_bias, tm=ROW_TILE)
```

```python
import functools
import math

import numpy as np
import jax
import jax.numpy as jnp
from jax import lax
from jax.experimental import pallas as pl
from jax.experimental.pallas import tpu as pltpu

D_MODEL = 1024
DEPTH = 1
GRID_W = 64
HEADS = 4
HEAD_W = 128
MAP_D = 64
BRANCH_W = HEADS * HEAD_W
ROPE_BASE = 10000.0
ROPE_FREQS = MAP_D // 4
EPS = 1e-5
LAMBDA_INIT = 0.8 - 0.6 * math.exp(-0.3 * 0)
ALPHA = (2.0 * DEPTH) ** 0.25

V7X_VMEM_LIMIT_BYTES = 56 * 1024 * 1024
SUBLANES = 8
LANES = 128
ROW_TILE = 512
ATTN_TQ = 512
ATTN_MAX_TK = 1408
HGRN_CHUNKS_PER_STEP = 4
HGRN_CHUNK = 128
HGRN_LEVELS = 7

_NT = (((1,), (1,)), ((), ()))
_TN = (((0,), (0,)), ((), ()))


def _silu(x):
    return x * jax.nn.sigmoid(x)


def _mod_kernel(c_ref, w_ref, b_ref, o_ref):
    a = _silu(c_ref[...]).astype(jnp.bfloat16)
    o_ref[...] = jnp.dot(a, w_ref[0].astype(jnp.bfloat16), preferred_element_type=jnp.float32) + b_ref[...]


def _modulation(cvec, w_ada, b_ada):
    rows, d = cvec.shape
    n_out = w_ada.shape[2]
    bn = 1024
    return pl.pallas_call(
        _mod_kernel,
        grid=(n_out // bn,),
        in_specs=[pl.BlockSpec((rows, d), lambda j: (0, 0)),
                  pl.BlockSpec((1, d, bn), lambda j: (0, 0, j)),
                  pl.BlockSpec((1, bn), lambda j: (0, j))],
        out_specs=pl.BlockSpec((rows, bn), lambda j: (0, j)),
        out_shape=jax.ShapeDtypeStruct((rows, n_out), jnp.float32),
        compiler_params=pltpu.CompilerParams(vmem_limit_bytes=V7X_VMEM_LIMIT_BYTES),
        name="modulation",
    )(cvec, w_ada, b_ada)


_LATENT_GROUPS = ((0, "rope_q", jnp.bfloat16), (1, "rope", jnp.bfloat16), (2, "transposed", jnp.bfloat16),
                  (3, "plain", jnp.bfloat16), (4, "silu", jnp.float32), (5, "plain", jnp.bfloat16),
                  (6, "plain", jnp.float32), (7, "plain", jnp.float32), (8, "plain", jnp.bfloat16))
_CTX_GROUPS = ((1, "plain", jnp.bfloat16), (2, "transposed", jnp.bfloat16), (4, "silu", jnp.float32),
               (5, "plain", jnp.bfloat16), (6, "plain", jnp.float32), (7, "plain", jnp.float32))
Q_SCALE = math.log2(math.e) / math.sqrt(MAP_D)


def _proj_kernel(groups, use_rope, *refs):
    if use_rope:
        x_ref, s1p_ref, sh_ref, w_ref, cos_ref, sin_ref = refs[:6]
        out_refs = refs[6:]
    else:
        x_ref, s1p_ref, sh_ref, w_ref = refs[:4]
        out_refs = refs[4:]
    h = (x_ref[0] * s1p_ref[0] + sh_ref[0]).astype(jnp.bfloat16)
    if use_rope:
        cos = jnp.concatenate([cos_ref[...]] * HEADS, axis=1)
        sin = jnp.concatenate([sin_ref[...]] * HEADS, axis=1)
        lane = lax.broadcasted_iota(jnp.int32, cos.shape, 1)
        first_half = (lane & (2 * ROPE_FREQS - 1)) < ROPE_FREQS
    for (g, kind, dt), o_ref in zip(groups, out_refs):
        p = jnp.dot(h, w_ref[:, g * BRANCH_W:(g + 1) * BRANCH_W], preferred_element_type=jnp.float32)
        if kind in ("rope", "rope_q"):
            partner = jnp.where(first_half,
                                pltpu.roll(p, BRANCH_W - ROPE_FREQS, axis=1),
                                pltpu.roll(p, ROPE_FREQS, axis=1))
            p = p * cos + partner * sin
            if kind == "rope_q":
                p = p * Q_SCALE
        elif kind == "silu":
            p = _silu(p)
        elif kind == "transposed":
            p = p.T
        o_ref[0] = p.astype(dt)


def _in_projection(x, s1p, sh, w_bf, groups, tm, rope_tables=None):
    B, T, D = x.shape
    use_rope = rope_tables is not None
    in_specs = [pl.BlockSpec((1, tm, D), lambda b, i: (b, i, 0)),
                pl.BlockSpec((1, 1, D), lambda b, i: (b, 0, 0)),
                pl.BlockSpec((1, 1, D), lambda b, i: (b, 0, 0)),
                pl.BlockSpec(w_bf.shape, lambda b, i: (0, 0), pipeline_mode=pl.Buffered(1))]
    args = [x, s1p, sh, w_bf]
    if use_rope:
        in_specs += [pl.BlockSpec((tm, HEAD_W), lambda b, i: (i, 0))] * 2
        args += list(rope_tables)
    out_specs = [pl.BlockSpec((1, BRANCH_W, tm), lambda b, i: (b, 0, i)) if kind == "transposed"
                 else pl.BlockSpec((1, tm, BRANCH_W), lambda b, i: (b, i, 0)) for _, kind, _ in groups]
    out_shape = [jax.ShapeDtypeStruct((B, BRANCH_W, T) if kind == "transposed" else (B, T, BRANCH_W), dt)
                 for _, kind, dt in groups]
    return pl.pallas_call(
        functools.partial(_proj_kernel, groups, use_rope),
        grid=(B, T // tm),
        in_specs=in_specs, out_specs=out_specs, out_shape=out_shape,
        compiler_params=pltpu.CompilerParams(
            dimension_semantics=("parallel", "parallel"), vmem_limit_bytes=V7X_VMEM_LIMIT_BYTES),
        name="in_proj_rope" if use_rope else "in_proj_ctx",
    )(*args)


def _rope_tables(n_tokens):
    n_rows = n_tokens // GRID_W
    lane = np.arange(HEAD_W) % MAP_D
    inv_freq = jnp.asarray(ROPE_BASE, jnp.float32) ** (-jnp.arange(ROPE_FREQS, dtype=jnp.float32) / ROPE_FREQS)
    freq = inv_freq[lane % ROPE_FREQS]
    row_lane = (lane // (2 * ROPE_FREQS)) == 0
    sign = np.where((lane % (2 * ROPE_FREQS)) < ROPE_FREQS, -1.0, 1.0).astype(np.float32)

    def table(fn):
        by_row = fn(jnp.arange(n_rows, dtype=jnp.float32)[:, None] * freq)
        by_col = fn(jnp.arange(GRID_W, dtype=jnp.float32)[:, None] * freq)
        full = jnp.where(row_lane, by_row[:, None, :], by_col[None, :, :])
        return full.reshape(n_tokens, HEAD_W)

    return table(jnp.cos), table(jnp.sin) * sign


def _attn_kernel(tk, q_ref, kc_ref, vtc_ref, kl_ref, vtl_ref, lam_ref, gain_ref, o_ref,
                 k_ref, vt_ref, s_ref, p_ref, acc_ref):
    t_ctx = kc_ref.shape[1]

    @pl.when(pl.program_id(2) == 0)
    def _():
        k_ref[:t_ctx, :] = kc_ref[0]
        k_ref[t_ctx:, :] = kl_ref[0]
        vt_ref[:, :t_ctx] = vtc_ref[0]
        vt_ref[:, t_ctx:] = vtl_ref[0]

    qt = q_ref[0].astype(jnp.float32).T
    row = lax.broadcasted_iota(jnp.int32, qt.shape, 0)
    qm = tuple(jnp.where(sel, qt, 0.0).astype(jnp.bfloat16) for sel in (row < MAP_D, row >= MAP_D))
    tq = qt.shape[1]
    n_chunks = k_ref.shape[0] // tk

    def scores(j):
        kb = k_ref[pl.ds(pl.multiple_of(j * tk, tk), tk), :]
        smax = []
        for m in range(2):
            s = jnp.dot(kb, qm[m], preferred_element_type=jnp.float32)
            s_ref[m] = s
            smax.append(jnp.max(s, axis=0, keepdims=True))
        return tuple(smax)

    def weights(smax, ml):
        new_ml, corrs = [], []
        for m in range(2):
            mx, l = ml[m]
            mx_new = jnp.maximum(mx, smax[m])
            corr = jnp.exp2(mx - mx_new)
            p = jnp.exp2(s_ref[m] - mx_new)
            p_ref[m] = p.astype(jnp.bfloat16)
            new_ml.append((mx_new, l * corr + jnp.sum(p, axis=0, keepdims=True)))
            corrs.append(corr)
        return tuple(new_ml), tuple(corrs)

    def values(j, corr):
        vtb = vt_ref[:, pl.ds(pl.multiple_of(j * tk, tk), tk)]
        for m in range(2):
            acc_ref[m] = acc_ref[m] * corr[m] + jnp.dot(vtb, p_ref[m], preferred_element_type=jnp.float32)

    ml = tuple((jnp.full((1, tq), -jnp.inf, jnp.float32), jnp.zeros((1, tq), jnp.float32)) for _ in range(2))
    acc_ref[...] = jnp.zeros_like(acc_ref)
    ml, corr = weights(scores(0), ml)
    smax = scores(1)

    def body(j, carry):
        ml, smax, corr_prev = carry
        values(j - 1, corr_prev)
        ml, corr = weights(smax, ml)
        return ml, scores(j + 1), corr

    ml, smax, corr = lax.fori_loop(1, n_chunks - 1, body, (ml, smax, corr))
    values(n_chunks - 2, corr)
    ml, corr = weights(smax, ml)
    values(n_chunks - 1, corr)

    lp = lam_ref[0]
    lam = (jnp.exp(jnp.sum(lp[0:1] * lp[1:2], axis=-1, keepdims=True))
           - jnp.exp(jnp.sum(lp[2:3] * lp[3:4], axis=-1, keepdims=True)) + LAMBDA_INIT)
    (_, l0), (_, l1) = ml
    ot = acc_ref[0] / l0 - lam * (acc_ref[1] / l1)
    ot = ot * lax.rsqrt(jnp.mean(ot * ot, axis=0, keepdims=True) + EPS)
    o_ref[0] = (ot.T * gain_ref[...] * (1.0 - LAMBDA_INIT)).astype(o_ref.dtype)


def _key_chunk(n_keys):
    return max(t for t in range(LANES, ATTN_MAX_TK + 1, LANES) if n_keys % t == 0)


def _diff_attention(q, k_ctx, vt_ctx, k, vt, diff_lambda, gain):
    B, N, _ = q.shape
    T_ctx = k_ctx.shape[1]
    Tk = T_ctx + N
    tq, tk = ATTN_TQ, _key_chunk(Tk)
    assert N % tq == 0 and Tk // tk >= 3 and T_ctx % LANES == 0
    return pl.pallas_call(
        functools.partial(_attn_kernel, tk),
        grid=(B, HEADS, N // tq),
        in_specs=[pl.BlockSpec((1, tq, HEAD_W), lambda b, h, i: (b, i, h)),
                  pl.BlockSpec((1, T_ctx, HEAD_W), lambda b, h, i: (b, 0, h)),
                  pl.BlockSpec((1, HEAD_W, T_ctx), lambda b, h, i: (b, h, 0)),
                  pl.BlockSpec((1, N, HEAD_W), lambda b, h, i: (b, 0, h)),
                  pl.BlockSpec((1, HEAD_W, N), lambda b, h, i: (b, h, 0)),
                  pl.BlockSpec((1, 4, MAP_D), lambda b, h, i: (0, 0, 0)),
                  pl.BlockSpec((1, HEAD_W), lambda b, h, i: (0, 0))],
        out_specs=pl.BlockSpec((1, tq, HEAD_W), lambda b, h, i: (b, i, h)),
        out_shape=jax.ShapeDtypeStruct((B, N, BRANCH_W), jnp.bfloat16),
        scratch_shapes=[pltpu.VMEM((Tk, HEAD_W), jnp.bfloat16), pltpu.VMEM((HEAD_W, Tk), jnp.bfloat16),
                        pltpu.VMEM((2, tk, tq), jnp.float32), pltpu.VMEM((2, tk, tq), jnp.bfloat16),
                        pltpu.VMEM((2, HEAD_W, tq), jnp.float32)],
        compiler_params=pltpu.CompilerParams(
            dimension_semantics=("parallel", "parallel", "arbitrary"),
            vmem_limit_bytes=V7X_VMEM_LIMIT_BYTES),
        name="diff_attention",
    )(q, k_ctx, vt_ctx, k, vt, diff_lambda, gain)


def _level_maps():
    t = np.arange(HGRN_CHUNK)[:, None]
    s = np.arange(HGRN_CHUNK)[None, :]
    x = t ^ s
    lvl = np.where(x > 0, np.floor(np.log2(np.maximum(x, 1))).astype(np.int32) + 1, 0)
    lvl = np.where(s > t, -1, lvl).astype(np.int32)
    return np.stack([lvl, lvl.T])


def _shift_down(x, s):
    return pltpu.roll(x, s, axis=1)


def _shift_up(x, s):
    return pltpu.roll(x, SUBLANES - s, axis=1)


def _neg_abs(x):
    bits = lax.bitcast_convert_type(x, jnp.uint32) | jnp.uint32(0x80000000)
    return lax.bitcast_convert_type(bits, jnp.float32)


def _hgrn_chain(q, z, v_bf, lb, st, masks, reverse):
    C = HGRN_CHUNK
    groups = C // SUBLANES
    sig = jax.nn.sigmoid(z)
    g = jnp.log2(lb + (1.0 - lb) * sig)
    kk = (1.0 - lb) * (1.0 - sig)
    grouped = (groups, SUBLANES, HEAD_W)
    sub = lax.broadcasted_iota(jnp.int32, grouped, 1)

    bg = g.reshape(grouped)
    for s in (1, 2, 4):
        if reverse:
            bg = bg + jnp.where(sub < SUBLANES - s, _shift_up(bg, s), 0.0)
        else:
            bg = bg + jnp.where(sub >= s, _shift_down(bg, s), 0.0)
    parts = [bg[r] for r in range(groups)]
    order = range(groups - 2, -1, -1) if reverse else range(1, groups)
    for r in order:
        prev = parts[r + 1][0:1, :] if reverse else parts[r - 1][SUBLANES - 1:SUBLANES, :]
        parts[r] = parts[r] + prev
    b = jnp.concatenate(parts, axis=0)
    bg = b.reshape(grouped)

    q_bf = q.astype(jnp.bfloat16)
    kk_bf = kk.astype(jnp.bfloat16)

    def rows(x, r):
        return x[r * SUBLANES:(r + 1) * SUBLANES, :]

    a_parts = [None] * groups

    own = bg
    nbr = _shift_up(bg, 1) if reverse else _shift_down(bg, 1)
    for j in range(1, HGRN_LEVELS + 1):
        h = 1 << (j - 1)
        if 2 * h <= SUBLANES:
            upper = (sub & h) != 0
            ref = jnp.where(upper, own, nbr) if reverse else jnp.where(upper, nbr, own)
            w = jnp.exp2(_neg_abs(bg - ref)).reshape(C, HEAD_W).astype(jnp.bfloat16)
            if 4 * h <= SUBLANES:
                if reverse:
                    own = jnp.where(upper, _shift_down(own, h), own)
                    nbr = jnp.where(upper, nbr, _shift_up(nbr, h))
                else:
                    own = jnp.where(upper, own, _shift_up(own, h))
                    nbr = jnp.where(upper, _shift_down(nbr, h), nbr)
            pm = lax.dot_general(q_bf * w, kk_bf * w, _NT, preferred_element_type=jnp.float32)
            a_parts = [jnp.where(rows(masks[j], r), rows(pm, r), 0.0 if a_parts[r] is None else a_parts[r])
                       for r in range(groups)]
        else:
            n_blk, half_groups = C // (2 * h), h // SUBLANES
            blocked = (n_blk, 2 * h, HEAD_W)
            b_blk = b.reshape(blocked)
            ref = b_blk[:, h:h + 1, :] if reverse else b_blk[:, h - 1:h, :]
            w_blk = jnp.exp2(_neg_abs(b_blk - ref))
            q_half, k_half = (slice(0, h), slice(h, 2 * h)) if reverse else (slice(h, 2 * h), slice(0, h))
            q_t = (q.reshape(blocked)[:, q_half] * w_blk[:, q_half]).reshape(C // 2, HEAD_W)
            k_t = kk.reshape(blocked)[:, k_half] * w_blk[:, k_half]
            zeros = jnp.zeros_like(k_t)
            k_t = jnp.concatenate([zeros, k_t] if reverse else [k_t, zeros], axis=1).reshape(C, HEAD_W)
            pm = lax.dot_general(q_t.astype(jnp.bfloat16), k_t.astype(jnp.bfloat16), _NT,
                                 preferred_element_type=jnp.float32)
            for blk in range(n_blk):
                for i in range(half_groups):
                    r = blk * 2 * half_groups + (0 if reverse else half_groups) + i
                    piece = rows(pm, blk * half_groups + i)
                    if n_blk == 1:
                        a_parts[r] = a_parts[r] + piece
                    else:
                        a_parts[r] = jnp.where(rows(masks[j], r), piece, a_parts[r])
    a = jnp.concatenate(a_parts, axis=0)

    o = jnp.dot(a.astype(jnp.bfloat16), v_bf, preferred_element_type=jnp.float32)
    o = o + lax.dot_general(q_bf * jnp.exp2(b).astype(jnp.bfloat16), st.astype(jnp.bfloat16), _NT,
                            preferred_element_type=jnp.float32)
    o = o + jnp.sum(q * kk, axis=-1, keepdims=True) * v_bf.astype(jnp.float32)
    b_tot = b[0:1, :] if reverse else b[C - 1:C, :]
    k_hat = kk_bf * jnp.exp2(b_tot - b).astype(jnp.bfloat16)
    st_new = st * jnp.exp2(b_tot) + lax.dot_general(v_bf, k_hat, _TN, preferred_element_type=jnp.float32)
    return o, st_new


def _hgrn_kernel(n_sub, qf_ref, vf_ref, zf_ref, qb_ref, vb_ref, zb_ref, lbp_ref, lvl_ref, s0_ref,
                 of_ref, ob_ref, sfin_ref, st_ref):
    i = pl.program_id(1)
    C = HGRN_CHUNK

    @pl.when(i == 0)
    def _():
        st_ref[...] = s0_ref[0]

    dirs = ((qf_ref, vf_ref, zf_ref, of_ref), (qb_ref, vb_ref, zb_ref, ob_ref))
    masks, lbs = [], []
    for d in range(2):
        lvl = lvl_ref[d]
        masks.append([lvl == j for j in range(HGRN_LEVELS + 1)])
        p0, p1 = lbp_ref[d, 0:1, :], lbp_ref[d, 1:2, :]
        pm = jnp.maximum(p0, p1)
        e0 = jnp.exp(p0 - pm)
        lbs.append(e0 / (e0 + jnp.exp(p1 - pm)))
    states = [[st_ref[d, hh] for hh in range(HEADS)] for d in range(2)]
    for sub in range(n_sub):
        for d, (q_ref, v_ref, z_ref, o_ref) in enumerate(dirs):
            c = n_sub - 1 - sub if d == 1 else sub
            tok = slice(c * C, (c + 1) * C)
            for hh in range(HEADS):
                cols = slice(hh * HEAD_W, (hh + 1) * HEAD_W)
                o, states[d][hh] = _hgrn_chain(q_ref[0, tok, cols], z_ref[0, tok, cols], v_ref[0, tok, cols],
                                               lbs[d][:, cols], states[d][hh], masks[d], reverse=(d == 1))
                o_ref[0, tok, cols] = o.astype(o_ref.dtype)
    for d in range(2):
        for hh in range(HEADS):
            st_ref[d, hh] = states[d][hh]

    @pl.when(i == pl.num_programs(1) - 1)
    def _():
        sfin_ref[0] = st_ref[...]


def _hgrn2_bidir(hq, hi, hff, hfb, lb_param, s0, n_sub):
    B, T, _ = hq.shape
    rows = HGRN_CHUNK * n_sub
    n = T // rows
    assert n * rows == T
    fwd = pl.BlockSpec((1, rows, BRANCH_W), lambda b, i: (b, i, 0))
    bwd = pl.BlockSpec((1, rows, BRANCH_W), lambda b, i: (b, n - 1 - i, 0))
    st_spec = pl.BlockSpec((1, 2, HEADS, HEAD_W, HEAD_W), lambda b, i: (b, 0, 0, 0, 0))
    lvl = jnp.asarray(_level_maps())
    return pl.pallas_call(
        functools.partial(_hgrn_kernel, n_sub),
        grid=(B, n),
        in_specs=[fwd, fwd, fwd, bwd, bwd, bwd,
                  pl.BlockSpec(lb_param.shape, lambda b, i: (0, 0, 0)),
                  pl.BlockSpec(lvl.shape, lambda b, i: (0, 0, 0)),
                  st_spec],
        out_specs=[fwd, bwd, st_spec],
        out_shape=[jax.ShapeDtypeStruct((B, T, BRANCH_W), jnp.bfloat16),
                   jax.ShapeDtypeStruct((B, T, BRANCH_W), jnp.bfloat16),
                   jax.ShapeDtypeStruct((B, 2, HEADS, HEAD_W, HEAD_W), jnp.float32)],
        scratch_shapes=[pltpu.VMEM((2, HEADS, HEAD_W, HEAD_W), jnp.float32)],
        compiler_params=pltpu.CompilerParams(
            dimension_semantics=("parallel", "arbitrary"), vmem_limit_bytes=V7X_VMEM_LIMIT_BYTES),
        name="hgrn2_bidir",
    )(hq, hi, hff, hq, hi, hfb, lb_param, lvl, s0)


def _merge_kernel(att_ref, ag_ref, of_ref, ob_ref, hg_ref, x_ref, gate_ref, w_ref, hgain_ref, lng_ref, lnb_ref,
                  o_ref):
    f32 = jnp.float32
    att = att_ref[0].astype(f32) * _silu(ag_ref[0].astype(f32))
    o = of_ref[0].astype(f32) + ob_ref[0].astype(f32)
    parts = []
    for hh in range(HEADS):
        oh = o[:, hh * HEAD_W:(hh + 1) * HEAD_W]
        parts.append(oh * lax.rsqrt(jnp.mean(oh * oh, axis=-1, keepdims=True) + EPS) * hgain_ref[...])
    hg = jnp.concatenate(parts, axis=1) * _silu(hg_ref[0].astype(f32))
    y_in = jnp.concatenate([att, hg], axis=1).astype(jnp.bfloat16)
    y = jnp.dot(y_in, w_ref[...], preferred_element_type=jnp.float32)
    u = ALPHA * x_ref[0] + gate_ref[0] * y
    mu = jnp.mean(u, axis=-1, keepdims=True)
    uc = u - mu
    var = jnp.mean(uc * uc, axis=-1, keepdims=True)
    o_ref[0] = uc * lax.rsqrt(var + EPS) * lng_ref[...] + lnb_ref[...]


def _merge(att, ag, o_f, o_b, hg, x, gate, w_out_bf, hgain, ln_g, ln_b, tm):
    B, N, D = x.shape
    half = pl.BlockSpec((1, tm, BRANCH_W), lambda b, i: (b, i, 0))
    full = pl.BlockSpec((1, tm, D), lambda b, i: (b, i, 0))
    row = lambda w: pl.BlockSpec((1, w), lambda b, i: (0, 0))
    return pl.pallas_call(
        _merge_kernel,
        grid=(B, N // tm),
        in_specs=[half, half, half, half, half, full,
                  pl.BlockSpec((1, 1, D), lambda b, i: (b, 0, 0)),
                  pl.BlockSpec(w_out_bf.shape, lambda b, i: (0, 0)),
                  row(HEAD_W), row(D), row(D)],
        out_specs=full,
        out_shape=jax.ShapeDtypeStruct((B, N, D), jnp.float32),
        compiler_params=pltpu.CompilerParams(
            dimension_semantics=("parallel", "parallel"), vmem_limit_bytes=V7X_VMEM_LIMIT_BYTES),
        name="merge_out_proj_ln",
    )(att, ag, o_f, o_b, hg, x, gate, w_out_bf, hgain, ln_g, ln_b)


def kernel(x, c, ctx, c_ctx, w_ada, b_ada, w_in, w_out, diff_lambda, diff_subln_gain, hgrn_lower_bound,
           hgrn_norm_gain, ln_gain, ln_bias):
    B, N, D = x.shape
    assert DEPTH == 1 and w_ada.shape[0] == 1
    cvec = jnp.concatenate([c, c_ctx[None, :], jnp.zeros((SUBLANES - B - 1, D), c.dtype)], axis=0)
    mod = _modulation(cvec, w_ada, b_ada)
    shift, scale, gate = mod[:, :D], mod[:, D:2 * D], mod[:, 2 * D:]
    s1p = (1.0 + scale)[:, None, :]
    shift = shift[:, None, :]
    ctx_rows = jnp.full((B,), B, jnp.int32)

    w_bf = w_in[0].astype(jnp.bfloat16)
    k_c, v_c, hq_c, hi_c, hff_c, hfb_c = _in_projection(
        ctx, s1p[ctx_rows], shift[ctx_rows], w_bf, _CTX_GROUPS, tm=ctx.shape[1])
    q, k, v, ag, hq, hi, hff, hfb, hg = _in_projection(
        x, s1p[:B], shift[:B], w_bf, _LATENT_GROUPS, tm=ROW_TILE, rope_tables=_rope_tables(N))

    att = _diff_attention(q, k_c, v_c, k, v, diff_lambda, diff_subln_gain)

    s_zero = jnp.zeros((B, 2, HEADS, HEAD_W, HEAD_W), jnp.float32)
    _, _, s_ctx = _hgrn2_bidir(hq_c, hi_c, hff_c, hfb_c, hgrn_lower_bound, s_zero, n_sub=ctx.shape[1] // HGRN_CHUNK)
    o_f, o_b, _ = _hgrn2_bidir(hq, hi, hff, hfb, hgrn_lower_bound, s_ctx, n_sub=HGRN_CHUNKS_PER_STEP)

    return _merge(att, ag, o_f, o_b, hg, x, gate[:B, None, :], w_out[0].astype(jnp.bfloat16),
                  hgrn_norm_gain, ln_gain, ln_bias, tm=ROW_TILE)
```

```python
import functools
import math

import numpy as np
import jax
import jax.numpy as jnp
from jax import lax
from jax.experimental import pallas as pl
from jax.experimental.pallas import tpu as pltpu

D_MODEL = 1024
DEPTH = 1
GRID_W = 64
HEADS = 4
HEAD_W = 128
MAP_D = 64
BRANCH_W = HEADS * HEAD_W
ROPE_BASE = 10000.0
ROPE_FREQS = MAP_D // 4
EPS = 1e-5
LAMBDA_INIT = 0.8 - 0.6 * math.exp(-0.3 * 0)
ALPHA = (2.0 * DEPTH) ** 0.25

V7X_VMEM_LIMIT_BYTES = 56 * 1024 * 1024
SUBLANES = 8
LANES = 128
ROW_TILE = 512
ATTN_TQ = 512
ATTN_MAX_TK = 1408
HGRN_CHUNKS_PER_STEP = 4
HGRN_CHUNK = 128
HGRN_LEVELS = 7

_NT = (((1,), (1,)), ((), ()))
_TN = (((0,), (0,)), ((), ()))


def _silu(x):
    return x * jax.nn.sigmoid(x)


def _mod_kernel(c_ref, w_ref, b_ref, o_ref):
    a = _silu(c_ref[...]).astype(jnp.bfloat16)
    o_ref[...] = jnp.dot(a, w_ref[0].astype(jnp.bfloat16), preferred_element_type=jnp.float32) + b_ref[...]


def _modulation(cvec, w_ada, b_ada):
    rows, d = cvec.shape
    n_out = w_ada.shape[2]
    bn = 1024
    return pl.pallas_call(
        _mod_kernel,
        grid=(n_out // bn,),
        in_specs=[pl.BlockSpec((rows, d), lambda j: (0, 0)),
                  pl.BlockSpec((1, d, bn), lambda j: (0, 0, j)),
                  pl.BlockSpec((1, bn), lambda j: (0, j))],
        out_specs=pl.BlockSpec((rows, bn), lambda j: (0, j)),
        out_shape=jax.ShapeDtypeStruct((rows, n_out), jnp.float32),
        compiler_params=pltpu.CompilerParams(vmem_limit_bytes=V7X_VMEM_LIMIT_BYTES),
        name="modulation",
    )(cvec, w_ada, b_ada)


_LATENT_GROUPS = ((0, "rope_q", jnp.bfloat16), (1, "rope", jnp.bfloat16), (2, "transposed", jnp.bfloat16),
                  (3, "plain", jnp.bfloat16), (4, "silu", jnp.float32), (5, "plain", jnp.bfloat16),
                  (6, "plain", jnp.float32), (7, "plain", jnp.float32), (8, "plain", jnp.bfloat16))
_CTX_GROUPS = ((1, "plain", jnp.bfloat16), (2, "transposed", jnp.bfloat16), (4, "silu", jnp.float32),
               (5, "plain", jnp.bfloat16), (6, "plain", jnp.float32), (7, "plain", jnp.float32))
Q_SCALE = math.log2(math.e) / math.sqrt(MAP_D)


def _proj_kernel(groups, use_rope, *refs):
    if use_rope:
        x_ref, s1p_ref, sh_ref, w_ref, cos_row_ref, sin_row_ref, cos_col_ref, sin_col_ref = refs[:8]
        out_refs = refs[8:]
    else:
        x_ref, s1p_ref, sh_ref, w_ref = refs[:4]
        out_refs = refs[4:]
    h = (x_ref[0] * s1p_ref[0] + sh_ref[0]).astype(jnp.bfloat16)
    if use_rope:
        by_row = (lax.broadcasted_iota(jnp.int32, (GRID_W, HEAD_W), 1) & (MAP_D - 1)) < 2 * ROPE_FREQS

        def token_table(row_ref, col_ref):
            grid_rows = [jnp.where(by_row, row_ref[r:r + 1, :], col_ref[...]) for r in range(row_ref.shape[0])]
            return jnp.concatenate([jnp.concatenate(grid_rows, axis=0)] * HEADS, axis=1)

        cos = token_table(cos_row_ref, cos_col_ref)
        sin = token_table(sin_row_ref, sin_col_ref)
        lane = lax.broadcasted_iota(jnp.int32, cos.shape, 1)
        first_half = (lane & (2 * ROPE_FREQS - 1)) < ROPE_FREQS
    for (g, kind, dt), o_ref in zip(groups, out_refs):
        p = jnp.dot(h, w_ref[:, g * BRANCH_W:(g + 1) * BRANCH_W], preferred_element_type=jnp.float32)
        if kind in ("rope", "rope_q"):
            partner = jnp.where(first_half,
                                pltpu.roll(p, BRANCH_W - ROPE_FREQS, axis=1),
                                pltpu.roll(p, ROPE_FREQS, axis=1))
            p = p * cos + partner * sin
            if kind == "rope_q":
                p = p * Q_SCALE
        elif kind == "silu":
            p = _silu(p)
        elif kind == "transposed":
            p = p.T
        o_ref[0] = p.astype(dt)


def _in_projection(x, s1p, sh, w_bf, groups, tm, rope_tables=None):
    B, T, D = x.shape
    use_rope = rope_tables is not None
    in_specs = [pl.BlockSpec((1, tm, D), lambda b, i: (b, i, 0)),
                pl.BlockSpec((1, 1, D), lambda b, i: (b, 0, 0)),
                pl.BlockSpec((1, 1, D), lambda b, i: (b, 0, 0)),
                pl.BlockSpec(w_bf.shape, lambda b, i: (0, 0), pipeline_mode=pl.Buffered(1))]
    args = [x, s1p, sh, w_bf]
    if use_rope:
        assert tm % (GRID_W * SUBLANES) == 0
        in_specs += [pl.BlockSpec((tm // GRID_W, HEAD_W), lambda b, i: (i, 0))] * 2
        in_specs += [pl.BlockSpec((GRID_W, HEAD_W), lambda b, i: (0, 0))] * 2
        args += list(rope_tables)
    out_specs = [pl.BlockSpec((1, BRANCH_W, tm), lambda b, i: (b, 0, i)) if kind == "transposed"
                 else pl.BlockSpec((1, tm, BRANCH_W), lambda b, i: (b, i, 0)) for _, kind, _ in groups]
    out_shape = [jax.ShapeDtypeStruct((B, BRANCH_W, T) if kind == "transposed" else (B, T, BRANCH_W), dt)
                 for _, kind, dt in groups]
    return pl.pallas_call(
        functools.partial(_proj_kernel, groups, use_rope),
        grid=(B, T // tm),
        in_specs=in_specs, out_specs=out_specs, out_shape=out_shape,
        compiler_params=pltpu.CompilerParams(
            dimension_semantics=("parallel", "parallel"), vmem_limit_bytes=V7X_VMEM_LIMIT_BYTES),
        name="in_proj_rope" if use_rope else "in_proj_ctx",
    )(*args)


def _rope_tables(n_tokens):
    lane = np.arange(HEAD_W) % MAP_D
    inv_freq = jnp.asarray(ROPE_BASE, jnp.float32) ** (-jnp.arange(ROPE_FREQS, dtype=jnp.float32) / ROPE_FREQS)
    freq = inv_freq[lane % ROPE_FREQS]
    sign = np.where((lane % (2 * ROPE_FREQS)) < ROPE_FREQS, -1.0, 1.0).astype(np.float32)
    row_ang = jnp.arange(n_tokens // GRID_W, dtype=jnp.float32)[:, None] * freq
    col_ang = jnp.arange(GRID_W, dtype=jnp.float32)[:, None] * freq
    return jnp.cos(row_ang), jnp.sin(row_ang) * sign, jnp.cos(col_ang), jnp.sin(col_ang) * sign


def _attn_kernel(tk, q_ref, kc_ref, vtc_ref, kl_ref, vtl_ref, lam_ref, gain_ref, o_ref,
                 k_ref, vt_ref, s_ref, p_ref, acc_ref):
    t_ctx = kc_ref.shape[1]

    @pl.when(pl.program_id(2) == 0)
    def _():
        k_ref[:t_ctx, :] = kc_ref[0]
        k_ref[t_ctx:, :] = kl_ref[0]
        vt_ref[:, :t_ctx] = vtc_ref[0]
        vt_ref[:, t_ctx:] = vtl_ref[0]

    qt = q_ref[0].astype(jnp.float32).T
    row = lax.broadcasted_iota(jnp.int32, qt.shape, 0)
    qm = tuple(jnp.where(sel, qt, 0.0).astype(jnp.bfloat16) for sel in (row < MAP_D, row >= MAP_D))
    tq = qt.shape[1]
    n_chunks = k_ref.shape[0] // tk

    def scores(j):
        kb = k_ref[pl.ds(pl.multiple_of(j * tk, tk), tk), :]
        smax = []
        for m in range(2):
            s = jnp.dot(kb, qm[m], preferred_element_type=jnp.float32)
            s_ref[m] = s
            smax.append(jnp.max(s, axis=0, keepdims=True))
        return tuple(smax)

    def weights(smax, ml):
        new_ml, corrs = [], []
        for m in range(2):
            mx, l = ml[m]
            mx_new = jnp.maximum(mx, smax[m])
            corr = jnp.exp2(mx - mx_new)
            p = jnp.exp2(s_ref[m] - mx_new)
            p_ref[m] = p.astype(jnp.bfloat16)
            new_ml.append((mx_new, l * corr + jnp.sum(p, axis=0, keepdims=True)))
            corrs.append(corr)
        return tuple(new_ml), tuple(corrs)

    def values(j, corr):
        vtb = vt_ref[:, pl.ds(pl.multiple_of(j * tk, tk), tk)]
        for m in range(2):
            acc_ref[m] = acc_ref[m] * corr[m] + jnp.dot(vtb, p_ref[m], preferred_element_type=jnp.float32)

    ml = tuple((jnp.full((1, tq), -jnp.inf, jnp.float32), jnp.zeros((1, tq), jnp.float32)) for _ in range(2))
    acc_ref[...] = jnp.zeros_like(acc_ref)
    ml, corr = weights(scores(0), ml)
    smax = scores(1)

    def body(j, carry):
        ml, smax, corr_prev = carry
        values(j - 1, corr_prev)
        ml, corr = weights(smax, ml)
        return ml, scores(j + 1), corr

    ml, smax, corr = lax.fori_loop(1, n_chunks - 1, body, (ml, smax, corr))
    values(n_chunks - 2, corr)
    ml, corr = weights(smax, ml)
    values(n_chunks - 1, corr)

    lp = lam_ref[0]
    lam = (jnp.exp(jnp.sum(lp[0:1] * lp[1:2], axis=-1, keepdims=True))
           - jnp.exp(jnp.sum(lp[2:3] * lp[3:4], axis=-1, keepdims=True)) + LAMBDA_INIT)
    (_, l0), (_, l1) = ml
    ot = acc_ref[0] / l0 - lam * (acc_ref[1] / l1)
    ot = ot * lax.rsqrt(jnp.mean(ot * ot, axis=0, keepdims=True) + EPS)
    o_ref[0] = (ot.T * gain_ref[...] * (1.0 - LAMBDA_INIT)).astype(o_ref.dtype)


def _key_chunk(n_keys):
    return max(t for t in range(LANES, ATTN_MAX_TK + 1, LANES) if n_keys % t == 0)


def _diff_attention(q, k_ctx, vt_ctx, k, vt, diff_lambda, gain):
    B, N, _ = q.shape
    T_ctx = k_ctx.shape[1]
    Tk = T_ctx + N
    tq, tk = ATTN_TQ, _key_chunk(Tk)
    assert N % tq == 0 and Tk // tk >= 3 and T_ctx % LANES == 0
    return pl.pallas_call(
        functools.partial(_attn_kernel, tk),
        grid=(B, HEADS, N // tq),
        in_specs=[pl.BlockSpec((1, tq, HEAD_W), lambda b, h, i: (b, i, h)),
                  pl.BlockSpec((1, T_ctx, HEAD_W), lambda b, h, i: (b, 0, h)),
                  pl.BlockSpec((1, HEAD_W, T_ctx), lambda b, h, i: (b, h, 0)),
                  pl.BlockSpec((1, N, HEAD_W), lambda b, h, i: (b, 0, h)),
                  pl.BlockSpec((1, HEAD_W, N), lambda b, h, i: (b, h, 0)),
                  pl.BlockSpec((1, 4, MAP_D), lambda b, h, i: (0, 0, 0)),
                  pl.BlockSpec((1, HEAD_W), lambda b, h, i: (0, 0))],
        out_specs=pl.BlockSpec((1, tq, HEAD_W), lambda b, h, i: (b, i, h)),
        out_shape=jax.ShapeDtypeStruct((B, N, BRANCH_W), jnp.bfloat16),
        scratch_shapes=[pltpu.VMEM((Tk, HEAD_W), jnp.bfloat16), pltpu.VMEM((HEAD_W, Tk), jnp.bfloat16),
                        pltpu.VMEM((2, tk, tq), jnp.float32), pltpu.VMEM((2, tk, tq), jnp.bfloat16),
                        pltpu.VMEM((2, HEAD_W, tq), jnp.float32)],
        compiler_params=pltpu.CompilerParams(
            dimension_semantics=("parallel", "parallel", "arbitrary"),
            vmem_limit_bytes=V7X_VMEM_LIMIT_BYTES),
        name="diff_attention",
    )(q, k_ctx, vt_ctx, k, vt, diff_lambda, gain)


def _level_maps():
    t = np.arange(HGRN_CHUNK)[:, None]
    s = np.arange(HGRN_CHUNK)[None, :]
    x = t ^ s
    lvl = np.where(x > 0, np.floor(np.log2(np.maximum(x, 1))).astype(np.int32) + 1, 0)
    lvl = np.where(s > t, -1, lvl).astype(np.int32)
    return np.stack([lvl, lvl.T])


def _shift_down(x, s):
    return pltpu.roll(x, s, axis=1)


def _shift_up(x, s):
    return pltpu.roll(x, SUBLANES - s, axis=1)


def _neg_abs(x):
    bits = lax.bitcast_convert_type(x, jnp.uint32) | jnp.uint32(0x80000000)
    return lax.bitcast_convert_type(bits, jnp.float32)


def _hgrn_chain(q, z, v_bf, lb, st, masks, reverse):
    C = HGRN_CHUNK
    groups = C // SUBLANES
    sig = jax.nn.sigmoid(z)
    g = jnp.log2(lb + (1.0 - lb) * sig)
    kk = (1.0 - lb) * (1.0 - sig)
    grouped = (groups, SUBLANES, HEAD_W)
    sub = lax.broadcasted_iota(jnp.int32, grouped, 1)

    bg = g.reshape(grouped)
    for s in (1, 2, 4):
        if reverse:
            bg = bg + jnp.where(sub < SUBLANES - s, _shift_up(bg, s), 0.0)
        else:
            bg = bg + jnp.where(sub >= s, _shift_down(bg, s), 0.0)
    parts = [bg[r] for r in range(groups)]
    order = range(groups - 2, -1, -1) if reverse else range(1, groups)
    for r in order:
        prev = parts[r + 1][0:1, :] if reverse else parts[r - 1][SUBLANES - 1:SUBLANES, :]
        parts[r] = parts[r] + prev
    b = jnp.concatenate(parts, axis=0)
    bg = b.reshape(grouped)

    q_bf = q.astype(jnp.bfloat16)
    kk_bf = kk.astype(jnp.bfloat16)

    def rows(x, r):
        return x[r * SUBLANES:(r + 1) * SUBLANES, :]

    a_parts = [None] * groups

    own = bg
    nbr = _shift_up(bg, 1) if reverse else _shift_down(bg, 1)
    for j in range(1, HGRN_LEVELS + 1):
        h = 1 << (j - 1)
        if 2 * h <= SUBLANES:
            upper = (sub & h) != 0
            ref = jnp.where(upper, own, nbr) if reverse else jnp.where(upper, nbr, own)
            w = jnp.exp2(_neg_abs(bg - ref)).reshape(C, HEAD_W).astype(jnp.bfloat16)
            if 4 * h <= SUBLANES:
                if reverse:
                    own = jnp.where(upper, _shift_down(own, h), own)
                    nbr = jnp.where(upper, nbr, _shift_up(nbr, h))
                else:
                    own = jnp.where(upper, own, _shift_up(own, h))
                    nbr = jnp.where(upper, _shift_down(nbr, h), nbr)
            pm = lax.dot_general(q_bf * w, kk_bf * w, _NT, preferred_element_type=jnp.float32)
            a_parts = [jnp.where(rows(masks[j], r), rows(pm, r), 0.0 if a_parts[r] is None else a_parts[r])
                       for r in range(groups)]
        else:
            n_blk, half_groups = C // (2 * h), h // SUBLANES
            blocked = (n_blk, 2 * h, HEAD_W)
            b_blk = b.reshape(blocked)
            ref = b_blk[:, h:h + 1, :] if reverse else b_blk[:, h - 1:h, :]
            w_blk = jnp.exp2(_neg_abs(b_blk - ref))
            q_half, k_half = (slice(0, h), slice(h, 2 * h)) if reverse else (slice(h, 2 * h), slice(0, h))
            q_t = (q.reshape(blocked)[:, q_half] * w_blk[:, q_half]).reshape(C // 2, HEAD_W)
            k_t = kk.reshape(blocked)[:, k_half] * w_blk[:, k_half]
            zeros = jnp.zeros_like(k_t)
            k_t = jnp.concatenate([zeros, k_t] if reverse else [k_t, zeros], axis=1).reshape(C, HEAD_W)
            pm = lax.dot_general(q_t.astype(jnp.bfloat16), k_t.astype(jnp.bfloat16), _NT,
                                 preferred_element_type=jnp.float32)
            for blk in range(n_blk):
                for i in range(half_groups):
                    r = blk * 2 * half_groups + (0 if reverse else half_groups) + i
                    piece = rows(pm, blk * half_groups + i)
                    if n_blk == 1:
                        a_parts[r] = a_parts[r] + piece
                    else:
                        a_parts[r] = jnp.where(rows(masks[j], r), piece, a_parts[r])
    a = jnp.concatenate(a_parts, axis=0)

    o = jnp.dot(a.astype(jnp.bfloat16), v_bf, preferred_element_type=jnp.float32)
    o = o + lax.dot_general(q_bf * jnp.exp2(b).astype(jnp.bfloat16), st.astype(jnp.bfloat16), _NT,
                            preferred_element_type=jnp.float32)
    o = o + jnp.sum(q * kk, axis=-1, keepdims=True) * v_bf.astype(jnp.float32)
    b_tot = b[0:1, :] if reverse else b[C - 1:C, :]
    k_hat = kk_bf * jnp.exp2(b_tot - b).astype(jnp.bfloat16)
    st_new = st * jnp.exp2(b_tot) + lax.dot_general(v_bf, k_hat, _TN, preferred_element_type=jnp.float32)
    return o, st_new


def _hgrn_kernel(n_sub, has_s0, qf_ref, vf_ref, zf_ref, qb_ref, vb_ref, zb_ref, lbp_ref, lvl_ref, *refs):
    s0_ref, (of_ref, ob_ref, sfin_ref, st_ref) = (refs[0], refs[1:]) if has_s0 else (None, refs)
    i = pl.program_id(1)
    C = HGRN_CHUNK

    @pl.when(i == 0)
    def _():
        st_ref[...] = jnp.zeros_like(st_ref) if s0_ref is None else s0_ref[0]

    dirs = ((qf_ref, vf_ref, zf_ref, of_ref), (qb_ref, vb_ref, zb_ref, ob_ref))
    masks, lbs = [], []
    for d in range(2):
        lvl = lvl_ref[d]
        masks.append([lvl == j for j in range(HGRN_LEVELS + 1)])
        p0, p1 = lbp_ref[d, 0:1, :], lbp_ref[d, 1:2, :]
        pm = jnp.maximum(p0, p1)
        e0 = jnp.exp(p0 - pm)
        lbs.append(e0 / (e0 + jnp.exp(p1 - pm)))
    states = [[st_ref[d, hh] for hh in range(HEADS)] for d in range(2)]
    for sub in range(n_sub):
        for d, (q_ref, v_ref, z_ref, o_ref) in enumerate(dirs):
            c = n_sub - 1 - sub if d == 1 else sub
            tok = slice(c * C, (c + 1) * C)
            for hh in range(HEADS):
                cols = slice(hh * HEAD_W, (hh + 1) * HEAD_W)
                o, states[d][hh] = _hgrn_chain(q_ref[0, tok, cols], z_ref[0, tok, cols], v_ref[0, tok, cols],
                                               lbs[d][:, cols], states[d][hh], masks[d], reverse=(d == 1))
                o_ref[0, tok, cols] = o.astype(o_ref.dtype)
    for d in range(2):
        for hh in range(HEADS):
            st_ref[d, hh] = states[d][hh]

    @pl.when(i == pl.num_programs(1) - 1)
    def _():
        sfin_ref[0] = st_ref[...]


def _hgrn2_bidir(hq, hi, hff, hfb, lb_param, s0, n_sub):
    B, T, _ = hq.shape
    rows = HGRN_CHUNK * n_sub
    n = T // rows
    assert n * rows == T
    fwd = pl.BlockSpec((1, rows, BRANCH_W), lambda b, i: (b, i, 0))
    bwd = pl.BlockSpec((1, rows, BRANCH_W), lambda b, i: (b, n - 1 - i, 0))
    st_spec = pl.BlockSpec((1, 2, HEADS, HEAD_W, HEAD_W), lambda b, i: (b, 0, 0, 0, 0))
    lvl = jnp.asarray(_level_maps())
    return pl.pallas_call(
        functools.partial(_hgrn_kernel, n_sub, s0 is not None),
        grid=(B, n),
        in_specs=[fwd, fwd, fwd, bwd, bwd, bwd,
                  pl.BlockSpec(lb_param.shape, lambda b, i: (0, 0, 0)),
                  pl.BlockSpec(lvl.shape, lambda b, i: (0, 0, 0))] + ([] if s0 is None else [st_spec]),
        out_specs=[fwd, bwd, st_spec],
        out_shape=[jax.ShapeDtypeStruct((B, T, BRANCH_W), jnp.bfloat16),
                   jax.ShapeDtypeStruct((B, T, BRANCH_W), jnp.bfloat16),
                   jax.ShapeDtypeStruct((B, 2, HEADS, HEAD_W, HEAD_W), jnp.float32)],
        scratch_shapes=[pltpu.VMEM((2, HEADS, HEAD_W, HEAD_W), jnp.float32)],
        compiler_params=pltpu.CompilerParams(
            dimension_semantics=("parallel", "arbitrary"), vmem_limit_bytes=V7X_VMEM_LIMIT_BYTES),
        name="hgrn2_bidir",
    )(hq, hi, hff, hq, hi, hfb, lb_param, lvl, *([] if s0 is None else [s0]))


def _merge_kernel(att_ref, ag_ref, of_ref, ob_ref, hg_ref, x_ref, gate_ref, w_ref, hgain_ref, lng_ref, lnb_ref,
                  o_ref):
    f32 = jnp.float32
    att = att_ref[0].astype(f32) * _silu(ag_ref[0].astype(f32))
    o = of_ref[0].astype(f32) + ob_ref[0].astype(f32)
    parts = []
    for hh in range(HEADS):
        oh = o[:, hh * HEAD_W:(hh + 1) * HEAD_W]
        parts.append(oh * lax.rsqrt(jnp.mean(oh * oh, axis=-1, keepdims=True) + EPS) * hgain_ref[...])
    hg = jnp.concatenate(parts, axis=1) * _silu(hg_ref[0].astype(f32))
    y_in = jnp.concatenate([att, hg], axis=1).astype(jnp.bfloat16)
    y = jnp.dot(y_in, w_ref[...], preferred_element_type=jnp.float32)
    u = ALPHA * x_ref[0] + gate_ref[0] * y
    mu = jnp.mean(u, axis=-1, keepdims=True)
    uc = u - mu
    var = jnp.mean(uc * uc, axis=-1, keepdims=True)
    o_ref[0] = uc * lax.rsqrt(var + EPS) * lng_ref[...] + lnb_ref[...]


def _merge(att, ag, o_f, o_b, hg, x, gate, w_out_bf, hgain, ln_g, ln_b, tm):
    B, N, D = x.shape
    half = pl.BlockSpec((1, tm, BRANCH_W), lambda b, i: (b, i, 0))
    full = pl.BlockSpec((1, tm, D), lambda b, i: (b, i, 0))
    row = lambda w: pl.BlockSpec((1, w), lambda b, i: (0, 0))
    return pl.pallas_call(
        _merge_kernel,
        grid=(B, N // tm),
        in_specs=[half, half, half, half, half, full,
                  pl.BlockSpec((1, 1, D), lambda b, i: (b, 0, 0)),
                  pl.BlockSpec(w_out_bf.shape, lambda b, i: (0, 0)),
                  row(HEAD_W), row(D), row(D)],
        out_specs=full,
        out_shape=jax.ShapeDtypeStruct((B, N, D), jnp.float32),
        compiler_params=pltpu.CompilerParams(
            dimension_semantics=("parallel", "parallel"), vmem_limit_bytes=V7X_VMEM_LIMIT_BYTES),
        name="merge_out_proj_ln",
    )(att, ag, o_f, o_b, hg, x, gate, w_out_bf, hgain, ln_g, ln_b)


def kernel(x, c, ctx, c_ctx, w_ada, b_ada, w_in, w_out, diff_lambda, diff_subln_gain, hgrn_lower_bound,
           hgrn_norm_gain, ln_gain, ln_bias):
    B, N, D = x.shape
    assert DEPTH == 1 and w_ada.shape[0] == 1
    cvec = jnp.concatenate([c, c_ctx[None, :], jnp.zeros((SUBLANES - B - 1, D), c.dtype)], axis=0)
    mod = _modulation(cvec, w_ada, b_ada)
    shift, scale, gate = mod[:, :D], mod[:, D:2 * D], mod[:, 2 * D:]
    s1p = (1.0 + scale)[:, None, :]
    shift = shift[:, None, :]
    ctx_rows = jnp.full((B,), B, jnp.int32)

    w_bf = w_in[0].astype(jnp.bfloat16)
    k_c, v_c, hq_c, hi_c, hff_c, hfb_c = _in_projection(
        ctx, s1p[ctx_rows], shift[ctx_rows], w_bf, _CTX_GROUPS, tm=ctx.shape[1])
    q, k, v, ag, hq, hi, hff, hfb, hg = _in_projection(
        x, s1p[:B], shift[:B], w_bf, _LATENT_GROUPS, tm=ROW_TILE, rope_tables=_rope_tables(N))

    att = _diff_attention(q, k_c, v_c, k, v, diff_lambda, diff_subln_gain)

    _, _, s_ctx = _hgrn2_bidir(hq_c, hi_c, hff_c, hfb_c, hgrn_lower_bound, None, n_sub=ctx.shape[1] // HGRN_CHUNK)
    o_f, o_b, _ = _hgrn2_bidir(hq, hi, hff, hfb, hgrn_lower_bound, s_ctx, n_sub=HGRN_CHUNKS_PER_STEP)

    return _merge(att, ag, o_f, o_b, hg, x, gate[:B, None, :], w_out[0].astype(jnp.bfloat16),
                  hgrn_norm_gain, ln_gain, ln_bias, tm=ROW_TILE)
```

```python
import functools
import math

import numpy as np
import jax
import jax.numpy as jnp
from jax import lax
from jax.experimental import pallas as pl
from jax.experimental.pallas import tpu as pltpu

D_MODEL = 1024
DEPTH = 1
GRID_W = 64
HEADS = 4
HEAD_W = 128
MAP_D = 64
BRANCH_W = HEADS * HEAD_W
ROPE_BASE = 10000.0
ROPE_FREQS = MAP_D // 4
EPS = 1e-5
LAMBDA_INIT = 0.8 - 0.6 * math.exp(-0.3 * 0)
ALPHA = (2.0 * DEPTH) ** 0.25

V7X_VMEM_LIMIT_BYTES = 56 * 1024 * 1024
SUBLANES = 8
LANES = 128
ROW_TILE = 512
ATTN_TQ = 512
ATTN_MAX_TK = 1408
HGRN_CHUNKS_PER_STEP = 4
HGRN_CHUNK = 128
HGRN_LEVELS = 7

_NT = (((1,), (1,)), ((), ()))
_TN = (((0,), (0,)), ((), ()))


def _silu(x):
    return x * jax.nn.sigmoid(x)


def _mod_kernel(c_ref, w_ref, b_ref, o_ref):
    a = _silu(c_ref[...]).astype(jnp.bfloat16)
    o_ref[...] = jnp.dot(a, w_ref[0].astype(jnp.bfloat16), preferred_element_type=jnp.float32) + b_ref[...]


def _modulation(cvec, w_ada, b_ada):
    rows, d = cvec.shape
    n_out = w_ada.shape[2]
    bn = 1024
    return pl.pallas_call(
        _mod_kernel,
        grid=(n_out // bn,),
        in_specs=[pl.BlockSpec((rows, d), lambda j: (0, 0)),
                  pl.BlockSpec((1, d, bn), lambda j: (0, 0, j)),
                  pl.BlockSpec((1, bn), lambda j: (0, j))],
        out_specs=pl.BlockSpec((rows, bn), lambda j: (0, j)),
        out_shape=jax.ShapeDtypeStruct((rows, n_out), jnp.float32),
        compiler_params=pltpu.CompilerParams(vmem_limit_bytes=V7X_VMEM_LIMIT_BYTES),
        name="modulation",
    )(cvec, w_ada, b_ada)


_LATENT_GROUPS = ((0, "rope_q", jnp.bfloat16), (1, "rope", jnp.bfloat16), (2, "transposed", jnp.bfloat16),
                  (3, "plain", jnp.bfloat16), (4, "silu", jnp.float32), (5, "plain", jnp.bfloat16),
                  (6, "plain", jnp.float32), (7, "plain", jnp.float32), (8, "plain", jnp.bfloat16))
_CTX_GROUPS = ((1, "plain", jnp.bfloat16), (2, "transposed", jnp.bfloat16), (4, "silu", jnp.float32),
               (5, "plain", jnp.bfloat16), (6, "plain", jnp.float32), (7, "plain", jnp.float32))
Q_SCALE = math.log2(math.e) / math.sqrt(MAP_D)


def _proj_kernel(groups, use_rope, *refs):
    if use_rope:
        x_ref, s1p_ref, sh_ref, w_ref, cos_row_ref, sin_row_ref, cos_col_ref, sin_col_ref = refs[:8]
        out_refs = refs[8:]
    else:
        x_ref, s1p_ref, sh_ref, w_ref = refs[:4]
        out_refs = refs[4:]
    h = (x_ref[0] * s1p_ref[0] + sh_ref[0]).astype(jnp.bfloat16)
    if use_rope:
        by_row = (lax.broadcasted_iota(jnp.int32, (GRID_W, HEAD_W), 1) & (MAP_D - 1)) < 2 * ROPE_FREQS

        def token_table(row_ref, col_ref):
            grid_rows = [jnp.where(by_row, row_ref[r:r + 1, :], col_ref[...]) for r in range(row_ref.shape[0])]
            return jnp.concatenate([jnp.concatenate(grid_rows, axis=0)] * HEADS, axis=1)

        cos = token_table(cos_row_ref, cos_col_ref)
        sin = token_table(sin_row_ref, sin_col_ref)
        lane = lax.broadcasted_iota(jnp.int32, cos.shape, 1)
        first_half = (lane & (2 * ROPE_FREQS - 1)) < ROPE_FREQS
    for (g, kind, dt), o_ref in zip(groups, out_refs):
        p = jnp.dot(h, w_ref[0, :, g * BRANCH_W:(g + 1) * BRANCH_W].astype(jnp.bfloat16),
                    preferred_element_type=jnp.float32)
        if kind in ("rope", "rope_q"):
            partner = jnp.where(first_half,
                                pltpu.roll(p, BRANCH_W - ROPE_FREQS, axis=1),
                                pltpu.roll(p, ROPE_FREQS, axis=1))
            p = p * cos + partner * sin
            if kind == "rope_q":
                p = p * Q_SCALE
        elif kind == "silu":
            p = _silu(p)
        elif kind == "transposed":
            p = p.T
        o_ref[0] = p.astype(dt)


def _in_projection(x, s1p, sh, w_in, groups, tm, rope_tables=None):
    B, T, D = x.shape
    use_rope = rope_tables is not None
    in_specs = [pl.BlockSpec((1, tm, D), lambda b, i: (b, i, 0)),
                pl.BlockSpec((1, 1, D), lambda b, i: (b, 0, 0)),
                pl.BlockSpec((1, 1, D), lambda b, i: (b, 0, 0)),
                pl.BlockSpec(w_in.shape, lambda b, i: (0, 0, 0), pipeline_mode=pl.Buffered(1))]
    args = [x, s1p, sh, w_in]
    if use_rope:
        assert tm % (GRID_W * SUBLANES) == 0
        in_specs += [pl.BlockSpec((tm // GRID_W, HEAD_W), lambda b, i: (i, 0))] * 2
        in_specs += [pl.BlockSpec((GRID_W, HEAD_W), lambda b, i: (0, 0))] * 2
        args += list(rope_tables)
    out_specs = [pl.BlockSpec((1, BRANCH_W, tm), lambda b, i: (b, 0, i)) if kind == "transposed"
                 else pl.BlockSpec((1, tm, BRANCH_W), lambda b, i: (b, i, 0)) for _, kind, _ in groups]
    out_shape = [jax.ShapeDtypeStruct((B, BRANCH_W, T) if kind == "transposed" else (B, T, BRANCH_W), dt)
                 for _, kind, dt in groups]
    return pl.pallas_call(
        functools.partial(_proj_kernel, groups, use_rope),
        grid=(B, T // tm),
        in_specs=in_specs, out_specs=out_specs, out_shape=out_shape,
        compiler_params=pltpu.CompilerParams(
            dimension_semantics=("parallel", "parallel"), vmem_limit_bytes=V7X_VMEM_LIMIT_BYTES),
        name="in_proj_rope" if use_rope else "in_proj_ctx",
    )(*args)


def _rope_tables(n_tokens):
    inv_freq = jnp.asarray(ROPE_BASE, jnp.float32) ** (-jnp.arange(ROPE_FREQS, dtype=jnp.float32) / ROPE_FREQS)
    freq = jnp.tile(inv_freq, HEAD_W // ROPE_FREQS)
    sign = np.where((np.arange(HEAD_W) % (2 * ROPE_FREQS)) < ROPE_FREQS, -1.0, 1.0).astype(np.float32)
    row_ang = jnp.arange(n_tokens // GRID_W, dtype=jnp.float32)[:, None] * freq
    col_ang = jnp.arange(GRID_W, dtype=jnp.float32)[:, None] * freq
    return jnp.cos(row_ang), jnp.sin(row_ang) * sign, jnp.cos(col_ang), jnp.sin(col_ang) * sign


def _attn_kernel(tk, q_ref, kc_ref, vtc_ref, kl_ref, vtl_ref, lam_ref, gain_ref, o_ref,
                 k_ref, vt_ref, s_ref, p_ref, acc_ref):
    t_ctx = kc_ref.shape[1]

    @pl.when(pl.program_id(2) == 0)
    def _():
        k_ref[:t_ctx, :] = kc_ref[0]
        k_ref[t_ctx:, :] = kl_ref[0]
        vt_ref[:, :t_ctx] = vtc_ref[0]
        vt_ref[:, t_ctx:] = vtl_ref[0]

    qt = q_ref[0].astype(jnp.float32).T
    row = lax.broadcasted_iota(jnp.int32, qt.shape, 0)
    qm = tuple(jnp.where(sel, qt, 0.0).astype(jnp.bfloat16) for sel in (row < MAP_D, row >= MAP_D))
    tq = qt.shape[1]
    n_chunks = k_ref.shape[0] // tk

    def scores(j):
        kb = k_ref[pl.ds(pl.multiple_of(j * tk, tk), tk), :]
        smax = []
        for m in range(2):
            s = jnp.dot(kb, qm[m], preferred_element_type=jnp.float32)
            s_ref[m] = s
            smax.append(jnp.max(s, axis=0, keepdims=True))
        return tuple(smax)

    def weights(smax, ml):
        new_ml, corrs = [], []
        for m in range(2):
            mx, l = ml[m]
            mx_new = jnp.maximum(mx, smax[m])
            corr = jnp.exp2(mx - mx_new)
            p = jnp.exp2(s_ref[m] - mx_new)
            p_ref[m] = p.astype(jnp.bfloat16)
            new_ml.append((mx_new, l * corr + jnp.sum(p, axis=0, keepdims=True)))
            corrs.append(corr)
        return tuple(new_ml), tuple(corrs)

    def values(j, corr):
        vtb = vt_ref[:, pl.ds(pl.multiple_of(j * tk, tk), tk)]
        for m in range(2):
            acc_ref[m] = acc_ref[m] * corr[m] + jnp.dot(vtb, p_ref[m], preferred_element_type=jnp.float32)

    ml = tuple((jnp.full((1, tq), -jnp.inf, jnp.float32), jnp.zeros((1, tq), jnp.float32)) for _ in range(2))
    acc_ref[...] = jnp.zeros_like(acc_ref)
    ml, corr = weights(scores(0), ml)
    smax = scores(1)

    def body(j, carry):
        ml, smax, corr_prev = carry
        values(j - 1, corr_prev)
        ml, corr = weights(smax, ml)
        return ml, scores(j + 1), corr

    ml, smax, corr = lax.fori_loop(1, n_chunks - 1, body, (ml, smax, corr))
    values(n_chunks - 2, corr)
    ml, corr = weights(smax, ml)
    values(n_chunks - 1, corr)

    lp = lam_ref[0]
    lam = (jnp.exp(jnp.sum(lp[0:1] * lp[1:2], axis=-1, keepdims=True))
           - jnp.exp(jnp.sum(lp[2:3] * lp[3:4], axis=-1, keepdims=True)) + LAMBDA_INIT)
    (_, l0), (_, l1) = ml
    ot = acc_ref[0] / l0 - lam * (acc_ref[1] / l1)
    ot = ot * lax.rsqrt(jnp.mean(ot * ot, axis=0, keepdims=True) + EPS)
    o_ref[0] = (ot.T * gain_ref[...] * (1.0 - LAMBDA_INIT)).astype(o_ref.dtype)


def _key_chunk(n_keys):
    return max(t for t in range(LANES, ATTN_MAX_TK + 1, LANES) if n_keys % t == 0)


def _diff_attention(q, k_ctx, vt_ctx, k, vt, diff_lambda, gain):
    B, N, _ = q.shape
    T_ctx = k_ctx.shape[1]
    Tk = T_ctx + N
    tq, tk = ATTN_TQ, _key_chunk(Tk)
    assert N % tq == 0 and Tk // tk >= 3 and T_ctx % LANES == 0
    return pl.pallas_call(
        functools.partial(_attn_kernel, tk),
        grid=(B, HEADS, N // tq),
        in_specs=[pl.BlockSpec((1, tq, HEAD_W), lambda b, h, i: (b, i, h)),
                  pl.BlockSpec((1, T_ctx, HEAD_W), lambda b, h, i: (b, 0, h)),
                  pl.BlockSpec((1, HEAD_W, T_ctx), lambda b, h, i: (b, h, 0)),
                  pl.BlockSpec((1, N, HEAD_W), lambda b, h, i: (b, 0, h)),
                  pl.BlockSpec((1, HEAD_W, N), lambda b, h, i: (b, h, 0)),
                  pl.BlockSpec((1, 4, MAP_D), lambda b, h, i: (0, 0, 0)),
                  pl.BlockSpec((1, HEAD_W), lambda b, h, i: (0, 0))],
        out_specs=pl.BlockSpec((1, tq, HEAD_W), lambda b, h, i: (b, i, h)),
        out_shape=jax.ShapeDtypeStruct((B, N, BRANCH_W), jnp.bfloat16),
        scratch_shapes=[pltpu.VMEM((Tk, HEAD_W), jnp.bfloat16), pltpu.VMEM((HEAD_W, Tk), jnp.bfloat16),
                        pltpu.VMEM((2, tk, tq), jnp.float32), pltpu.VMEM((2, tk, tq), jnp.bfloat16),
                        pltpu.VMEM((2, HEAD_W, tq), jnp.float32)],
        compiler_params=pltpu.CompilerParams(
            dimension_semantics=("parallel", "parallel", "arbitrary"),
            vmem_limit_bytes=V7X_VMEM_LIMIT_BYTES),
        name="diff_attention",
    )(q, k_ctx, vt_ctx, k, vt, diff_lambda, gain)


def _level_maps():
    t = np.arange(HGRN_CHUNK)[:, None]
    s = np.arange(HGRN_CHUNK)[None, :]
    x = t ^ s
    lvl = np.where(x > 0, np.floor(np.log2(np.maximum(x, 1))).astype(np.int32) + 1, 0)
    lvl = np.where(s > t, -1, lvl).astype(np.int32)
    return np.stack([lvl, lvl.T])


def _shift_down(x, s):
    return pltpu.roll(x, s, axis=1)


def _shift_up(x, s):
    return pltpu.roll(x, SUBLANES - s, axis=1)


def _neg_abs(x):
    bits = lax.bitcast_convert_type(x, jnp.uint32) | jnp.uint32(0x80000000)
    return lax.bitcast_convert_type(bits, jnp.float32)


def _hgrn_chain(q, z, v_bf, lb, st, masks, reverse):
    C = HGRN_CHUNK
    groups = C // SUBLANES
    sig = jax.nn.sigmoid(z)
    g = jnp.log2(lb + (1.0 - lb) * sig)
    kk = (1.0 - lb) * (1.0 - sig)
    grouped = (groups, SUBLANES, HEAD_W)
    sub = lax.broadcasted_iota(jnp.int32, grouped, 1)

    bg = g.reshape(grouped)
    for s in (1, 2, 4):
        if reverse:
            bg = bg + jnp.where(sub < SUBLANES - s, _shift_up(bg, s), 0.0)
        else:
            bg = bg + jnp.where(sub >= s, _shift_down(bg, s), 0.0)
    parts = [bg[r] for r in range(groups)]
    order = range(groups - 2, -1, -1) if reverse else range(1, groups)
    for r in order:
        prev = parts[r + 1][0:1, :] if reverse else parts[r - 1][SUBLANES - 1:SUBLANES, :]
        parts[r] = parts[r] + prev
    b = jnp.concatenate(parts, axis=0)
    bg = b.reshape(grouped)

    q_bf = q.astype(jnp.bfloat16)
    kk_bf = kk.astype(jnp.bfloat16)

    def rows(x, r):
        return x[r * SUBLANES:(r + 1) * SUBLANES, :]

    a_parts = [None] * groups

    own = bg
    nbr = _shift_up(bg, 1) if reverse else _shift_down(bg, 1)
    for j in range(1, HGRN_LEVELS + 1):
        h = 1 << (j - 1)
        if 2 * h <= SUBLANES:
            upper = (sub & h) != 0
            ref = jnp.where(upper, own, nbr) if reverse else jnp.where(upper, nbr, own)
            w = jnp.exp2(_neg_abs(bg - ref)).reshape(C, HEAD_W).astype(jnp.bfloat16)
            if 4 * h <= SUBLANES:
                if reverse:
                    own = jnp.where(upper, _shift_down(own, h), own)
                    nbr = jnp.where(upper, nbr, _shift_up(nbr, h))
                else:
                    own = jnp.where(upper, own, _shift_up(own, h))
                    nbr = jnp.where(upper, _shift_down(nbr, h), nbr)
            pm = lax.dot_general(q_bf * w, kk_bf * w, _NT, preferred_element_type=jnp.float32)
            a_parts = [jnp.where(rows(masks[j], r), rows(pm, r), 0.0 if a_parts[r] is None else a_parts[r])
                       for r in range(groups)]
        else:
            n_blk, half_groups = C // (2 * h), h // SUBLANES
            blocked = (n_blk, 2 * h, HEAD_W)
            b_blk = b.reshape(blocked)
            ref = b_blk[:, h:h + 1, :] if reverse else b_blk[:, h - 1:h, :]
            w_blk = jnp.exp2(_neg_abs(b_blk - ref))
            q_half, k_half = (slice(0, h), slice(h, 2 * h)) if reverse else (slice(h, 2 * h), slice(0, h))
            q_t = (q.reshape(blocked)[:, q_half] * w_blk[:, q_half]).reshape(C // 2, HEAD_W)
            k_t = kk.reshape(blocked)[:, k_half] * w_blk[:, k_half]
            zeros = jnp.zeros_like(k_t)
            k_t = jnp.concatenate([zeros, k_t] if reverse else [k_t, zeros], axis=1).reshape(C, HEAD_W)
            pm = lax.dot_general(q_t.astype(jnp.bfloat16), k_t.astype(jnp.bfloat16), _NT,
                                 preferred_element_type=jnp.float32)
            for blk in range(n_blk):
                for i in range(half_groups):
                    r = blk * 2 * half_groups + (0 if reverse else half_groups) + i
                    piece = rows(pm, blk * half_groups + i)
                    if n_blk == 1:
                        a_parts[r] = a_parts[r] + piece
                    else:
                        a_parts[r] = jnp.where(rows(masks[j], r), piece, a_parts[r])
    a = jnp.concatenate(a_parts, axis=0)

    o = jnp.dot(a.astype(jnp.bfloat16), v_bf, preferred_element_type=jnp.float32)
    o = o + lax.dot_general(q_bf * jnp.exp2(b).astype(jnp.bfloat16), st.astype(jnp.bfloat16), _NT,
                            preferred_element_type=jnp.float32)
    o = o + jnp.sum(q * kk, axis=-1, keepdims=True) * v_bf.astype(jnp.float32)
    b_tot = b[0:1, :] if reverse else b[C - 1:C, :]
    k_hat = kk_bf * jnp.exp2(b_tot - b).astype(jnp.bfloat16)
    st_new = st * jnp.exp2(b_tot) + lax.dot_general(v_bf, k_hat, _TN, preferred_element_type=jnp.float32)
    return o, st_new


def _hgrn_kernel(n_sub, has_s0, qf_ref, vf_ref, zf_ref, qb_ref, vb_ref, zb_ref, lbp_ref, lvl_ref, *refs):
    s0_ref, (of_ref, ob_ref, sfin_ref, st_ref) = (refs[0], refs[1:]) if has_s0 else (None, refs)
    i = pl.program_id(1)
    C = HGRN_CHUNK

    @pl.when(i == 0)
    def _():
        st_ref[...] = jnp.zeros_like(st_ref) if s0_ref is None else s0_ref[0]

    dirs = ((qf_ref, vf_ref, zf_ref, of_ref), (qb_ref, vb_ref, zb_ref, ob_ref))
    masks, lbs = [], []
    for d in range(2):
        lvl = lvl_ref[d]
        masks.append([lvl == j for j in range(HGRN_LEVELS + 1)])
        p0, p1 = lbp_ref[d, 0:1, :], lbp_ref[d, 1:2, :]
        pm = jnp.maximum(p0, p1)
        e0 = jnp.exp(p0 - pm)
        lbs.append(e0 / (e0 + jnp.exp(p1 - pm)))
    states = [[st_ref[d, hh] for hh in range(HEADS)] for d in range(2)]
    for sub in range(n_sub):
        for d, (q_ref, v_ref, z_ref, o_ref) in enumerate(dirs):
            c = n_sub - 1 - sub if d == 1 else sub
            tok = slice(c * C, (c + 1) * C)
            for hh in range(HEADS):
                cols = slice(hh * HEAD_W, (hh + 1) * HEAD_W)
                o, states[d][hh] = _hgrn_chain(q_ref[0, tok, cols], z_ref[0, tok, cols], v_ref[0, tok, cols],
                                               lbs[d][:, cols], states[d][hh], masks[d], reverse=(d == 1))
                o_ref[0, tok, cols] = o.astype(o_ref.dtype)
    for d in range(2):
        for hh in range(HEADS):
            st_ref[d, hh] = states[d][hh]

    @pl.when(i == pl.num_programs(1) - 1)
    def _():
        sfin_ref[0] = st_ref[...]


def _hgrn2_bidir(hq, hi, hff, hfb, lb_param, s0, n_sub):
    B, T, _ = hq.shape
    rows = HGRN_CHUNK * n_sub
    n = T // rows
    assert n * rows == T
    fwd = pl.BlockSpec((1, rows, BRANCH_W), lambda b, i: (b, i, 0))
    bwd = pl.BlockSpec((1, rows, BRANCH_W), lambda b, i: (b, n - 1 - i, 0))
    st_spec = pl.BlockSpec((1, 2, HEADS, HEAD_W, HEAD_W), lambda b, i: (b, 0, 0, 0, 0))
    lvl = jnp.asarray(_level_maps())
    return pl.pallas_call(
        functools.partial(_hgrn_kernel, n_sub, s0 is not None),
        grid=(B, n),
        in_specs=[fwd, fwd, fwd, bwd, bwd, bwd,
                  pl.BlockSpec(lb_param.shape, lambda b, i: (0, 0, 0)),
                  pl.BlockSpec(lvl.shape, lambda b, i: (0, 0, 0))] + ([] if s0 is None else [st_spec]),
        out_specs=[fwd, bwd, st_spec],
        out_shape=[jax.ShapeDtypeStruct((B, T, BRANCH_W), jnp.bfloat16),
                   jax.ShapeDtypeStruct((B, T, BRANCH_W), jnp.bfloat16),
                   jax.ShapeDtypeStruct((B, 2, HEADS, HEAD_W, HEAD_W), jnp.float32)],
        scratch_shapes=[pltpu.VMEM((2, HEADS, HEAD_W, HEAD_W), jnp.float32)],
        compiler_params=pltpu.CompilerParams(
            dimension_semantics=("parallel", "arbitrary"), vmem_limit_bytes=V7X_VMEM_LIMIT_BYTES),
        name="hgrn2_bidir",
    )(hq, hi, hff, hq, hi, hfb, lb_param, lvl, *([] if s0 is None else [s0]))


def _merge_kernel(att_ref, ag_ref, of_ref, ob_ref, hg_ref, x_ref, gate_ref, w_ref, hgain_ref, lng_ref, lnb_ref,
                  o_ref):
    f32 = jnp.float32
    att = att_ref[0].astype(f32) * _silu(ag_ref[0].astype(f32))
    o = of_ref[0].astype(f32) + ob_ref[0].astype(f32)
    parts = []
    for hh in range(HEADS):
        oh = o[:, hh * HEAD_W:(hh + 1) * HEAD_W]
        parts.append(oh * lax.rsqrt(jnp.mean(oh * oh, axis=-1, keepdims=True) + EPS) * hgain_ref[...])
    hg = jnp.concatenate(parts, axis=1) * _silu(hg_ref[0].astype(f32))
    y_in = jnp.concatenate([att, hg], axis=1).astype(jnp.bfloat16)
    y = jnp.dot(y_in, w_ref[...], preferred_element_type=jnp.float32)
    u = ALPHA * x_ref[0] + gate_ref[0] * y
    mu = jnp.mean(u, axis=-1, keepdims=True)
    uc = u - mu
    var = jnp.mean(uc * uc, axis=-1, keepdims=True)
    o_ref[0] = uc * lax.rsqrt(var + EPS) * lng_ref[...] + lnb_ref[...]


def _merge(att, ag, o_f, o_b, hg, x, gate, w_out_bf, hgain, ln_g, ln_b, tm):
    B, N, D = x.shape
    half = pl.BlockSpec((1, tm, BRANCH_W), lambda b, i: (b, i, 0))
    full = pl.BlockSpec((1, tm, D), lambda b, i: (b, i, 0))
    row = lambda w: pl.BlockSpec((1, w), lambda b, i: (0, 0))
    return pl.pallas_call(
        _merge_kernel,
        grid=(B, N // tm),
        in_specs=[half, half, half, half, half, full,
                  pl.BlockSpec((1, 1, D), lambda b, i: (b, 0, 0)),
                  pl.BlockSpec(w_out_bf.shape, lambda b, i: (0, 0)),
                  row(HEAD_W), row(D), row(D)],
        out_specs=full,
        out_shape=jax.ShapeDtypeStruct((B, N, D), jnp.float32),
        compiler_params=pltpu.CompilerParams(
            dimension_semantics=("parallel", "parallel"), vmem_limit_bytes=V7X_VMEM_LIMIT_BYTES),
        name="merge_out_proj_ln",
    )(att, ag, o_f, o_b, hg, x, gate, w_out_bf, hgain, ln_g, ln_b)


def kernel(x, c, ctx, c_ctx, w_ada, b_ada, w_in, w_out, diff_lambda, diff_subln_gain, hgrn_lower_bound,
           hgrn_norm_gain, ln_gain, ln_bias):
    B, N, D = x.shape
    assert DEPTH == 1 and w_ada.shape[0] == 1
    cvec = jnp.concatenate([c, c_ctx[None, :], jnp.zeros((SUBLANES - B - 1, D), c.dtype)], axis=0)
    mod = _modulation(cvec, w_ada, b_ada)
    shift, scale, gate = mod[:, :D], mod[:, D:2 * D], mod[:, 2 * D:]
    s1p = (1.0 + scale)[:, None, :]
    shift = shift[:, None, :]
    ctx_rows = jnp.full((B,), B, jnp.int32)

    k_c, v_c, hq_c, hi_c, hff_c, hfb_c = _in_projection(
        ctx, s1p[ctx_rows], shift[ctx_rows], w_in, _CTX_GROUPS, tm=ctx.shape[1])
    q, k, v, ag, hq, hi, hff, hfb, hg = _in_projection(
        x, s1p[:B], shift[:B], w_in, _LATENT_GROUPS, tm=ROW_TILE, rope_tables=_rope_tables(N))

    att = _diff_attention(q, k_c, v_c, k, v, diff_lambda, diff_subln_gain)

    _, _, s_ctx = _hgrn2_bidir(hq_c, hi_c, hff_c, hfb_c, hgrn_lower_bound, None, n_sub=ctx.shape[1] // HGRN_CHUNK)
    o_f, o_b, _ = _hgrn2_bidir(hq, hi, hff, hfb, hgrn_lower_bound, s_ctx, n_sub=HGRN_CHUNKS_PER_STEP)

    return _merge(att, ag, o_f, o_b, hg, x, gate[:B, None, :], w_out[0].astype(jnp.bfloat16),
                  hgrn_norm_gain, ln_gain, ln_bias, tm=ROW_TILE)
```

```python
import functools
import math

import numpy as np
import jax
import jax.numpy as jnp
from jax import lax
from jax.experimental import pallas as pl
from jax.experimental.pallas import tpu as pltpu

DEPTH = 1
GRID_W = 64
HEADS = 4
HEAD_W = 128
MAP_D = 64
BRANCH_W = HEADS * HEAD_W
ROPE_BASE = 10000.0
ROPE_FREQS = MAP_D // 4
EPS = 1e-5
LAMBDA_INIT = 0.8 - 0.6 * math.exp(-0.3 * 0)
ALPHA = (2.0 * DEPTH) ** 0.25

V7X_VMEM_LIMIT_BYTES = 56 * 1024 * 1024
SUBLANES = 8
LANES = 128
ROW_TILE = 512
ATTN_TQ = 1024
ATTN_MAX_TK = 1408
HGRN_CHUNKS_PER_STEP = 4
HGRN_CHUNK = 128
HGRN_LEVELS = 7

_NT = (((1,), (1,)), ((), ()))
_TN = (((0,), (0,)), ((), ()))


def _silu(x):
    return x * jax.nn.sigmoid(x)


def _mod_kernel(c_ref, w_ref, b_ref, o_ref):
    a = _silu(c_ref[...]).astype(jnp.bfloat16)
    o_ref[...] = jnp.dot(a, w_ref[0].astype(jnp.bfloat16), preferred_element_type=jnp.float32) + b_ref[...]


def _modulation(cvec, w_ada, b_ada):
    rows, d = cvec.shape
    n_out = w_ada.shape[2]
    bn = 1024
    return pl.pallas_call(
        _mod_kernel,
        grid=(n_out // bn,),
        in_specs=[pl.BlockSpec((rows, d), lambda j: (0, 0)),
                  pl.BlockSpec((1, d, bn), lambda j: (0, 0, j)),
                  pl.BlockSpec((1, bn), lambda j: (0, j))],
        out_specs=pl.BlockSpec((rows, bn), lambda j: (0, j)),
        out_shape=jax.ShapeDtypeStruct((rows, n_out), jnp.float32),
        compiler_params=pltpu.CompilerParams(vmem_limit_bytes=V7X_VMEM_LIMIT_BYTES),
        name="modulation",
    )(cvec, w_ada, b_ada)


_LATENT_GROUPS = ((0, "rope_q", jnp.bfloat16), (1, "rope", jnp.bfloat16), (2, "transposed", jnp.bfloat16),
                  (3, "plain", jnp.bfloat16), (4, "silu", jnp.float32), (5, "plain", jnp.bfloat16),
                  (6, "plain", jnp.float32), (7, "plain", jnp.float32), (8, "plain", jnp.bfloat16))
_CTX_GROUPS = ((1, "plain", jnp.bfloat16), (2, "transposed", jnp.bfloat16), (4, "silu", jnp.float32),
               (5, "plain", jnp.bfloat16), (6, "plain", jnp.float32), (7, "plain", jnp.float32))
Q_SCALE = math.log2(math.e) / math.sqrt(MAP_D)


def _proj_kernel(groups, use_rope, *refs):
    if use_rope:
        x_ref, s1p_ref, sh_ref, w_ref, cos_row_ref, sin_row_ref, cos_col_ref, sin_col_ref = refs[:8]
        out_refs = refs[8:]
    else:
        x_ref, s1p_ref, sh_ref, w_ref = refs[:4]
        out_refs = refs[4:]
    h = (x_ref[0] * s1p_ref[0] + sh_ref[0]).astype(jnp.bfloat16)
    if use_rope:
        by_row = (lax.broadcasted_iota(jnp.int32, (GRID_W, HEAD_W), 1) & (MAP_D - 1)) < 2 * ROPE_FREQS

        def token_table(row_ref, col_ref):
            grid_rows = [jnp.where(by_row, row_ref[r:r + 1, :], col_ref[...]) for r in range(row_ref.shape[0])]
            return jnp.concatenate([jnp.concatenate(grid_rows, axis=0)] * HEADS, axis=1)

        cos = token_table(cos_row_ref, cos_col_ref)
        sin = token_table(sin_row_ref, sin_col_ref)
        lane = lax.broadcasted_iota(jnp.int32, cos.shape, 1)
        first_half = (lane & (2 * ROPE_FREQS - 1)) < ROPE_FREQS
    for (g, kind, dt), o_ref in zip(groups, out_refs):
        p = jnp.dot(h, w_ref[0, :, g * BRANCH_W:(g + 1) * BRANCH_W].astype(jnp.bfloat16),
                    preferred_element_type=jnp.float32)
        if kind in ("rope", "rope_q"):
            partner = jnp.where(first_half,
                                pltpu.roll(p, BRANCH_W - ROPE_FREQS, axis=1),
                                pltpu.roll(p, ROPE_FREQS, axis=1))
            p = p * cos + partner * sin
            if kind == "rope_q":
                p = p * Q_SCALE
        elif kind == "silu":
            p = _silu(p)
        elif kind == "transposed":
            p = p.T
        o_ref[0] = p.astype(dt)


def _in_projection(x, s1p, sh, w_in, groups, tm, rope_tables=None):
    B, T, D = x.shape
    use_rope = rope_tables is not None
    in_specs = [pl.BlockSpec((1, tm, D), lambda b, i: (b, i, 0)),
                pl.BlockSpec((1, 1, D), lambda b, i: (b, 0, 0)),
                pl.BlockSpec((1, 1, D), lambda b, i: (b, 0, 0)),
                pl.BlockSpec(w_in.shape, lambda b, i: (0, 0, 0), pipeline_mode=pl.Buffered(1))]
    args = [x, s1p, sh, w_in]
    if use_rope:
        assert tm % (GRID_W * SUBLANES) == 0
        in_specs += [pl.BlockSpec((tm // GRID_W, HEAD_W), lambda b, i: (i, 0))] * 2
        in_specs += [pl.BlockSpec((GRID_W, HEAD_W), lambda b, i: (0, 0))] * 2
        args += list(rope_tables)
    out_specs = [pl.BlockSpec((1, BRANCH_W, tm), lambda b, i: (b, 0, i)) if kind == "transposed"
                 else pl.BlockSpec((1, tm, BRANCH_W), lambda b, i: (b, i, 0)) for _, kind, _ in groups]
    out_shape = [jax.ShapeDtypeStruct((B, BRANCH_W, T) if kind == "transposed" else (B, T, BRANCH_W), dt)
                 for _, kind, dt in groups]
    return pl.pallas_call(
        functools.partial(_proj_kernel, groups, use_rope),
        grid=(B, T // tm),
        in_specs=in_specs, out_specs=out_specs, out_shape=out_shape,
        compiler_params=pltpu.CompilerParams(
            dimension_semantics=("parallel", "parallel"), vmem_limit_bytes=V7X_VMEM_LIMIT_BYTES),
        name="in_proj_rope" if use_rope else "in_proj_ctx",
    )(*args)


def _rope_tables(n_tokens):
    inv_freq = jnp.asarray(ROPE_BASE, jnp.float32) ** (-jnp.arange(ROPE_FREQS, dtype=jnp.float32) / ROPE_FREQS)
    freq = jnp.tile(inv_freq, HEAD_W // ROPE_FREQS)
    sign = np.where((np.arange(HEAD_W) % (2 * ROPE_FREQS)) < ROPE_FREQS, -1.0, 1.0).astype(np.float32)
    row_ang = jnp.arange(n_tokens // GRID_W, dtype=jnp.float32)[:, None] * freq
    col_ang = jnp.arange(GRID_W, dtype=jnp.float32)[:, None] * freq
    return jnp.cos(row_ang), jnp.sin(row_ang) * sign, jnp.cos(col_ang), jnp.sin(col_ang) * sign


def _attn_kernel(tk, q_ref, kc_ref, vtc_ref, kl_ref, vtl_ref, lam_ref, gain_ref, o_ref,
                 k_ref, vt_ref, s_ref, p_ref, acc_ref):
    t_ctx = kc_ref.shape[1]

    @pl.when(pl.program_id(2) == 0)
    def _():
        k_ref[:t_ctx, :] = kc_ref[0]
        k_ref[t_ctx:, :] = kl_ref[0]
        vt_ref[:, :t_ctx] = vtc_ref[0]
        vt_ref[:, t_ctx:] = vtl_ref[0]

    qt = q_ref[0].astype(jnp.float32).T
    row = lax.broadcasted_iota(jnp.int32, qt.shape, 0)
    qm = tuple(jnp.where(sel, qt, 0.0).astype(jnp.bfloat16) for sel in (row < MAP_D, row >= MAP_D))
    tq = qt.shape[1]
    n_chunks = k_ref.shape[0] // tk

    def scores(j):
        kb = k_ref[pl.ds(pl.multiple_of(j * tk, tk), tk), :]
        smax = []
        for m in range(2):
            s = jnp.dot(kb, qm[m], preferred_element_type=jnp.float32)
            s_ref[m] = s
            smax.append(jnp.max(s, axis=0, keepdims=True))
        return tuple(smax)

    def weights(smax, ml):
        new_ml, corrs = [], []
        for m in range(2):
            mx, l = ml[m]
            mx_new = jnp.maximum(mx, smax[m])
            corr = jnp.exp2(mx - mx_new)
            p = jnp.exp2(s_ref[m] - mx_new)
            p_ref[m] = p.astype(jnp.bfloat16)
            new_ml.append((mx_new, l * corr + jnp.sum(p, axis=0, keepdims=True)))
            corrs.append(corr)
        return tuple(new_ml), tuple(corrs)

    def values(j, corr):
        vtb = vt_ref[:, pl.ds(pl.multiple_of(j * tk, tk), tk)]
        for m in range(2):
            acc_ref[m] = acc_ref[m] * corr[m] + jnp.dot(vtb, p_ref[m], preferred_element_type=jnp.float32)

    ml = tuple((jnp.full((1, tq), -jnp.inf, jnp.float32), jnp.zeros((1, tq), jnp.float32)) for _ in range(2))
    acc_ref[...] = jnp.zeros_like(acc_ref)
    ml, corr = weights(scores(0), ml)
    smax = scores(1)

    def body(j, carry):
        ml, smax, corr_prev = carry
        values(j - 1, corr_prev)
        ml, corr = weights(smax, ml)
        return ml, scores(j + 1), corr

    ml, smax, corr = lax.fori_loop(1, n_chunks - 1, body, (ml, smax, corr))
    values(n_chunks - 2, corr)
    ml, corr = weights(smax, ml)
    values(n_chunks - 1, corr)

    lp = lam_ref[0]
    lam = (jnp.exp(jnp.sum(lp[0:1] * lp[1:2], axis=-1, keepdims=True))
           - jnp.exp(jnp.sum(lp[2:3] * lp[3:4], axis=-1, keepdims=True)) + LAMBDA_INIT)
    (_, l0), (_, l1) = ml
    ot = acc_ref[0] / l0 - lam * (acc_ref[1] / l1)
    ot = ot * lax.rsqrt(jnp.mean(ot * ot, axis=0, keepdims=True) + EPS)
    o_ref[0] = (ot.T * gain_ref[...] * (1.0 - LAMBDA_INIT)).astype(o_ref.dtype)


def _key_chunk(n_keys):
    return max(t for t in range(LANES, ATTN_MAX_TK + 1, LANES) if n_keys % t == 0)


def _diff_attention(q, k_ctx, vt_ctx, k, vt, diff_lambda, gain):
    B, N, _ = q.shape
    T_ctx = k_ctx.shape[1]
    Tk = T_ctx + N
    tq, tk = ATTN_TQ, _key_chunk(Tk)
    assert N % tq == 0 and Tk // tk >= 3 and T_ctx % LANES == 0
    return pl.pallas_call(
        functools.partial(_attn_kernel, tk),
        grid=(B, HEADS, N // tq),
        in_specs=[pl.BlockSpec((1, tq, HEAD_W), lambda b, h, i: (b, i, h)),
                  pl.BlockSpec((1, T_ctx, HEAD_W), lambda b, h, i: (b, 0, h)),
                  pl.BlockSpec((1, HEAD_W, T_ctx), lambda b, h, i: (b, h, 0)),
                  pl.BlockSpec((1, N, HEAD_W), lambda b, h, i: (b, 0, h)),
                  pl.BlockSpec((1, HEAD_W, N), lambda b, h, i: (b, h, 0)),
                  pl.BlockSpec((1, 4, MAP_D), lambda b, h, i: (0, 0, 0)),
                  pl.BlockSpec((1, HEAD_W), lambda b, h, i: (0, 0))],
        out_specs=pl.BlockSpec((1, tq, HEAD_W), lambda b, h, i: (b, i, h)),
        out_shape=jax.ShapeDtypeStruct((B, N, BRANCH_W), jnp.bfloat16),
        scratch_shapes=[pltpu.VMEM((Tk, HEAD_W), jnp.bfloat16), pltpu.VMEM((HEAD_W, Tk), jnp.bfloat16),
                        pltpu.VMEM((2, tk, tq), jnp.float32), pltpu.VMEM((2, tk, tq), jnp.bfloat16),
                        pltpu.VMEM((2, HEAD_W, tq), jnp.float32)],
        compiler_params=pltpu.CompilerParams(
            dimension_semantics=("parallel", "parallel", "arbitrary"),
            vmem_limit_bytes=V7X_VMEM_LIMIT_BYTES),
        name="diff_attention",
    )(q, k_ctx, vt_ctx, k, vt, diff_lambda, gain)


def _level_maps():
    t = np.arange(HGRN_CHUNK)[:, None]
    s = np.arange(HGRN_CHUNK)[None, :]
    x = t ^ s
    lvl = np.where(x > 0, np.floor(np.log2(np.maximum(x, 1))).astype(np.int32) + 1, 0)
    lvl = np.where(s > t, -1, lvl).astype(np.int32)
    return np.stack([lvl, lvl.T])


def _shift_down(x, s):
    return pltpu.roll(x, s, axis=1)


def _shift_up(x, s):
    return pltpu.roll(x, SUBLANES - s, axis=1)


def _neg_abs(x):
    bits = lax.bitcast_convert_type(x, jnp.uint32) | jnp.uint32(0x80000000)
    return lax.bitcast_convert_type(bits, jnp.float32)


def _hgrn_chain(q, z, v_bf, lb, st, masks, reverse):
    C = HGRN_CHUNK
    groups = C // SUBLANES
    sig = jax.nn.sigmoid(z)
    g = jnp.log2(lb + (1.0 - lb) * sig)
    kk = (1.0 - lb) * (1.0 - sig)
    grouped = (groups, SUBLANES, HEAD_W)
    sub = lax.broadcasted_iota(jnp.int32, grouped, 1)

    bg = g.reshape(grouped)
    for s in (1, 2, 4):
        if reverse:
            bg = bg + jnp.where(sub < SUBLANES - s, _shift_up(bg, s), 0.0)
        else:
            bg = bg + jnp.where(sub >= s, _shift_down(bg, s), 0.0)
    parts = [bg[r] for r in range(groups)]
    order = range(groups - 2, -1, -1) if reverse else range(1, groups)
    for r in order:
        prev = parts[r + 1][0:1, :] if reverse else parts[r - 1][SUBLANES - 1:SUBLANES, :]
        parts[r] = parts[r] + prev
    b = jnp.concatenate(parts, axis=0)
    bg = b.reshape(grouped)

    q_bf = q.astype(jnp.bfloat16)
    kk_bf = kk.astype(jnp.bfloat16)

    def rows(x, r):
        return x[r * SUBLANES:(r + 1) * SUBLANES, :]

    a_parts = [None] * groups

    own = bg
    nbr = _shift_up(bg, 1) if reverse else _shift_down(bg, 1)
    for j in range(1, HGRN_LEVELS + 1):
        h = 1 << (j - 1)
        if 2 * h <= SUBLANES:
            upper = (sub & h) != 0
            ref = jnp.where(upper, own, nbr) if reverse else jnp.where(upper, nbr, own)
            w = jnp.exp2(_neg_abs(bg - ref)).reshape(C, HEAD_W).astype(jnp.bfloat16)
            if 4 * h <= SUBLANES:
                if reverse:
                    own = jnp.where(upper, _shift_down(own, h), own)
                    nbr = jnp.where(upper, nbr, _shift_up(nbr, h))
                else:
                    own = jnp.where(upper, own, _shift_up(own, h))
                    nbr = jnp.where(upper, _shift_down(nbr, h), nbr)
            pm = lax.dot_general(q_bf * w, kk_bf * w, _NT, preferred_element_type=jnp.float32)
            a_parts = [jnp.where(rows(masks[j], r), rows(pm, r), 0.0 if a_parts[r] is None else a_parts[r])
                       for r in range(groups)]
        else:
            n_blk, half_groups = C // (2 * h), h // SUBLANES
            blocked = (n_blk, 2 * h, HEAD_W)
            b_blk = b.reshape(blocked)
            ref = b_blk[:, h:h + 1, :] if reverse else b_blk[:, h - 1:h, :]
            w_blk = jnp.exp2(_neg_abs(b_blk - ref))
            q_half, k_half = (slice(0, h), slice(h, 2 * h)) if reverse else (slice(h, 2 * h), slice(0, h))
            q_t = (q.reshape(blocked)[:, q_half] * w_blk[:, q_half]).reshape(C // 2, HEAD_W)
            k_t = kk.reshape(blocked)[:, k_half] * w_blk[:, k_half]
            zeros = jnp.zeros_like(k_t)
            k_t = jnp.concatenate([zeros, k_t] if reverse else [k_t, zeros], axis=1).reshape(C, HEAD_W)
            pm = lax.dot_general(q_t.astype(jnp.bfloat16), k_t.astype(jnp.bfloat16), _NT,
                                 preferred_element_type=jnp.float32)
            for blk in range(n_blk):
                for i in range(half_groups):
                    r = blk * 2 * half_groups + (0 if reverse else half_groups) + i
                    piece = rows(pm, blk * half_groups + i)
                    if n_blk == 1:
                        a_parts[r] = a_parts[r] + piece
                    else:
                        a_parts[r] = jnp.where(rows(masks[j], r), piece, a_parts[r])
    a = jnp.concatenate(a_parts, axis=0)

    o = jnp.dot(a.astype(jnp.bfloat16), v_bf, preferred_element_type=jnp.float32)
    o = o + lax.dot_general(q_bf * jnp.exp2(b).astype(jnp.bfloat16), st.astype(jnp.bfloat16), _NT,
                            preferred_element_type=jnp.float32)
    o = o + jnp.sum(q * kk, axis=-1, keepdims=True) * v_bf.astype(jnp.float32)
    b_tot = b[0:1, :] if reverse else b[C - 1:C, :]
    k_hat = kk_bf * jnp.exp2(b_tot - b).astype(jnp.bfloat16)
    st_new = st * jnp.exp2(b_tot) + lax.dot_general(v_bf, k_hat, _TN, preferred_element_type=jnp.float32)
    return o, st_new


def _hgrn_kernel(n_sub, has_s0, qf_ref, vf_ref, zf_ref, qb_ref, vb_ref, zb_ref, lbp_ref, lvl_ref, *refs):
    s0_ref, (of_ref, ob_ref, sfin_ref, st_ref) = (refs[0], refs[1:]) if has_s0 else (None, refs)
    i = pl.program_id(1)
    C = HGRN_CHUNK

    @pl.when(i == 0)
    def _():
        st_ref[...] = jnp.zeros_like(st_ref) if s0_ref is None else s0_ref[0]

    dirs = ((qf_ref, vf_ref, zf_ref, of_ref), (qb_ref, vb_ref, zb_ref, ob_ref))
    masks, lbs = [], []
    for d in range(2):
        lvl = lvl_ref[d]
        masks.append([lvl == j for j in range(HGRN_LEVELS + 1)])
        p0, p1 = lbp_ref[d, 0:1, :], lbp_ref[d, 1:2, :]
        pm = jnp.maximum(p0, p1)
        e0 = jnp.exp(p0 - pm)
        lbs.append(e0 / (e0 + jnp.exp(p1 - pm)))
    states = [[st_ref[d, hh] for hh in range(HEADS)] for d in range(2)]
    for sub in range(n_sub):
        for d, (q_ref, v_ref, z_ref, o_ref) in enumerate(dirs):
            c = n_sub - 1 - sub if d == 1 else sub
            tok = slice(c * C, (c + 1) * C)
            for hh in range(HEADS):
                cols = slice(hh * HEAD_W, (hh + 1) * HEAD_W)
                o, states[d][hh] = _hgrn_chain(q_ref[0, tok, cols], z_ref[0, tok, cols], v_ref[0, tok, cols],
                                               lbs[d][:, cols], states[d][hh], masks[d], reverse=(d == 1))
                o_ref[0, tok, cols] = o.astype(o_ref.dtype)
    for d in range(2):
        for hh in range(HEADS):
            st_ref[d, hh] = states[d][hh]

    @pl.when(i == pl.num_programs(1) - 1)
    def _():
        sfin_ref[0] = st_ref[...]


def _hgrn2_bidir(hq, hi, hff, hfb, lb_param, s0, n_sub):
    B, T, _ = hq.shape
    rows = HGRN_CHUNK * n_sub
    n = T // rows
    assert n * rows == T
    fwd = pl.BlockSpec((1, rows, BRANCH_W), lambda b, i: (b, i, 0))
    bwd = pl.BlockSpec((1, rows, BRANCH_W), lambda b, i: (b, n - 1 - i, 0))
    st_spec = pl.BlockSpec((1, 2, HEADS, HEAD_W, HEAD_W), lambda b, i: (b, 0, 0, 0, 0))
    lvl = jnp.asarray(_level_maps())
    return pl.pallas_call(
        functools.partial(_hgrn_kernel, n_sub, s0 is not None),
        grid=(B, n),
        in_specs=[fwd, fwd, fwd, bwd, bwd, bwd,
                  pl.BlockSpec(lb_param.shape, lambda b, i: (0, 0, 0)),
                  pl.BlockSpec(lvl.shape, lambda b, i: (0, 0, 0))] + ([] if s0 is None else [st_spec]),
        out_specs=[fwd, bwd, st_spec],
        out_shape=[jax.ShapeDtypeStruct((B, T, BRANCH_W), jnp.bfloat16),
                   jax.ShapeDtypeStruct((B, T, BRANCH_W), jnp.bfloat16),
                   jax.ShapeDtypeStruct((B, 2, HEADS, HEAD_W, HEAD_W), jnp.float32)],
        scratch_shapes=[pltpu.VMEM((2, HEADS, HEAD_W, HEAD_W), jnp.float32)],
        compiler_params=pltpu.CompilerParams(
            dimension_semantics=("parallel", "arbitrary"), vmem_limit_bytes=V7X_VMEM_LIMIT_BYTES),
        name="hgrn2_bidir",
    )(hq, hi, hff, hq, hi, hfb, lb_param, lvl, *([] if s0 is None else [s0]))


def _merge_kernel(att_ref, ag_ref, of_ref, ob_ref, hg_ref, x_ref, gate_ref, w_ref, hgain_ref, lng_ref, lnb_ref,
                  o_ref):
    f32 = jnp.float32
    att = att_ref[0].astype(f32) * _silu(ag_ref[0].astype(f32))
    o = of_ref[0].astype(f32) + ob_ref[0].astype(f32)
    parts = []
    for hh in range(HEADS):
        oh = o[:, hh * HEAD_W:(hh + 1) * HEAD_W]
        parts.append(oh * lax.rsqrt(jnp.mean(oh * oh, axis=-1, keepdims=True) + EPS) * hgain_ref[...])
    hg = jnp.concatenate(parts, axis=1) * _silu(hg_ref[0].astype(f32))
    y_in = jnp.concatenate([att, hg], axis=1).astype(jnp.bfloat16)
    y = jnp.dot(y_in, w_ref[...], preferred_element_type=jnp.float32)
    u = ALPHA * x_ref[0] + gate_ref[0] * y
    mu = jnp.mean(u, axis=-1, keepdims=True)
    uc = u - mu
    var = jnp.mean(uc * uc, axis=-1, keepdims=True)
    o_ref[0] = uc * lax.rsqrt(var + EPS) * lng_ref[...] + lnb_ref[...]


def _merge(att, ag, o_f, o_b, hg, x, gate, w_out_bf, hgain, ln_g, ln_b, tm):
    B, N, D = x.shape
    half = pl.BlockSpec((1, tm, BRANCH_W), lambda b, i: (b, i, 0))
    full = pl.BlockSpec((1, tm, D), lambda b, i: (b, i, 0))
    row = lambda w: pl.BlockSpec((1, w), lambda b, i: (0, 0))
    return pl.pallas_call(
        _merge_kernel,
        grid=(B, N // tm),
        in_specs=[half, half, half, half, half, full,
                  pl.BlockSpec((1, 1, D), lambda b, i: (b, 0, 0)),
                  pl.BlockSpec(w_out_bf.shape, lambda b, i: (0, 0)),
                  row(HEAD_W), row(D), row(D)],
        out_specs=full,
        out_shape=jax.ShapeDtypeStruct((B, N, D), jnp.float32),
        compiler_params=pltpu.CompilerParams(
            dimension_semantics=("parallel", "parallel"), vmem_limit_bytes=V7X_VMEM_LIMIT_BYTES),
        name="merge_out_proj_ln",
    )(att, ag, o_f, o_b, hg, x, gate, w_out_bf, hgain, ln_g, ln_b)


def kernel(x, c, ctx, c_ctx, w_ada, b_ada, w_in, w_out, diff_lambda, diff_subln_gain, hgrn_lower_bound,
           hgrn_norm_gain, ln_gain, ln_bias):
    B, N, D = x.shape
    assert DEPTH == 1 and w_ada.shape[0] == 1
    cvec = jnp.concatenate([c, c_ctx[None, :], jnp.zeros((SUBLANES - B - 1, D), c.dtype)], axis=0)
    mod = _modulation(cvec, w_ada, b_ada)
    shift, scale, gate = mod[:, :D], mod[:, D:2 * D], mod[:, 2 * D:]
    s1p = (1.0 + scale)[:, None, :]
    shift = shift[:, None, :]
    ctx_rows = jnp.full((B,), B, jnp.int32)

    k_c, v_c, hq_c, hi_c, hff_c, hfb_c = _in_projection(
        ctx, s1p[ctx_rows], shift[ctx_rows], w_in, _CTX_GROUPS, tm=ctx.shape[1])
    q, k, v, ag, hq, hi, hff, hfb, hg = _in_projection(
        x, s1p[:B], shift[:B], w_in, _LATENT_GROUPS, tm=ROW_TILE, rope_tables=_rope_tables(N))

    att = _diff_attention(q, k_c, v_c, k, v, diff_lambda, diff_subln_gain)

    _, _, s_ctx = _hgrn2_bidir(hq_c, hi_c, hff_c, hfb_c, hgrn_lower_bound, None, n_sub=ctx.shape[1] // HGRN_CHUNK)
    o_f, o_b, _ = _hgrn2_bidir(hq, hi, hff, hfb, hgrn_lower_bound, s_ctx, n_sub=HGRN_CHUNKS_PER_STEP)

    return _merge(att, ag, o_f, o_b, hg, x, gate[:B, None, :], w_out[0].astype(jnp.bfloat16),
                  hgrn_norm_gain, ln_gain, ln_bias, tm=ROW_TILE)
```

```python
import functools
import math

import numpy as np
import jax
import jax.numpy as jnp
from jax import lax
from jax.experimental import pallas as pl
from jax.experimental.pallas import tpu as pltpu

DEPTH = 1
GRID_W = 64
HEADS = 4
HEAD_W = 128
MAP_D = 64
BRANCH_W = HEADS * HEAD_W
ROPE_BASE = 10000.0
ROPE_FREQS = MAP_D // 4
EPS = 1e-5
LAMBDA_INIT = 0.8 - 0.6 * math.exp(-0.3 * 0)
ALPHA = (2.0 * DEPTH) ** 0.25

V7X_VMEM_LIMIT_BYTES = 56 * 1024 * 1024
SUBLANES = 8
LANES = 128
ROW_TILE = 512
ATTN_TQ = 2048
ATTN_MAX_TK = 1408
HGRN_CHUNKS_PER_STEP = 4
HGRN_CHUNK = 128
HGRN_LEVELS = 7

_NT = (((1,), (1,)), ((), ()))
_TN = (((0,), (0,)), ((), ()))


def _silu(x):
    return x * jax.nn.sigmoid(x)


def _mod_kernel(c_ref, w_ref, b_ref, o_ref):
    a = _silu(c_ref[...]).astype(jnp.bfloat16)
    o_ref[...] = jnp.dot(a, w_ref[0].astype(jnp.bfloat16), preferred_element_type=jnp.float32) + b_ref[...]


def _modulation(cvec, w_ada, b_ada):
    rows, d = cvec.shape
    n_out = w_ada.shape[2]
    bn = 1024
    return pl.pallas_call(
        _mod_kernel,
        grid=(n_out // bn,),
        in_specs=[pl.BlockSpec((rows, d), lambda j: (0, 0)),
                  pl.BlockSpec((1, d, bn), lambda j: (0, 0, j)),
                  pl.BlockSpec((1, bn), lambda j: (0, j))],
        out_specs=pl.BlockSpec((rows, bn), lambda j: (0, j)),
        out_shape=jax.ShapeDtypeStruct((rows, n_out), jnp.float32),
        compiler_params=pltpu.CompilerParams(vmem_limit_bytes=V7X_VMEM_LIMIT_BYTES),
        name="modulation",
    )(cvec, w_ada, b_ada)


_LATENT_GROUPS = ((0, "rope_q", jnp.bfloat16), (1, "rope", jnp.bfloat16), (2, "transposed", jnp.bfloat16),
                  (3, "plain", jnp.bfloat16), (4, "silu", jnp.float32), (5, "plain", jnp.bfloat16),
                  (6, "plain", jnp.float32), (7, "plain", jnp.float32), (8, "plain", jnp.bfloat16))
_CTX_GROUPS = ((1, "plain", jnp.bfloat16), (2, "transposed", jnp.bfloat16), (4, "silu", jnp.float32),
               (5, "plain", jnp.bfloat16), (6, "plain", jnp.float32), (7, "plain", jnp.float32))
Q_SCALE = math.log2(math.e) / math.sqrt(MAP_D)


def _proj_kernel(groups, use_rope, *refs):
    if use_rope:
        x_ref, s1p_ref, sh_ref, w_ref, cos_row_ref, sin_row_ref, cos_col_ref, sin_col_ref = refs[:8]
        out_refs = refs[8:]
    else:
        x_ref, s1p_ref, sh_ref, w_ref = refs[:4]
        out_refs = refs[4:]
    h = (x_ref[0] * s1p_ref[0] + sh_ref[0]).astype(jnp.bfloat16)
    if use_rope:
        by_row = (lax.broadcasted_iota(jnp.int32, (GRID_W, HEAD_W), 1) & (MAP_D - 1)) < 2 * ROPE_FREQS

        def token_table(row_ref, col_ref):
            grid_rows = [jnp.where(by_row, row_ref[r:r + 1, :], col_ref[...]) for r in range(row_ref.shape[0])]
            return jnp.concatenate([jnp.concatenate(grid_rows, axis=0)] * HEADS, axis=1)

        cos = token_table(cos_row_ref, cos_col_ref)
        sin = token_table(sin_row_ref, sin_col_ref)
        lane = lax.broadcasted_iota(jnp.int32, cos.shape, 1)
        first_half = (lane & (2 * ROPE_FREQS - 1)) < ROPE_FREQS
    for (g, kind, dt), o_ref in zip(groups, out_refs):
        p = jnp.dot(h, w_ref[0, :, g * BRANCH_W:(g + 1) * BRANCH_W].astype(jnp.bfloat16),
                    preferred_element_type=jnp.float32)
        if kind in ("rope", "rope_q"):
            partner = jnp.where(first_half,
                                pltpu.roll(p, BRANCH_W - ROPE_FREQS, axis=1),
                                pltpu.roll(p, ROPE_FREQS, axis=1))
            p = p * cos + partner * sin
            if kind == "rope_q":
                p = p * Q_SCALE
        elif kind == "silu":
            p = _silu(p)
        elif kind == "transposed":
            p = p.T
        o_ref[0] = p.astype(dt)


def _in_projection(x, s1p, sh, w_in, groups, tm, rope_tables=None):
    B, T, D = x.shape
    use_rope = rope_tables is not None
    in_specs = [pl.BlockSpec((1, tm, D), lambda b, i: (b, i, 0)),
                pl.BlockSpec((1, 1, D), lambda b, i: (b, 0, 0)),
                pl.BlockSpec((1, 1, D), lambda b, i: (b, 0, 0)),
                pl.BlockSpec(w_in.shape, lambda b, i: (0, 0, 0), pipeline_mode=pl.Buffered(1))]
    args = [x, s1p, sh, w_in]
    if use_rope:
        assert tm % (GRID_W * SUBLANES) == 0
        in_specs += [pl.BlockSpec((tm // GRID_W, HEAD_W), lambda b, i: (i, 0))] * 2
        in_specs += [pl.BlockSpec((GRID_W, HEAD_W), lambda b, i: (0, 0))] * 2
        args += list(rope_tables)
    out_specs = [pl.BlockSpec((1, BRANCH_W, tm), lambda b, i: (b, 0, i)) if kind == "transposed"
                 else pl.BlockSpec((1, tm, BRANCH_W), lambda b, i: (b, i, 0)) for _, kind, _ in groups]
    out_shape = [jax.ShapeDtypeStruct((B, BRANCH_W, T) if kind == "transposed" else (B, T, BRANCH_W), dt)
                 for _, kind, dt in groups]
    return pl.pallas_call(
        functools.partial(_proj_kernel, groups, use_rope),
        grid=(B, T // tm),
        in_specs=in_specs, out_specs=out_specs, out_shape=out_shape,
        compiler_params=pltpu.CompilerParams(
            dimension_semantics=("parallel", "parallel"), vmem_limit_bytes=V7X_VMEM_LIMIT_BYTES),
        name="in_proj_rope" if use_rope else "in_proj_ctx",
    )(*args)


def _rope_tables(n_tokens):
    inv_freq = jnp.asarray(ROPE_BASE, jnp.float32) ** (-jnp.arange(ROPE_FREQS, dtype=jnp.float32) / ROPE_FREQS)
    freq = jnp.tile(inv_freq, HEAD_W // ROPE_FREQS)
    sign = np.where((np.arange(HEAD_W) % (2 * ROPE_FREQS)) < ROPE_FREQS, -1.0, 1.0).astype(np.float32)
    row_ang = jnp.arange(n_tokens // GRID_W, dtype=jnp.float32)[:, None] * freq
    col_ang = jnp.arange(GRID_W, dtype=jnp.float32)[:, None] * freq
    return jnp.cos(row_ang), jnp.sin(row_ang) * sign, jnp.cos(col_ang), jnp.sin(col_ang) * sign


def _attn_kernel(tk, q_ref, kc_ref, vtc_ref, kl_ref, vtl_ref, lam_ref, gain_ref, o_ref,
                 k_ref, vt_ref, s_ref, p_ref, acc_ref):
    t_ctx = kc_ref.shape[1]

    @pl.when(pl.program_id(2) == 0)
    def _():
        k_ref[:t_ctx, :] = kc_ref[0]
        k_ref[t_ctx:, :] = kl_ref[0]
        vt_ref[:, :t_ctx] = vtc_ref[0]
        vt_ref[:, t_ctx:] = vtl_ref[0]

    qt = q_ref[0].astype(jnp.float32).T
    row = lax.broadcasted_iota(jnp.int32, qt.shape, 0)
    qm = tuple(jnp.where(sel, qt, 0.0).astype(jnp.bfloat16) for sel in (row < MAP_D, row >= MAP_D))
    tq = qt.shape[1]
    n_chunks = k_ref.shape[0] // tk

    def scores(j):
        kb = k_ref[pl.ds(pl.multiple_of(j * tk, tk), tk), :]
        smax = []
        for m in range(2):
            s = jnp.dot(kb, qm[m], preferred_element_type=jnp.float32)
            s_ref[m] = s
            smax.append(jnp.max(s, axis=0, keepdims=True))
        return tuple(smax)

    def weights(smax, ml):
        new_ml, corrs = [], []
        for m in range(2):
            mx, l = ml[m]
            mx_new = jnp.maximum(mx, smax[m])
            corr = jnp.exp2(mx - mx_new)
            p = jnp.exp2(s_ref[m] - mx_new)
            p_ref[m] = p.astype(jnp.bfloat16)
            new_ml.append((mx_new, l * corr + jnp.sum(p, axis=0, keepdims=True)))
            corrs.append(corr)
        return tuple(new_ml), tuple(corrs)

    def values(j, corr):
        vtb = vt_ref[:, pl.ds(pl.multiple_of(j * tk, tk), tk)]
        for m in range(2):
            acc_ref[m] = acc_ref[m] * corr[m] + jnp.dot(vtb, p_ref[m], preferred_element_type=jnp.float32)

    ml = tuple((jnp.full((1, tq), -jnp.inf, jnp.float32), jnp.zeros((1, tq), jnp.float32)) for _ in range(2))
    acc_ref[...] = jnp.zeros_like(acc_ref)
    ml, corr = weights(scores(0), ml)
    smax = scores(1)

    def body(j, carry):
        ml, smax, corr_prev = carry
        values(j - 1, corr_prev)
        ml, corr = weights(smax, ml)
        return ml, scores(j + 1), corr

    ml, smax, corr = lax.fori_loop(1, n_chunks - 1, body, (ml, smax, corr))
    values(n_chunks - 2, corr)
    ml, corr = weights(smax, ml)
    values(n_chunks - 1, corr)

    lp = lam_ref[0]
    lam = (jnp.exp(jnp.sum(lp[0:1] * lp[1:2], axis=-1, keepdims=True))
           - jnp.exp(jnp.sum(lp[2:3] * lp[3:4], axis=-1, keepdims=True)) + LAMBDA_INIT)
    (_, l0), (_, l1) = ml
    ot = acc_ref[0] / l0 - lam * (acc_ref[1] / l1)
    ot = ot * lax.rsqrt(jnp.mean(ot * ot, axis=0, keepdims=True) + EPS)
    o_ref[0] = (ot.T * gain_ref[...] * (1.0 - LAMBDA_INIT)).astype(o_ref.dtype)


def _key_chunk(n_keys):
    return max(t for t in range(LANES, ATTN_MAX_TK + 1, LANES) if n_keys % t == 0)


def _diff_attention(q, k_ctx, vt_ctx, k, vt, diff_lambda, gain):
    B, N, _ = q.shape
    T_ctx = k_ctx.shape[1]
    Tk = T_ctx + N
    tq, tk = ATTN_TQ, _key_chunk(Tk)
    assert N % tq == 0 and Tk // tk >= 3 and T_ctx % LANES == 0
    return pl.pallas_call(
        functools.partial(_attn_kernel, tk),
        grid=(B, HEADS, N // tq),
        in_specs=[pl.BlockSpec((1, tq, HEAD_W), lambda b, h, i: (b, i, h)),
                  pl.BlockSpec((1, T_ctx, HEAD_W), lambda b, h, i: (b, 0, h)),
                  pl.BlockSpec((1, HEAD_W, T_ctx), lambda b, h, i: (b, h, 0)),
                  pl.BlockSpec((1, N, HEAD_W), lambda b, h, i: (b, 0, h)),
                  pl.BlockSpec((1, HEAD_W, N), lambda b, h, i: (b, h, 0)),
                  pl.BlockSpec((1, 4, MAP_D), lambda b, h, i: (0, 0, 0)),
                  pl.BlockSpec((1, HEAD_W), lambda b, h, i: (0, 0))],
        out_specs=pl.BlockSpec((1, tq, HEAD_W), lambda b, h, i: (b, i, h)),
        out_shape=jax.ShapeDtypeStruct((B, N, BRANCH_W), jnp.bfloat16),
        scratch_shapes=[pltpu.VMEM((Tk, HEAD_W), jnp.bfloat16), pltpu.VMEM((HEAD_W, Tk), jnp.bfloat16),
                        pltpu.VMEM((2, tk, tq), jnp.float32), pltpu.VMEM((2, tk, tq), jnp.bfloat16),
                        pltpu.VMEM((2, HEAD_W, tq), jnp.float32)],
        compiler_params=pltpu.CompilerParams(
            dimension_semantics=("parallel", "parallel", "arbitrary"),
            vmem_limit_bytes=V7X_VMEM_LIMIT_BYTES),
        name="diff_attention",
    )(q, k_ctx, vt_ctx, k, vt, diff_lambda, gain)


def _level_maps():
    t = np.arange(HGRN_CHUNK)[:, None]
    s = np.arange(HGRN_CHUNK)[None, :]
    x = t ^ s
    lvl = np.where(x > 0, np.floor(np.log2(np.maximum(x, 1))).astype(np.int32) + 1, 0)
    lvl = np.where(s > t, -1, lvl).astype(np.int32)
    return np.stack([lvl, lvl.T])


def _shift_down(x, s):
    return pltpu.roll(x, s, axis=1)


def _shift_up(x, s):
    return pltpu.roll(x, SUBLANES - s, axis=1)


def _neg_abs(x):
    bits = lax.bitcast_convert_type(x, jnp.uint32) | jnp.uint32(0x80000000)
    return lax.bitcast_convert_type(bits, jnp.float32)


def _hgrn_chain(q, z, v_bf, lb, st, masks, reverse):
    C = HGRN_CHUNK
    groups = C // SUBLANES
    sig = jax.nn.sigmoid(z)
    g = jnp.log2(lb + (1.0 - lb) * sig)
    kk = (1.0 - lb) * (1.0 - sig)
    grouped = (groups, SUBLANES, HEAD_W)
    sub = lax.broadcasted_iota(jnp.int32, grouped, 1)

    bg = g.reshape(grouped)
    for s in (1, 2, 4):
        if reverse:
            bg = bg + jnp.where(sub < SUBLANES - s, _shift_up(bg, s), 0.0)
        else:
            bg = bg + jnp.where(sub >= s, _shift_down(bg, s), 0.0)
    parts = [bg[r] for r in range(groups)]
    order = range(groups - 2, -1, -1) if reverse else range(1, groups)
    for r in order:
        prev = parts[r + 1][0:1, :] if reverse else parts[r - 1][SUBLANES - 1:SUBLANES, :]
        parts[r] = parts[r] + prev
    b = jnp.concatenate(parts, axis=0)
    bg = b.reshape(grouped)

    q_bf = q.astype(jnp.bfloat16)
    kk_bf = kk.astype(jnp.bfloat16)

    def rows(x, r):
        return x[r * SUBLANES:(r + 1) * SUBLANES, :]

    a_parts = [None] * groups

    own = bg
    nbr = _shift_up(bg, 1) if reverse else _shift_down(bg, 1)
    for j in range(1, HGRN_LEVELS + 1):
        h = 1 << (j - 1)
        if 2 * h <= SUBLANES:
            upper = (sub & h) != 0
            ref = jnp.where(upper, own, nbr) if reverse else jnp.where(upper, nbr, own)
            w = jnp.exp2(_neg_abs(bg - ref)).reshape(C, HEAD_W).astype(jnp.bfloat16)
            if 4 * h <= SUBLANES:
                if reverse:
                    own = jnp.where(upper, _shift_down(own, h), own)
                    nbr = jnp.where(upper, nbr, _shift_up(nbr, h))
                else:
                    own = jnp.where(upper, own, _shift_up(own, h))
                    nbr = jnp.where(upper, _shift_down(nbr, h), nbr)
            pm = lax.dot_general(q_bf * w, kk_bf * w, _NT, preferred_element_type=jnp.float32)
            a_parts = [jnp.where(rows(masks[j], r), rows(pm, r), 0.0 if a_parts[r] is None else a_parts[r])
                       for r in range(groups)]
        else:
            n_blk, half_groups = C // (2 * h), h // SUBLANES
            blocked = (n_blk, 2 * h, HEAD_W)
            b_blk = b.reshape(blocked)
            ref = b_blk[:, h:h + 1, :] if reverse else b_blk[:, h - 1:h, :]
            w_blk = jnp.exp2(_neg_abs(b_blk - ref))
            q_half, k_half = (slice(0, h), slice(h, 2 * h)) if reverse else (slice(h, 2 * h), slice(0, h))
            q_t = (q.reshape(blocked)[:, q_half] * w_blk[:, q_half]).reshape(C // 2, HEAD_W)
            k_t = kk.reshape(blocked)[:, k_half] * w_blk[:, k_half]
            zeros = jnp.zeros_like(k_t)
            k_t = jnp.concatenate([zeros, k_t] if reverse else [k_t, zeros], axis=1).reshape(C, HEAD_W)
            pm = lax.dot_general(q_t.astype(jnp.bfloat16), k_t.astype(jnp.bfloat16), _NT,
                                 preferred_element_type=jnp.float32)
            for blk in range(n_blk):
                for i in range(half_groups):
                    r = blk * 2 * half_groups + (0 if reverse else half_groups) + i
                    piece = rows(pm, blk * half_groups + i)
                    if n_blk == 1:
                        a_parts[r] = a_parts[r] + piece
                    else:
                        a_parts[r] = jnp.where(rows(masks[j], r), piece, a_parts[r])
    a = jnp.concatenate(a_parts, axis=0)

    o = jnp.dot(a.astype(jnp.bfloat16), v_bf, preferred_element_type=jnp.float32)
    o = o + lax.dot_general(q_bf * jnp.exp2(b).astype(jnp.bfloat16), st.astype(jnp.bfloat16), _NT,
                            preferred_element_type=jnp.float32)
    o = o + jnp.sum(q * kk, axis=-1, keepdims=True) * v_bf.astype(jnp.float32)
    b_tot = b[0:1, :] if reverse else b[C - 1:C, :]
    k_hat = kk_bf * jnp.exp2(b_tot - b).astype(jnp.bfloat16)
    st_new = st * jnp.exp2(b_tot) + lax.dot_general(v_bf, k_hat, _TN, preferred_element_type=jnp.float32)
    return o, st_new


def _hgrn_kernel(n_sub, has_s0, qf_ref, vf_ref, zf_ref, qb_ref, vb_ref, zb_ref, lbp_ref, lvl_ref, *refs):
    s0_ref, (of_ref, ob_ref, sfin_ref, st_ref) = (refs[0], refs[1:]) if has_s0 else (None, refs)
    i = pl.program_id(1)
    C = HGRN_CHUNK

    @pl.when(i == 0)
    def _():
        st_ref[...] = jnp.zeros_like(st_ref) if s0_ref is None else s0_ref[0]

    dirs = ((qf_ref, vf_ref, zf_ref, of_ref), (qb_ref, vb_ref, zb_ref, ob_ref))
    masks, lbs = [], []
    for d in range(2):
        lvl = lvl_ref[d]
        masks.append([lvl == j for j in range(HGRN_LEVELS + 1)])
        p0, p1 = lbp_ref[d, 0:1, :], lbp_ref[d, 1:2, :]
        pm = jnp.maximum(p0, p1)
        e0 = jnp.exp(p0 - pm)
        lbs.append(e0 / (e0 + jnp.exp(p1 - pm)))
    states = [[st_ref[d, hh] for hh in range(HEADS)] for d in range(2)]
    for sub in range(n_sub):
        for d, (q_ref, v_ref, z_ref, o_ref) in enumerate(dirs):
            c = n_sub - 1 - sub if d == 1 else sub
            tok = slice(c * C, (c + 1) * C)
            for hh in range(HEADS):
                cols = slice(hh * HEAD_W, (hh + 1) * HEAD_W)
                o, states[d][hh] = _hgrn_chain(q_ref[0, tok, cols], z_ref[0, tok, cols], v_ref[0, tok, cols],
                                               lbs[d][:, cols], states[d][hh], masks[d], reverse=(d == 1))
                o_ref[0, tok, cols] = o.astype(o_ref.dtype)
    for d in range(2):
        for hh in range(HEADS):
            st_ref[d, hh] = states[d][hh]

    @pl.when(i == pl.num_programs(1) - 1)
    def _():
        sfin_ref[0] = st_ref[...]


def _hgrn2_bidir(hq, hi, hff, hfb, lb_param, s0, n_sub):
    B, T, _ = hq.shape
    rows = HGRN_CHUNK * n_sub
    n = T // rows
    assert n * rows == T
    fwd = pl.BlockSpec((1, rows, BRANCH_W), lambda b, i: (b, i, 0))
    bwd = pl.BlockSpec((1, rows, BRANCH_W), lambda b, i: (b, n - 1 - i, 0))
    st_spec = pl.BlockSpec((1, 2, HEADS, HEAD_W, HEAD_W), lambda b, i: (b, 0, 0, 0, 0))
    lvl = jnp.asarray(_level_maps())
    return pl.pallas_call(
        functools.partial(_hgrn_kernel, n_sub, s0 is not None),
        grid=(B, n),
        in_specs=[fwd, fwd, fwd, bwd, bwd, bwd,
                  pl.BlockSpec(lb_param.shape, lambda b, i: (0, 0, 0)),
                  pl.BlockSpec(lvl.shape, lambda b, i: (0, 0, 0))] + ([] if s0 is None else [st_spec]),
        out_specs=[fwd, bwd, st_spec],
        out_shape=[jax.ShapeDtypeStruct((B, T, BRANCH_W), jnp.bfloat16),
                   jax.ShapeDtypeStruct((B, T, BRANCH_W), jnp.bfloat16),
                   jax.ShapeDtypeStruct((B, 2, HEADS, HEAD_W, HEAD_W), jnp.float32)],
        scratch_shapes=[pltpu.VMEM((2, HEADS, HEAD_W, HEAD_W), jnp.float32)],
        compiler_params=pltpu.CompilerParams(
            dimension_semantics=("parallel", "arbitrary"), vmem_limit_bytes=V7X_VMEM_LIMIT_BYTES),
        name="hgrn2_bidir",
    )(hq, hi, hff, hq, hi, hfb, lb_param, lvl, *([] if s0 is None else [s0]))


def _merge_kernel(att_ref, ag_ref, of_ref, ob_ref, hg_ref, x_ref, gate_ref, w_ref, hgain_ref, lng_ref, lnb_ref,
                  o_ref):
    f32 = jnp.float32
    att = att_ref[0].astype(f32) * _silu(ag_ref[0].astype(f32))
    o = of_ref[0].astype(f32) + ob_ref[0].astype(f32)
    parts = []
    for hh in range(HEADS):
        oh = o[:, hh * HEAD_W:(hh + 1) * HEAD_W]
        parts.append(oh * lax.rsqrt(jnp.mean(oh * oh, axis=-1, keepdims=True) + EPS) * hgain_ref[...])
    hg = jnp.concatenate(parts, axis=1) * _silu(hg_ref[0].astype(f32))
    y_in = jnp.concatenate([att, hg], axis=1).astype(jnp.bfloat16)
    y = jnp.dot(y_in, w_ref[...], preferred_element_type=jnp.float32)
    u = ALPHA * x_ref[0] + gate_ref[0] * y
    mu = jnp.mean(u, axis=-1, keepdims=True)
    uc = u - mu
    var = jnp.mean(uc * uc, axis=-1, keepdims=True)
    o_ref[0] = uc * lax.rsqrt(var + EPS) * lng_ref[...] + lnb_ref[...]


def _merge(att, ag, o_f, o_b, hg, x, gate, w_out_bf, hgain, ln_g, ln_b, tm):
    B, N, D = x.shape
    half = pl.BlockSpec((1, tm, BRANCH_W), lambda b, i: (b, i, 0))
    full = pl.BlockSpec((1, tm, D), lambda b, i: (b, i, 0))
    row = lambda w: pl.BlockSpec((1, w), lambda b, i: (0, 0))
    return pl.pallas_call(
        _merge_kernel,
        grid=(B, N // tm),
        in_specs=[half, half, half, half, half, full,
                  pl.BlockSpec((1, 1, D), lambda b, i: (b, 0, 0)),
                  pl.BlockSpec(w_out_bf.shape, lambda b, i: (0, 0)),
                  row(HEAD_W), row(D), row(D)],
        out_specs=full,
        out_shape=jax.ShapeDtypeStruct((B, N, D), jnp.float32),
        compiler_params=pltpu.CompilerParams(
            dimension_semantics=("parallel", "parallel"), vmem_limit_bytes=V7X_VMEM_LIMIT_BYTES),
        name="merge_out_proj_ln",
    )(att, ag, o_f, o_b, hg, x, gate, w_out_bf, hgain, ln_g, ln_b)


def kernel(x, c, ctx, c_ctx, w_ada, b_ada, w_in, w_out, diff_lambda, diff_subln_gain, hgrn_lower_bound,
           hgrn_norm_gain, ln_gain, ln_bias):
    B, N, D = x.shape
    assert DEPTH == 1 and w_ada.shape[0] == 1
    cvec = jnp.concatenate([c, c_ctx[None, :], jnp.zeros((SUBLANES - B - 1, D), c.dtype)], axis=0)
    mod = _modulation(cvec, w_ada, b_ada)
    shift, scale, gate = mod[:, :D], mod[:, D:2 * D], mod[:, 2 * D:]
    s1p = (1.0 + scale)[:, None, :]
    shift = shift[:, None, :]
    ctx_rows = jnp.full((B,), B, jnp.int32)

    k_c, v_c, hq_c, hi_c, hff_c, hfb_c = _in_projection(
        ctx, s1p[ctx_rows], shift[ctx_rows], w_in, _CTX_GROUPS, tm=ctx.shape[1])
    q, k, v, ag, hq, hi, hff, hfb, hg = _in_projection(
        x, s1p[:B], shift[:B], w_in, _LATENT_GROUPS, tm=ROW_TILE, rope_tables=_rope_tables(N))

    att = _diff_attention(q, k_c, v_c, k, v, diff_lambda, diff_subln_gain)

    _, _, s_ctx = _hgrn2_bidir(hq_c, hi_c, hff_c, hfb_c, hgrn_lower_bound, None, n_sub=ctx.shape[1] // HGRN_CHUNK)
    o_f, o_b, _ = _hgrn2_bidir(hq, hi, hff, hfb, hgrn_lower_bound, s_ctx, n_sub=HGRN_CHUNKS_PER_STEP)

    return _merge(att, ag, o_f, o_b, hg, x, gate[:B, None, :], w_out[0].astype(jnp.bfloat16),
                  hgrn_norm_gain, ln_gain, ln_bias, tm=ROW_TILE)
```

```python
import functools
import math

import numpy as np
import jax
import jax.numpy as jnp
from jax import lax
from jax.experimental import pallas as pl
from jax.experimental.pallas import tpu as pltpu

DEPTH = 1
GRID_W = 64
HEADS = 4
HEAD_W = 128
MAP_D = 64
BRANCH_W = HEADS * HEAD_W
ROPE_BASE = 10000.0
ROPE_FREQS = MAP_D // 4
EPS = 1e-5
LAMBDA_INIT = 0.8 - 0.6 * math.exp(-0.3 * 0)
ALPHA = (2.0 * DEPTH) ** 0.25

V7X_VMEM_LIMIT_BYTES = 56 * 1024 * 1024
SUBLANES = 8
LANES = 128
ROW_TILE = 512
ATTN_TQ = 2048
ATTN_MAX_TK = 1408
HGRN_CHUNKS_PER_STEP = 8
HGRN_CHUNK = 128
HGRN_LEVELS = 7

_NT = (((1,), (1,)), ((), ()))
_TN = (((0,), (0,)), ((), ()))


def _silu(x):
    return x * jax.nn.sigmoid(x)


def _mod_kernel(c_ref, w_ref, b_ref, o_ref):
    a = _silu(c_ref[...]).astype(jnp.bfloat16)
    o_ref[...] = jnp.dot(a, w_ref[0].astype(jnp.bfloat16), preferred_element_type=jnp.float32) + b_ref[...]


def _modulation(cvec, w_ada, b_ada):
    rows, d = cvec.shape
    n_out = w_ada.shape[2]
    bn = 1024
    return pl.pallas_call(
        _mod_kernel,
        grid=(n_out // bn,),
        in_specs=[pl.BlockSpec((rows, d), lambda j: (0, 0)),
                  pl.BlockSpec((1, d, bn), lambda j: (0, 0, j)),
                  pl.BlockSpec((1, bn), lambda j: (0, j))],
        out_specs=pl.BlockSpec((rows, bn), lambda j: (0, j)),
        out_shape=jax.ShapeDtypeStruct((rows, n_out), jnp.float32),
        compiler_params=pltpu.CompilerParams(vmem_limit_bytes=V7X_VMEM_LIMIT_BYTES),
        name="modulation",
    )(cvec, w_ada, b_ada)


_LATENT_GROUPS = ((0, "rope_q", jnp.bfloat16), (1, "rope", jnp.bfloat16), (2, "transposed", jnp.bfloat16),
                  (3, "plain", jnp.bfloat16), (4, "silu", jnp.float32), (5, "plain", jnp.bfloat16),
                  (6, "plain", jnp.float32), (7, "plain", jnp.float32), (8, "plain", jnp.bfloat16))
_CTX_GROUPS = ((1, "plain", jnp.bfloat16), (2, "transposed", jnp.bfloat16), (4, "silu", jnp.float32),
               (5, "plain", jnp.bfloat16), (6, "plain", jnp.float32), (7, "plain", jnp.float32))
Q_SCALE = math.log2(math.e) / math.sqrt(MAP_D)


def _proj_kernel(groups, use_rope, *refs):
    if use_rope:
        x_ref, s1p_ref, sh_ref, w_ref, cos_row_ref, sin_row_ref, cos_col_ref, sin_col_ref = refs[:8]
        out_refs = refs[8:]
    else:
        x_ref, s1p_ref, sh_ref, w_ref = refs[:4]
        out_refs = refs[4:]
    h = (x_ref[0] * s1p_ref[0] + sh_ref[0]).astype(jnp.bfloat16)
    if use_rope:
        by_row = (lax.broadcasted_iota(jnp.int32, (GRID_W, HEAD_W), 1) & (MAP_D - 1)) < 2 * ROPE_FREQS

        def token_table(row_ref, col_ref):
            grid_rows = [jnp.where(by_row, row_ref[r:r + 1, :], col_ref[...]) for r in range(row_ref.shape[0])]
            return jnp.concatenate([jnp.concatenate(grid_rows, axis=0)] * HEADS, axis=1)

        cos = token_table(cos_row_ref, cos_col_ref)
        sin = token_table(sin_row_ref, sin_col_ref)
        lane = lax.broadcasted_iota(jnp.int32, cos.shape, 1)
        first_half = (lane & (2 * ROPE_FREQS - 1)) < ROPE_FREQS
    for (g, kind, dt), o_ref in zip(groups, out_refs):
        p = jnp.dot(h, w_ref[0, :, g * BRANCH_W:(g + 1) * BRANCH_W].astype(jnp.bfloat16),
                    preferred_element_type=jnp.float32)
        if kind in ("rope", "rope_q"):
            partner = jnp.where(first_half,
                                pltpu.roll(p, BRANCH_W - ROPE_FREQS, axis=1),
                                pltpu.roll(p, ROPE_FREQS, axis=1))
            p = p * cos + partner * sin
            if kind == "rope_q":
                p = p * Q_SCALE
        elif kind == "silu":
            p = _silu(p)
        elif kind == "transposed":
            p = p.T
        o_ref[0] = p.astype(dt)


def _in_projection(x, s1p, sh, w_in, groups, tm, rope_tables=None):
    B, T, D = x.shape
    use_rope = rope_tables is not None
    in_specs = [pl.BlockSpec((1, tm, D), lambda b, i: (b, i, 0)),
                pl.BlockSpec((1, 1, D), lambda b, i: (b, 0, 0)),
                pl.BlockSpec((1, 1, D), lambda b, i: (b, 0, 0)),
                pl.BlockSpec(w_in.shape, lambda b, i: (0, 0, 0), pipeline_mode=pl.Buffered(1))]
    args = [x, s1p, sh, w_in]
    if use_rope:
        assert tm % (GRID_W * SUBLANES) == 0
        in_specs += [pl.BlockSpec((tm // GRID_W, HEAD_W), lambda b, i: (i, 0))] * 2
        in_specs += [pl.BlockSpec((GRID_W, HEAD_W), lambda b, i: (0, 0))] * 2
        args += list(rope_tables)
    out_specs = [pl.BlockSpec((1, BRANCH_W, tm), lambda b, i: (b, 0, i)) if kind == "transposed"
                 else pl.BlockSpec((1, tm, BRANCH_W), lambda b, i: (b, i, 0)) for _, kind, _ in groups]
    out_shape = [jax.ShapeDtypeStruct((B, BRANCH_W, T) if kind == "transposed" else (B, T, BRANCH_W), dt)
                 for _, kind, dt in groups]
    return pl.pallas_call(
        functools.partial(_proj_kernel, groups, use_rope),
        grid=(B, T // tm),
        in_specs=in_specs, out_specs=out_specs, out_shape=out_shape,
        compiler_params=pltpu.CompilerParams(
            dimension_semantics=("parallel", "parallel"), vmem_limit_bytes=V7X_VMEM_LIMIT_BYTES),
        name="in_proj_rope" if use_rope else "in_proj_ctx",
    )(*args)


def _rope_tables(n_tokens):
    inv_freq = jnp.asarray(ROPE_BASE, jnp.float32) ** (-jnp.arange(ROPE_FREQS, dtype=jnp.float32) / ROPE_FREQS)
    freq = jnp.tile(inv_freq, HEAD_W // ROPE_FREQS)
    sign = np.where((np.arange(HEAD_W) % (2 * ROPE_FREQS)) < ROPE_FREQS, -1.0, 1.0).astype(np.float32)
    row_ang = jnp.arange(n_tokens // GRID_W, dtype=jnp.float32)[:, None] * freq
    col_ang = jnp.arange(GRID_W, dtype=jnp.float32)[:, None] * freq
    return jnp.cos(row_ang), jnp.sin(row_ang) * sign, jnp.cos(col_ang), jnp.sin(col_ang) * sign


def _attn_kernel(tk, q_ref, kc_ref, vtc_ref, kl_ref, vtl_ref, lam_ref, gain_ref, o_ref,
                 k_ref, vt_ref, s_ref, p_ref, acc_ref):
    t_ctx = kc_ref.shape[1]

    @pl.when(pl.program_id(2) == 0)
    def _():
        k_ref[:t_ctx, :] = kc_ref[0]
        k_ref[t_ctx:, :] = kl_ref[0]
        vt_ref[:, :t_ctx] = vtc_ref[0]
        vt_ref[:, t_ctx:] = vtl_ref[0]

    qt = q_ref[0].astype(jnp.float32).T
    row = lax.broadcasted_iota(jnp.int32, qt.shape, 0)
    qm = tuple(jnp.where(sel, qt, 0.0).astype(jnp.bfloat16) for sel in (row < MAP_D, row >= MAP_D))
    tq = qt.shape[1]
    n_chunks = k_ref.shape[0] // tk

    def scores(j):
        kb = k_ref[pl.ds(pl.multiple_of(j * tk, tk), tk), :]
        smax = []
        for m in range(2):
            s = jnp.dot(kb, qm[m], preferred_element_type=jnp.float32)
            s_ref[m] = s
            smax.append(jnp.max(s, axis=0, keepdims=True))
        return tuple(smax)

    def weights(smax, ml):
        new_ml, corrs = [], []
        for m in range(2):
            mx, l = ml[m]
            mx_new = jnp.maximum(mx, smax[m])
            corr = jnp.exp2(mx - mx_new)
            p = jnp.exp2(s_ref[m] - mx_new)
            p_ref[m] = p.astype(jnp.bfloat16)
            new_ml.append((mx_new, l * corr + jnp.sum(p, axis=0, keepdims=True)))
            corrs.append(corr)
        return tuple(new_ml), tuple(corrs)

    def values(j, corr):
        vtb = vt_ref[:, pl.ds(pl.multiple_of(j * tk, tk), tk)]
        for m in range(2):
            acc_ref[m] = acc_ref[m] * corr[m] + jnp.dot(vtb, p_ref[m], preferred_element_type=jnp.float32)

    ml = tuple((jnp.full((1, tq), -jnp.inf, jnp.float32), jnp.zeros((1, tq), jnp.float32)) for _ in range(2))
    acc_ref[...] = jnp.zeros_like(acc_ref)
    ml, corr = weights(scores(0), ml)
    smax = scores(1)

    def body(j, carry):
        ml, smax, corr_prev = carry
        values(j - 1, corr_prev)
        ml, corr = weights(smax, ml)
        return ml, scores(j + 1), corr

    ml, smax, corr = lax.fori_loop(1, n_chunks - 1, body, (ml, smax, corr))
    values(n_chunks - 2, corr)
    ml, corr = weights(smax, ml)
    values(n_chunks - 1, corr)

    lp = lam_ref[0]
    lam = (jnp.exp(jnp.sum(lp[0:1] * lp[1:2], axis=-1, keepdims=True))
           - jnp.exp(jnp.sum(lp[2:3] * lp[3:4], axis=-1, keepdims=True)) + LAMBDA_INIT)
    (_, l0), (_, l1) = ml
    ot = acc_ref[0] / l0 - lam * (acc_ref[1] / l1)
    ot = ot * lax.rsqrt(jnp.mean(ot * ot, axis=0, keepdims=True) + EPS)
    o_ref[0] = (ot.T * gain_ref[...] * (1.0 - LAMBDA_INIT)).astype(o_ref.dtype)


def _key_chunk(n_keys):
    return max(t for t in range(LANES, ATTN_MAX_TK + 1, LANES) if n_keys % t == 0)


def _diff_attention(q, k_ctx, vt_ctx, k, vt, diff_lambda, gain):
    B, N, _ = q.shape
    T_ctx = k_ctx.shape[1]
    Tk = T_ctx + N
    tq, tk = ATTN_TQ, _key_chunk(Tk)
    assert N % tq == 0 and Tk // tk >= 3 and T_ctx % LANES == 0
    return pl.pallas_call(
        functools.partial(_attn_kernel, tk),
        grid=(B, HEADS, N // tq),
        in_specs=[pl.BlockSpec((1, tq, HEAD_W), lambda b, h, i: (b, i, h)),
                  pl.BlockSpec((1, T_ctx, HEAD_W), lambda b, h, i: (b, 0, h)),
                  pl.BlockSpec((1, HEAD_W, T_ctx), lambda b, h, i: (b, h, 0)),
                  pl.BlockSpec((1, N, HEAD_W), lambda b, h, i: (b, 0, h)),
                  pl.BlockSpec((1, HEAD_W, N), lambda b, h, i: (b, h, 0)),
                  pl.BlockSpec((1, 4, MAP_D), lambda b, h, i: (0, 0, 0)),
                  pl.BlockSpec((1, HEAD_W), lambda b, h, i: (0, 0))],
        out_specs=pl.BlockSpec((1, tq, HEAD_W), lambda b, h, i: (b, i, h)),
        out_shape=jax.ShapeDtypeStruct((B, N, BRANCH_W), jnp.bfloat16),
        scratch_shapes=[pltpu.VMEM((Tk, HEAD_W), jnp.bfloat16), pltpu.VMEM((HEAD_W, Tk), jnp.bfloat16),
                        pltpu.VMEM((2, tk, tq), jnp.float32), pltpu.VMEM((2, tk, tq), jnp.bfloat16),
                        pltpu.VMEM((2, HEAD_W, tq), jnp.float32)],
        compiler_params=pltpu.CompilerParams(
            dimension_semantics=("parallel", "parallel", "arbitrary"),
            vmem_limit_bytes=V7X_VMEM_LIMIT_BYTES),
        name="diff_attention",
    )(q, k_ctx, vt_ctx, k, vt, diff_lambda, gain)


def _level_maps():
    t = np.arange(HGRN_CHUNK)[:, None]
    s = np.arange(HGRN_CHUNK)[None, :]
    x = t ^ s
    lvl = np.where(x > 0, np.floor(np.log2(np.maximum(x, 1))).astype(np.int32) + 1, 0)
    lvl = np.where(s > t, -1, lvl).astype(np.int32)
    return np.stack([lvl, lvl.T])


def _shift_down(x, s):
    return pltpu.roll(x, s, axis=1)


def _shift_up(x, s):
    return pltpu.roll(x, SUBLANES - s, axis=1)


def _neg_abs(x):
    bits = lax.bitcast_convert_type(x, jnp.uint32) | jnp.uint32(0x80000000)
    return lax.bitcast_convert_type(bits, jnp.float32)


def _hgrn_chain(q, z, v_bf, lb, st, masks, reverse):
    C = HGRN_CHUNK
    groups = C // SUBLANES
    sig = jax.nn.sigmoid(z)
    g = jnp.log2(lb + (1.0 - lb) * sig)
    kk = (1.0 - lb) * (1.0 - sig)
    grouped = (groups, SUBLANES, HEAD_W)
    sub = lax.broadcasted_iota(jnp.int32, grouped, 1)

    bg = g.reshape(grouped)
    for s in (1, 2, 4):
        if reverse:
            bg = bg + jnp.where(sub < SUBLANES - s, _shift_up(bg, s), 0.0)
        else:
            bg = bg + jnp.where(sub >= s, _shift_down(bg, s), 0.0)
    parts = [bg[r] for r in range(groups)]
    order = range(groups - 2, -1, -1) if reverse else range(1, groups)
    for r in order:
        prev = parts[r + 1][0:1, :] if reverse else parts[r - 1][SUBLANES - 1:SUBLANES, :]
        parts[r] = parts[r] + prev
    b = jnp.concatenate(parts, axis=0)
    bg = b.reshape(grouped)

    q_bf = q.astype(jnp.bfloat16)
    kk_bf = kk.astype(jnp.bfloat16)

    def rows(x, r):
        return x[r * SUBLANES:(r + 1) * SUBLANES, :]

    a_parts = [None] * groups

    own = bg
    nbr = _shift_up(bg, 1) if reverse else _shift_down(bg, 1)
    for j in range(1, HGRN_LEVELS + 1):
        h = 1 << (j - 1)
        if 2 * h <= SUBLANES:
            upper = (sub & h) != 0
            ref = jnp.where(upper, own, nbr) if reverse else jnp.where(upper, nbr, own)
            w = jnp.exp2(_neg_abs(bg - ref)).reshape(C, HEAD_W).astype(jnp.bfloat16)
            if 4 * h <= SUBLANES:
                if reverse:
                    own = jnp.where(upper, _shift_down(own, h), own)
                    nbr = jnp.where(upper, nbr, _shift_up(nbr, h))
                else:
                    own = jnp.where(upper, own, _shift_up(own, h))
                    nbr = jnp.where(upper, _shift_down(nbr, h), nbr)
            pm = lax.dot_general(q_bf * w, kk_bf * w, _NT, preferred_element_type=jnp.float32)
            a_parts = [jnp.where(rows(masks[j], r), rows(pm, r), 0.0 if a_parts[r] is None else a_parts[r])
                       for r in range(groups)]
        else:
            n_blk, half_groups = C // (2 * h), h // SUBLANES
            blocked = (n_blk, 2 * h, HEAD_W)
            b_blk = b.reshape(blocked)
            ref = b_blk[:, h:h + 1, :] if reverse else b_blk[:, h - 1:h, :]
            w_blk = jnp.exp2(_neg_abs(b_blk - ref))
            q_half, k_half = (slice(0, h), slice(h, 2 * h)) if reverse else (slice(h, 2 * h), slice(0, h))
            q_t = (q.reshape(blocked)[:, q_half] * w_blk[:, q_half]).reshape(C // 2, HEAD_W)
            k_t = kk.reshape(blocked)[:, k_half] * w_blk[:, k_half]
            zeros = jnp.zeros_like(k_t)
            k_t = jnp.concatenate([zeros, k_t] if reverse else [k_t, zeros], axis=1).reshape(C, HEAD_W)
            pm = lax.dot_general(q_t.astype(jnp.bfloat16), k_t.astype(jnp.bfloat16), _NT,
                                 preferred_element_type=jnp.float32)
            for blk in range(n_blk):
                for i in range(half_groups):
                    r = blk * 2 * half_groups + (0 if reverse else half_groups) + i
                    piece = rows(pm, blk * half_groups + i)
                    if n_blk == 1:
                        a_parts[r] = a_parts[r] + piece
                    else:
                        a_parts[r] = jnp.where(rows(masks[j], r), piece, a_parts[r])
    a = jnp.concatenate(a_parts, axis=0)

    o = jnp.dot(a.astype(jnp.bfloat16), v_bf, preferred_element_type=jnp.float32)
    o = o + lax.dot_general(q_bf * jnp.exp2(b).astype(jnp.bfloat16), st.astype(jnp.bfloat16), _NT,
                            preferred_element_type=jnp.float32)
    o = o + jnp.sum(q * kk, axis=-1, keepdims=True) * v_bf.astype(jnp.float32)
    b_tot = b[0:1, :] if reverse else b[C - 1:C, :]
    k_hat = kk_bf * jnp.exp2(b_tot - b).astype(jnp.bfloat16)
    st_new = st * jnp.exp2(b_tot) + lax.dot_general(v_bf, k_hat, _TN, preferred_element_type=jnp.float32)
    return o, st_new


def _hgrn_kernel(n_sub, has_s0, qf_ref, vf_ref, zf_ref, qb_ref, vb_ref, zb_ref, lbp_ref, lvl_ref, *refs):
    s0_ref, (of_ref, ob_ref, sfin_ref, st_ref) = (refs[0], refs[1:]) if has_s0 else (None, refs)
    i = pl.program_id(1)
    C = HGRN_CHUNK

    @pl.when(i == 0)
    def _():
        st_ref[...] = jnp.zeros_like(st_ref) if s0_ref is None else s0_ref[0]

    dirs = ((qf_ref, vf_ref, zf_ref, of_ref), (qb_ref, vb_ref, zb_ref, ob_ref))
    masks, lbs = [], []
    for d in range(2):
        lvl = lvl_ref[d]
        masks.append([lvl == j for j in range(HGRN_LEVELS + 1)])
        p0, p1 = lbp_ref[d, 0:1, :], lbp_ref[d, 1:2, :]
        pm = jnp.maximum(p0, p1)
        e0 = jnp.exp(p0 - pm)
        lbs.append(e0 / (e0 + jnp.exp(p1 - pm)))
    states = [[st_ref[d, hh] for hh in range(HEADS)] for d in range(2)]
    for sub in range(n_sub):
        for d, (q_ref, v_ref, z_ref, o_ref) in enumerate(dirs):
            c = n_sub - 1 - sub if d == 1 else sub
            tok = slice(c * C, (c + 1) * C)
            for hh in range(HEADS):
                cols = slice(hh * HEAD_W, (hh + 1) * HEAD_W)
                o, states[d][hh] = _hgrn_chain(q_ref[0, tok, cols], z_ref[0, tok, cols], v_ref[0, tok, cols],
                                               lbs[d][:, cols], states[d][hh], masks[d], reverse=(d == 1))
                o_ref[0, tok, cols] = o.astype(o_ref.dtype)
    for d in range(2):
        for hh in range(HEADS):
            st_ref[d, hh] = states[d][hh]

    @pl.when(i == pl.num_programs(1) - 1)
    def _():
        sfin_ref[0] = st_ref[...]


def _hgrn2_bidir(hq, hi, hff, hfb, lb_param, s0, n_sub):
    B, T, _ = hq.shape
    rows = HGRN_CHUNK * n_sub
    n = T // rows
    assert n * rows == T
    fwd = pl.BlockSpec((1, rows, BRANCH_W), lambda b, i: (b, i, 0))
    bwd = pl.BlockSpec((1, rows, BRANCH_W), lambda b, i: (b, n - 1 - i, 0))
    st_spec = pl.BlockSpec((1, 2, HEADS, HEAD_W, HEAD_W), lambda b, i: (b, 0, 0, 0, 0))
    lvl = jnp.asarray(_level_maps())
    return pl.pallas_call(
        functools.partial(_hgrn_kernel, n_sub, s0 is not None),
        grid=(B, n),
        in_specs=[fwd, fwd, fwd, bwd, bwd, bwd,
                  pl.BlockSpec(lb_param.shape, lambda b, i: (0, 0, 0)),
                  pl.BlockSpec(lvl.shape, lambda b, i: (0, 0, 0))] + ([] if s0 is None else [st_spec]),
        out_specs=[fwd, bwd, st_spec],
        out_shape=[jax.ShapeDtypeStruct((B, T, BRANCH_W), jnp.bfloat16),
                   jax.ShapeDtypeStruct((B, T, BRANCH_W), jnp.bfloat16),
                   jax.ShapeDtypeStruct((B, 2, HEADS, HEAD_W, HEAD_W), jnp.float32)],
        scratch_shapes=[pltpu.VMEM((2, HEADS, HEAD_W, HEAD_W), jnp.float32)],
        compiler_params=pltpu.CompilerParams(
            dimension_semantics=("parallel", "arbitrary"), vmem_limit_bytes=V7X_VMEM_LIMIT_BYTES),
        name="hgrn2_bidir",
    )(hq, hi, hff, hq, hi, hfb, lb_param, lvl, *([] if s0 is None else [s0]))


def _merge_kernel(att_ref, ag_ref, of_ref, ob_ref, hg_ref, x_ref, gate_ref, w_ref, hgain_ref, lng_ref, lnb_ref,
                  o_ref):
    f32 = jnp.float32
    att = att_ref[0].astype(f32) * _silu(ag_ref[0].astype(f32))
    o = of_ref[0].astype(f32) + ob_ref[0].astype(f32)
    parts = []
    for hh in range(HEADS):
        oh = o[:, hh * HEAD_W:(hh + 1) * HEAD_W]
        parts.append(oh * lax.rsqrt(jnp.mean(oh * oh, axis=-1, keepdims=True) + EPS) * hgain_ref[...])
    hg = jnp.concatenate(parts, axis=1) * _silu(hg_ref[0].astype(f32))
    y_in = jnp.concatenate([att, hg], axis=1).astype(jnp.bfloat16)
    y = jnp.dot(y_in, w_ref[...], preferred_element_type=jnp.float32)
    u = ALPHA * x_ref[0] + gate_ref[0] * y
    mu = jnp.mean(u, axis=-1, keepdims=True)
    uc = u - mu
    var = jnp.mean(uc * uc, axis=-1, keepdims=True)
    o_ref[0] = uc * lax.rsqrt(var + EPS) * lng_ref[...] + lnb_ref[...]


def _merge(att, ag, o_f, o_b, hg, x, gate, w_out_bf, hgain, ln_g, ln_b, tm):
    B, N, D = x.shape
    half = pl.BlockSpec((1, tm, BRANCH_W), lambda b, i: (b, i, 0))
    full = pl.BlockSpec((1, tm, D), lambda b, i: (b, i, 0))
    row = lambda w: pl.BlockSpec((1, w), lambda b, i: (0, 0))
    return pl.pallas_call(
        _merge_kernel,
        grid=(B, N // tm),
        in_specs=[half, half, half, half, half, full,
                  pl.BlockSpec((1, 1, D), lambda b, i: (b, 0, 0)),
                  pl.BlockSpec(w_out_bf.shape, lambda b, i: (0, 0)),
                  row(HEAD_W), row(D), row(D)],
        out_specs=full,
        out_shape=jax.ShapeDtypeStruct((B, N, D), jnp.float32),
        compiler_params=pltpu.CompilerParams(
            dimension_semantics=("parallel", "parallel"), vmem_limit_bytes=V7X_VMEM_LIMIT_BYTES),
        name="merge_out_proj_ln",
    )(att, ag, o_f, o_b, hg, x, gate, w_out_bf, hgain, ln_g, ln_b)


def kernel(x, c, ctx, c_ctx, w_ada, b_ada, w_in, w_out, diff_lambda, diff_subln_gain, hgrn_lower_bound,
           hgrn_norm_gain, ln_gain, ln_bias):
    B, N, D = x.shape
    assert DEPTH == 1 and w_ada.shape[0] == 1
    cvec = jnp.concatenate([c, c_ctx[None, :], jnp.zeros((SUBLANES - B - 1, D), c.dtype)], axis=0)
    mod = _modulation(cvec, w_ada, b_ada)
    shift, scale, gate = mod[:, :D], mod[:, D:2 * D], mod[:, 2 * D:]
    s1p = (1.0 + scale)[:, None, :]
    shift = shift[:, None, :]
    ctx_rows = jnp.full((B,), B, jnp.int32)

    k_c, v_c, hq_c, hi_c, hff_c, hfb_c = _in_projection(
        ctx, s1p[ctx_rows], shift[ctx_rows], w_in, _CTX_GROUPS, tm=ctx.shape[1])
    q, k, v, ag, hq, hi, hff, hfb, hg = _in_projection(
        x, s1p[:B], shift[:B], w_in, _LATENT_GROUPS, tm=ROW_TILE, rope_tables=_rope_tables(N))

    att = _diff_attention(q, k_c, v_c, k, v, diff_lambda, diff_subln_gain)

    _, _, s_ctx = _hgrn2_bidir(hq_c, hi_c, hff_c, hfb_c, hgrn_lower_bound, None, n_sub=ctx.shape[1] // HGRN_CHUNK)
    o_f, o_b, _ = _hgrn2_bidir(hq, hi, hff, hfb, hgrn_lower_bound, s_ctx, n_sub=HGRN_CHUNKS_PER_STEP)

    return _merge(att, ag, o_f, o_b, hg, x, gate[:B, None, :], w_out[0].astype(jnp.bfloat16),
                  hgrn_norm_gain, ln_gain, ln_bias, tm=ROW_TILE)
```

```python
import functools
import math

import numpy as np
import jax
import jax.numpy as jnp
from jax import lax
from jax.experimental import pallas as pl
from jax.experimental.pallas import tpu as pltpu

DEPTH = 1
GRID_W = 64
HEADS = 4
HEAD_W = 128
MAP_D = 64
BRANCH_W = HEADS * HEAD_W
ROPE_BASE = 10000.0
ROPE_FREQS = MAP_D // 4
EPS = 1e-5
LAMBDA_INIT = 0.8 - 0.6 * math.exp(-0.3 * 0)
ALPHA = (2.0 * DEPTH) ** 0.25

V7X_VMEM_LIMIT_BYTES = 56 * 1024 * 1024
SUBLANES = 8
LANES = 128
SUM_ROWS = 16
ROW_TILE = 512
ATTN_TQ = 2048
ATTN_MAX_TK = 1408
HGRN_CHUNKS_PER_STEP = 4
HGRN_CHUNK = 128
HGRN_LEVELS = 7

_NT = (((1,), (1,)), ((), ()))
_TN = (((0,), (0,)), ((), ()))


def _silu(x):
    return x * jax.nn.sigmoid(x)


def _mod_kernel(c_ref, w_ref, b_ref, o_ref):
    a = _silu(c_ref[...]).astype(jnp.bfloat16)
    o_ref[...] = jnp.dot(a, w_ref[0].astype(jnp.bfloat16), preferred_element_type=jnp.float32) + b_ref[...]


def _modulation(cvec, w_ada, b_ada):
    rows, d = cvec.shape
    n_out = w_ada.shape[2]
    bn = 1024
    return pl.pallas_call(
        _mod_kernel,
        grid=(n_out // bn,),
        in_specs=[pl.BlockSpec((rows, d), lambda j: (0, 0)),
                  pl.BlockSpec((1, d, bn), lambda j: (0, 0, j)),
                  pl.BlockSpec((1, bn), lambda j: (0, j))],
        out_specs=pl.BlockSpec((rows, bn), lambda j: (0, j)),
        out_shape=jax.ShapeDtypeStruct((rows, n_out), jnp.float32),
        compiler_params=pltpu.CompilerParams(vmem_limit_bytes=V7X_VMEM_LIMIT_BYTES),
        name="modulation",
    )(cvec, w_ada, b_ada)


_LATENT_GROUPS = ((0, "rope_q", jnp.bfloat16), (1, "rope", jnp.bfloat16), (2, "transposed", jnp.bfloat16),
                  (3, "plain", jnp.bfloat16), (4, "silu", jnp.float32), (5, "plain", jnp.bfloat16),
                  (6, "plain", jnp.float32), (7, "plain", jnp.float32), (8, "plain", jnp.bfloat16))
_CTX_GROUPS = ((1, "plain", jnp.bfloat16), (2, "transposed", jnp.bfloat16), (4, "silu", jnp.float32),
               (5, "plain", jnp.bfloat16), (6, "plain", jnp.float32), (7, "plain", jnp.float32))
Q_SCALE = math.log2(math.e) / math.sqrt(MAP_D)


def _proj_kernel(groups, use_rope, *refs):
    if use_rope:
        x_ref, s1p_ref, sh_ref, w_ref, cos_row_ref, sin_row_ref, cos_col_ref, sin_col_ref = refs[:8]
        out_refs = refs[8:]
    else:
        x_ref, s1p_ref, sh_ref, w_ref = refs[:4]
        out_refs = refs[4:]
    h = (x_ref[0] * s1p_ref[0] + sh_ref[0]).astype(jnp.bfloat16)
    if use_rope:
        by_row = (lax.broadcasted_iota(jnp.int32, (GRID_W, HEAD_W), 1) & (MAP_D - 1)) < 2 * ROPE_FREQS

        def token_table(row_ref, col_ref):
            grid_rows = [jnp.where(by_row, row_ref[r:r + 1, :], col_ref[...]) for r in range(row_ref.shape[0])]
            return jnp.concatenate([jnp.concatenate(grid_rows, axis=0)] * HEADS, axis=1)

        cos = token_table(cos_row_ref, cos_col_ref)
        sin = token_table(sin_row_ref, sin_col_ref)
        lane = lax.broadcasted_iota(jnp.int32, cos.shape, 1)
        first_half = (lane & (2 * ROPE_FREQS - 1)) < ROPE_FREQS
    for (g, kind, dt), o_ref in zip(groups, out_refs):
        p = jnp.dot(h, w_ref[0, :, g * BRANCH_W:(g + 1) * BRANCH_W].astype(jnp.bfloat16),
                    preferred_element_type=jnp.float32)
        if kind in ("rope", "rope_q"):
            partner = jnp.where(first_half,
                                pltpu.roll(p, BRANCH_W - ROPE_FREQS, axis=1),
                                pltpu.roll(p, ROPE_FREQS, axis=1))
            p = p * cos + partner * sin
            if kind == "rope_q":
                p = p * Q_SCALE
        elif kind == "silu":
            p = _silu(p)
        elif kind == "transposed":
            p = p.T
        o_ref[0] = p.astype(dt)


def _in_projection(x, s1p, sh, w_in, groups, tm, rope_tables=None):
    B, T, D = x.shape
    use_rope = rope_tables is not None
    in_specs = [pl.BlockSpec((1, tm, D), lambda b, i: (b, i, 0)),
                pl.BlockSpec((1, 1, D), lambda b, i: (b, 0, 0)),
                pl.BlockSpec((1, 1, D), lambda b, i: (b, 0, 0)),
                pl.BlockSpec(w_in.shape, lambda b, i: (0, 0, 0), pipeline_mode=pl.Buffered(1))]
    args = [x, s1p, sh, w_in]
    if use_rope:
        assert tm % (GRID_W * SUBLANES) == 0
        in_specs += [pl.BlockSpec((tm // GRID_W, HEAD_W), lambda b, i: (i, 0))] * 2
        in_specs += [pl.BlockSpec((GRID_W, HEAD_W), lambda b, i: (0, 0))] * 2
        args += list(rope_tables)
    out_specs = [pl.BlockSpec((1, BRANCH_W, tm), lambda b, i: (b, 0, i)) if kind == "transposed"
                 else pl.BlockSpec((1, tm, BRANCH_W), lambda b, i: (b, i, 0)) for _, kind, _ in groups]
    out_shape = [jax.ShapeDtypeStruct((B, BRANCH_W, T) if kind == "transposed" else (B, T, BRANCH_W), dt)
                 for _, kind, dt in groups]
    return pl.pallas_call(
        functools.partial(_proj_kernel, groups, use_rope),
        grid=(B, T // tm),
        in_specs=in_specs, out_specs=out_specs, out_shape=out_shape,
        compiler_params=pltpu.CompilerParams(
            dimension_semantics=("parallel", "parallel"), vmem_limit_bytes=V7X_VMEM_LIMIT_BYTES),
        name="in_proj_rope" if use_rope else "in_proj_ctx",
    )(*args)


def _rope_tables(n_tokens):
    inv_freq = jnp.asarray(ROPE_BASE, jnp.float32) ** (-jnp.arange(ROPE_FREQS, dtype=jnp.float32) / ROPE_FREQS)
    freq = jnp.tile(inv_freq, HEAD_W // ROPE_FREQS)
    sign = np.where((np.arange(HEAD_W) % (2 * ROPE_FREQS)) < ROPE_FREQS, -1.0, 1.0).astype(np.float32)
    row_ang = jnp.arange(n_tokens // GRID_W, dtype=jnp.float32)[:, None] * freq
    col_ang = jnp.arange(GRID_W, dtype=jnp.float32)[:, None] * freq
    return jnp.cos(row_ang), jnp.sin(row_ang) * sign, jnp.cos(col_ang), jnp.sin(col_ang) * sign


def _attn_kernel(tk, q_ref, kc_ref, vtc_ref, kl_ref, vtl_ref, lam_ref, gain_ref, o_ref,
                 k_ref, vt_ref, s_ref, p_ref, acc_ref):
    t_ctx = kc_ref.shape[1]

    @pl.when(pl.program_id(2) == 0)
    def _():
        k_ref[:t_ctx, :] = kc_ref[0]
        k_ref[t_ctx:, :] = kl_ref[0]
        vt_ref[:HEAD_W, :t_ctx] = vtc_ref[0]
        vt_ref[:HEAD_W, t_ctx:] = vtl_ref[0]
        extra = lax.broadcasted_iota(jnp.int32, (SUM_ROWS, vt_ref.shape[1]), 0) == 0
        vt_ref[HEAD_W:, :] = jnp.where(extra, 1.0, 0.0).astype(vt_ref.dtype)

    qt = q_ref[0].astype(jnp.float32).T
    row = lax.broadcasted_iota(jnp.int32, qt.shape, 0)
    qm = tuple(jnp.where(sel, qt, 0.0).astype(jnp.bfloat16) for sel in (row < MAP_D, row >= MAP_D))
    tq = qt.shape[1]
    n_chunks = k_ref.shape[0] // tk

    def scores(j):
        kb = k_ref[pl.ds(pl.multiple_of(j * tk, tk), tk), :]
        smax = []
        for m in range(2):
            s = jnp.dot(kb, qm[m], preferred_element_type=jnp.float32)
            s_ref[m] = s
            smax.append(jnp.max(s, axis=0, keepdims=True))
        return tuple(smax)

    def weights(smax, mx):
        new_mx, corrs = [], []
        for m in range(2):
            mx_new = jnp.maximum(mx[m], smax[m])
            p_ref[m] = jnp.exp2(s_ref[m] - mx_new).astype(jnp.bfloat16)
            new_mx.append(mx_new)
            corrs.append(jnp.exp2(mx[m] - mx_new))
        return tuple(new_mx), tuple(corrs)

    def values(j, corr):
        vtb = vt_ref[:, pl.ds(pl.multiple_of(j * tk, tk), tk)]
        for m in range(2):
            acc_ref[m] = acc_ref[m] * corr[m] + jnp.dot(vtb, p_ref[m], preferred_element_type=jnp.float32)

    mx = tuple(jnp.full((1, tq), -jnp.inf, jnp.float32) for _ in range(2))
    acc_ref[...] = jnp.zeros_like(acc_ref)
    mx, corr = weights(scores(0), mx)
    smax = scores(1)

    def body(j, carry):
        mx, smax, corr_prev = carry
        values(j - 1, corr_prev)
        mx, corr = weights(smax, mx)
        return mx, scores(j + 1), corr

    mx, smax, corr = lax.fori_loop(1, n_chunks - 1, body, (mx, smax, corr))
    values(n_chunks - 2, corr)
    mx, corr = weights(smax, mx)
    values(n_chunks - 1, corr)

    lp = lam_ref[0]
    lam = (jnp.exp(jnp.sum(lp[0:1] * lp[1:2], axis=-1, keepdims=True))
           - jnp.exp(jnp.sum(lp[2:3] * lp[3:4], axis=-1, keepdims=True)) + LAMBDA_INIT)
    l0, l1 = acc_ref[0, HEAD_W:HEAD_W + 1, :], acc_ref[1, HEAD_W:HEAD_W + 1, :]
    ot = acc_ref[0, :HEAD_W, :] / l0 - lam * (acc_ref[1, :HEAD_W, :] / l1)
    ot = ot * lax.rsqrt(jnp.mean(ot * ot, axis=0, keepdims=True) + EPS)
    o_ref[0] = (ot.T * gain_ref[...] * (1.0 - LAMBDA_INIT)).astype(o_ref.dtype)


def _key_chunk(n_keys):
    return max(t for t in range(LANES, ATTN_MAX_TK + 1, LANES) if n_keys % t == 0)


def _diff_attention(q, k_ctx, vt_ctx, k, vt, diff_lambda, gain):
    B, N, _ = q.shape
    T_ctx = k_ctx.shape[1]
    Tk = T_ctx + N
    tq, tk = ATTN_TQ, _key_chunk(Tk)
    assert N % tq == 0 and Tk // tk >= 3 and T_ctx % LANES == 0
    return pl.pallas_call(
        functools.partial(_attn_kernel, tk),
        grid=(B, HEADS, N // tq),
        in_specs=[pl.BlockSpec((1, tq, HEAD_W), lambda b, h, i: (b, i, h)),
                  pl.BlockSpec((1, T_ctx, HEAD_W), lambda b, h, i: (b, 0, h)),
                  pl.BlockSpec((1, HEAD_W, T_ctx), lambda b, h, i: (b, h, 0)),
                  pl.BlockSpec((1, N, HEAD_W), lambda b, h, i: (b, 0, h)),
                  pl.BlockSpec((1, HEAD_W, N), lambda b, h, i: (b, h, 0)),
                  pl.BlockSpec((1, 4, MAP_D), lambda b, h, i: (0, 0, 0)),
                  pl.BlockSpec((1, HEAD_W), lambda b, h, i: (0, 0))],
        out_specs=pl.BlockSpec((1, tq, HEAD_W), lambda b, h, i: (b, i, h)),
        out_shape=jax.ShapeDtypeStruct((B, N, BRANCH_W), jnp.bfloat16),
        scratch_shapes=[pltpu.VMEM((Tk, HEAD_W), jnp.bfloat16), pltpu.VMEM((HEAD_W + SUM_ROWS, Tk), jnp.bfloat16),
                        pltpu.VMEM((2, tk, tq), jnp.float32), pltpu.VMEM((2, tk, tq), jnp.bfloat16),
                        pltpu.VMEM((2, HEAD_W + SUM_ROWS, tq), jnp.float32)],
        compiler_params=pltpu.CompilerParams(
            dimension_semantics=("parallel", "parallel", "arbitrary"),
            vmem_limit_bytes=V7X_VMEM_LIMIT_BYTES),
        name="diff_attention",
    )(q, k_ctx, vt_ctx, k, vt, diff_lambda, gain)


def _level_maps():
    t = np.arange(HGRN_CHUNK)[:, None]
    s = np.arange(HGRN_CHUNK)[None, :]
    x = t ^ s
    lvl = np.where(x > 0, np.floor(np.log2(np.maximum(x, 1))).astype(np.int32) + 1, 0)
    lvl = np.where(s > t, -1, lvl).astype(np.int32)
    return np.stack([lvl, lvl.T])


def _shift_down(x, s):
    return pltpu.roll(x, s, axis=1)


def _shift_up(x, s):
    return pltpu.roll(x, SUBLANES - s, axis=1)


def _neg_abs(x):
    bits = lax.bitcast_convert_type(x, jnp.uint32) | jnp.uint32(0x80000000)
    return lax.bitcast_convert_type(bits, jnp.float32)


def _hgrn_chain(q, z, v_bf, lb, st, masks, reverse):
    C = HGRN_CHUNK
    groups = C // SUBLANES
    sig = jax.nn.sigmoid(z)
    g = jnp.log2(lb + (1.0 - lb) * sig)
    kk = (1.0 - lb) * (1.0 - sig)
    grouped = (groups, SUBLANES, HEAD_W)
    sub = lax.broadcasted_iota(jnp.int32, grouped, 1)

    bg = g.reshape(grouped)
    for s in (1, 2, 4):
        if reverse:
            bg = bg + jnp.where(sub < SUBLANES - s, _shift_up(bg, s), 0.0)
        else:
            bg = bg + jnp.where(sub >= s, _shift_down(bg, s), 0.0)
    parts = [bg[r] for r in range(groups)]
    order = range(groups - 2, -1, -1) if reverse else range(1, groups)
    for r in order:
        prev = parts[r + 1][0:1, :] if reverse else parts[r - 1][SUBLANES - 1:SUBLANES, :]
        parts[r] = parts[r] + prev
    b = jnp.concatenate(parts, axis=0)
    bg = b.reshape(grouped)

    q_bf = q.astype(jnp.bfloat16)
    kk_bf = kk.astype(jnp.bfloat16)

    def rows(x, r):
        return x[r * SUBLANES:(r + 1) * SUBLANES, :]

    a_parts = [None] * groups

    own = bg
    nbr = _shift_up(bg, 1) if reverse else _shift_down(bg, 1)
    for j in range(1, HGRN_LEVELS + 1):
        h = 1 << (j - 1)
        if 2 * h <= SUBLANES:
            upper = (sub & h) != 0
            ref = jnp.where(upper, own, nbr) if reverse else jnp.where(upper, nbr, own)
            w = jnp.exp2(_neg_abs(bg - ref)).reshape(C, HEAD_W).astype(jnp.bfloat16)
            if 4 * h <= SUBLANES:
                if reverse:
                    own = jnp.where(upper, _shift_down(own, h), own)
                    nbr = jnp.where(upper, nbr, _shift_up(nbr, h))
                else:
                    own = jnp.where(upper, own, _shift_up(own, h))
                    nbr = jnp.where(upper, _shift_down(nbr, h), nbr)
            pm = lax.dot_general(q_bf * w, kk_bf * w, _NT, preferred_element_type=jnp.float32)
            a_parts = [jnp.where(rows(masks[j], r), rows(pm, r), 0.0 if a_parts[r] is None else a_parts[r])
                       for r in range(groups)]
        else:
            n_blk, half_groups = C // (2 * h), h // SUBLANES
            blocked = (n_blk, 2 * h, HEAD_W)
            b_blk = b.reshape(blocked)
            ref = b_blk[:, h:h + 1, :] if reverse else b_blk[:, h - 1:h, :]
            w_blk = jnp.exp2(_neg_abs(b_blk - ref))
            q_half, k_half = (slice(0, h), slice(h, 2 * h)) if reverse else (slice(h, 2 * h), slice(0, h))
            q_t = (q.reshape(blocked)[:, q_half] * w_blk[:, q_half]).reshape(C // 2, HEAD_W)
            k_t = kk.reshape(blocked)[:, k_half] * w_blk[:, k_half]
            zeros = jnp.zeros_like(k_t)
            k_t = jnp.concatenate([zeros, k_t] if reverse else [k_t, zeros], axis=1).reshape(C, HEAD_W)
            pm = lax.dot_general(q_t.astype(jnp.bfloat16), k_t.astype(jnp.bfloat16), _NT,
                                 preferred_element_type=jnp.float32)
            for blk in range(n_blk):
                for i in range(half_groups):
                    r = blk * 2 * half_groups + (0 if reverse else half_groups) + i
                    piece = rows(pm, blk * half_groups + i)
                    if n_blk == 1:
                        a_parts[r] = a_parts[r] + piece
                    else:
                        a_parts[r] = jnp.where(rows(masks[j], r), piece, a_parts[r])
    a = jnp.concatenate(a_parts, axis=0)

    o = jnp.dot(a.astype(jnp.bfloat16), v_bf, preferred_element_type=jnp.float32)
    o = o + lax.dot_general(q_bf * jnp.exp2(b).astype(jnp.bfloat16), st.astype(jnp.bfloat16), _NT,
                            preferred_element_type=jnp.float32)
    o = o + jnp.sum(q * kk, axis=-1, keepdims=True) * v_bf.astype(jnp.float32)
    b_tot = b[0:1, :] if reverse else b[C - 1:C, :]
    k_hat = kk_bf * jnp.exp2(b_tot - b).astype(jnp.bfloat16)
    st_new = st * jnp.exp2(b_tot) + lax.dot_general(v_bf, k_hat, _TN, preferred_element_type=jnp.float32)
    return o, st_new


def _hgrn_kernel(n_sub, has_s0, qf_ref, vf_ref, zf_ref, qb_ref, vb_ref, zb_ref, lbp_ref, lvl_ref, *refs):
    s0_ref, (of_ref, ob_ref, sfin_ref, st_ref) = (refs[0], refs[1:]) if has_s0 else (None, refs)
    i = pl.program_id(1)
    C = HGRN_CHUNK

    @pl.when(i == 0)
    def _():
        st_ref[...] = jnp.zeros_like(st_ref) if s0_ref is None else s0_ref[0]

    dirs = ((qf_ref, vf_ref, zf_ref, of_ref), (qb_ref, vb_ref, zb_ref, ob_ref))
    masks, lbs = [], []
    for d in range(2):
        lvl = lvl_ref[d]
        masks.append([lvl == j for j in range(HGRN_LEVELS + 1)])
        p0, p1 = lbp_ref[d, 0:1, :], lbp_ref[d, 1:2, :]
        pm = jnp.maximum(p0, p1)
        e0 = jnp.exp(p0 - pm)
        lbs.append(e0 / (e0 + jnp.exp(p1 - pm)))
    states = [[st_ref[d, hh] for hh in range(HEADS)] for d in range(2)]
    for sub in range(n_sub):
        for d, (q_ref, v_ref, z_ref, o_ref) in enumerate(dirs):
            c = n_sub - 1 - sub if d == 1 else sub
            tok = slice(c * C, (c + 1) * C)
            for hh in range(HEADS):
                cols = slice(hh * HEAD_W, (hh + 1) * HEAD_W)
                o, states[d][hh] = _hgrn_chain(q_ref[0, tok, cols], z_ref[0, tok, cols], v_ref[0, tok, cols],
                                               lbs[d][:, cols], states[d][hh], masks[d], reverse=(d == 1))
                o_ref[0, tok, cols] = o.astype(o_ref.dtype)
    for d in range(2):
        for hh in range(HEADS):
            st_ref[d, hh] = states[d][hh]

    @pl.when(i == pl.num_programs(1) - 1)
    def _():
        sfin_ref[0] = st_ref[...]


def _hgrn2_bidir(hq, hi, hff, hfb, lb_param, s0, n_sub):
    B, T, _ = hq.shape
    rows = HGRN_CHUNK * n_sub
    n = T // rows
    assert n * rows == T
    fwd = pl.BlockSpec((1, rows, BRANCH_W), lambda b, i: (b, i, 0))
    bwd = pl.BlockSpec((1, rows, BRANCH_W), lambda b, i: (b, n - 1 - i, 0))
    st_spec = pl.BlockSpec((1, 2, HEADS, HEAD_W, HEAD_W), lambda b, i: (b, 0, 0, 0, 0))
    lvl = jnp.asarray(_level_maps())
    return pl.pallas_call(
        functools.partial(_hgrn_kernel, n_sub, s0 is not None),
        grid=(B, n),
        in_specs=[fwd, fwd, fwd, bwd, bwd, bwd,
                  pl.BlockSpec(lb_param.shape, lambda b, i: (0, 0, 0)),
                  pl.BlockSpec(lvl.shape, lambda b, i: (0, 0, 0))] + ([] if s0 is None else [st_spec]),
        out_specs=[fwd, bwd, st_spec],
        out_shape=[jax.ShapeDtypeStruct((B, T, BRANCH_W), jnp.bfloat16),
                   jax.ShapeDtypeStruct((B, T, BRANCH_W), jnp.bfloat16),
                   jax.ShapeDtypeStruct((B, 2, HEADS, HEAD_W, HEAD_W), jnp.float32)],
        scratch_shapes=[pltpu.VMEM((2, HEADS, HEAD_W, HEAD_W), jnp.float32)],
        compiler_params=pltpu.CompilerParams(
            dimension_semantics=("parallel", "arbitrary"), vmem_limit_bytes=V7X_VMEM_LIMIT_BYTES),
        name="hgrn2_bidir",
    )(hq, hi, hff, hq, hi, hfb, lb_param, lvl, *([] if s0 is None else [s0]))


def _merge_kernel(att_ref, ag_ref, of_ref, ob_ref, hg_ref, x_ref, gate_ref, w_ref, hgain_ref, lng_ref, lnb_ref,
                  o_ref):
    f32 = jnp.float32
    att = att_ref[0].astype(f32) * _silu(ag_ref[0].astype(f32))
    o = of_ref[0].astype(f32) + ob_ref[0].astype(f32)
    parts = []
    for hh in range(HEADS):
        oh = o[:, hh * HEAD_W:(hh + 1) * HEAD_W]
        parts.append(oh * lax.rsqrt(jnp.mean(oh * oh, axis=-1, keepdims=True) + EPS) * hgain_ref[...])
    hg = jnp.concatenate(parts, axis=1) * _silu(hg_ref[0].astype(f32))
    y_in = jnp.concatenate([att, hg], axis=1).astype(jnp.bfloat16)
    y = jnp.dot(y_in, w_ref[...], preferred_element_type=jnp.float32)
    u = ALPHA * x_ref[0] + gate_ref[0] * y
    mu = jnp.mean(u, axis=-1, keepdims=True)
    uc = u - mu
    var = jnp.mean(uc * uc, axis=-1, keepdims=True)
    o_ref[0] = uc * lax.rsqrt(var + EPS) * lng_ref[...] + lnb_ref[...]


def _merge(att, ag, o_f, o_b, hg, x, gate, w_out_bf, hgain, ln_g, ln_b, tm):
    B, N, D = x.shape
    half = pl.BlockSpec((1, tm, BRANCH_W), lambda b, i: (b, i, 0))
    full = pl.BlockSpec((1, tm, D), lambda b, i: (b, i, 0))
    row = lambda w: pl.BlockSpec((1, w), lambda b, i: (0, 0))
    return pl.pallas_call(
        _merge_kernel,
        grid=(B, N // tm),
        in_specs=[half, half, half, half, half, full,
                  pl.BlockSpec((1, 1, D), lambda b, i: (b, 0, 0)),
                  pl.BlockSpec(w_out_bf.shape, lambda b, i: (0, 0)),
                  row(HEAD_W), row(D), row(D)],
        out_specs=full,
        out_shape=jax.ShapeDtypeStruct((B, N, D), jnp.float32),
        compiler_params=pltpu.CompilerParams(
            dimension_semantics=("parallel", "parallel"), vmem_limit_bytes=V7X_VMEM_LIMIT_BYTES),
        name="merge_out_proj_ln",
    )(att, ag, o_f, o_b, hg, x, gate, w_out_bf, hgain, ln_g, ln_b)


def kernel(x, c, ctx, c_ctx, w_ada, b_ada, w_in, w_out, diff_lambda, diff_subln_gain, hgrn_lower_bound,
           hgrn_norm_gain, ln_gain, ln_bias):
    B, N, D = x.shape
    assert DEPTH == 1 and w_ada.shape[0] == 1
    cvec = jnp.concatenate([c, c_ctx[None, :], jnp.zeros((SUBLANES - B - 1, D), c.dtype)], axis=0)
    mod = _modulation(cvec, w_ada, b_ada)
    shift, scale, gate = mod[:, :D], mod[:, D:2 * D], mod[:, 2 * D:]
    s1p = (1.0 + scale)[:, None, :]
    shift = shift[:, None, :]
    ctx_rows = jnp.full((B,), B, jnp.int32)

    k_c, v_c, hq_c, hi_c, hff_c, hfb_c = _in_projection(
        ctx, s1p[ctx_rows], shift[ctx_rows], w_in, _CTX_GROUPS, tm=ctx.shape[1])
    q, k, v, ag, hq, hi, hff, hfb, hg = _in_projection(
        x, s1p[:B], shift[:B], w_in, _LATENT_GROUPS, tm=ROW_TILE, rope_tables=_rope_tables(N))

    att = _diff_attention(q, k_c, v_c, k, v, diff_lambda, diff_subln_gain)

    _, _, s_ctx = _hgrn2_bidir(hq_c, hi_c, hff_c, hfb_c, hgrn_lower_bound, None, n_sub=ctx.shape[1] // HGRN_CHUNK)
    o_f, o_b, _ = _hgrn2_bidir(hq, hi, hff, hfb, hgrn_lower_bound, s_ctx, n_sub=HGRN_CHUNKS_PER_STEP)

    return _merge(att, ag, o_f, o_b, hg, x, gate[:B, None, :], w_out[0].astype(jnp.bfloat16),
                  hgrn_norm_gain, ln_gain, ln_bias, tm=ROW_TILE)
```

```python
import functools
import math

import numpy as np
import jax
import jax.numpy as jnp
from jax import lax
from jax.experimental import pallas as pl
from jax.experimental.pallas import tpu as pltpu

DEPTH = 1
GRID_W = 64
HEADS = 4
HEAD_W = 128
MAP_D = 64
BRANCH_W = HEADS * HEAD_W
ROPE_BASE = 10000.0
ROPE_FREQS = MAP_D // 4
EPS = 1e-5
LAMBDA_INIT = 0.8 - 0.6 * math.exp(-0.3 * 0)
ALPHA = (2.0 * DEPTH) ** 0.25

V7X_VMEM_LIMIT_BYTES = 56 * 1024 * 1024
SUBLANES = 8
LANES = 128
SUM_ROWS = 16
ROW_TILE = 512
ATTN_TQ = 2048
ATTN_TK = 1280
HGRN_CHUNKS_PER_STEP = 4
HGRN_CHUNK = 128
HGRN_LEVELS = 7

_NT = (((1,), (1,)), ((), ()))
_TN = (((0,), (0,)), ((), ()))


def _silu(x):
    return x * jax.nn.sigmoid(x)


def _mod_kernel(c_ref, w_ref, b_ref, o_ref):
    a = _silu(c_ref[...]).astype(jnp.bfloat16)
    o_ref[...] = jnp.dot(a, w_ref[0].astype(jnp.bfloat16), preferred_element_type=jnp.float32) + b_ref[...]


def _modulation(cvec, w_ada, b_ada):
    rows, d = cvec.shape
    n_out = w_ada.shape[2]
    bn = 1024
    return pl.pallas_call(
        _mod_kernel,
        grid=(n_out // bn,),
        in_specs=[pl.BlockSpec((rows, d), lambda j: (0, 0)),
                  pl.BlockSpec((1, d, bn), lambda j: (0, 0, j)),
                  pl.BlockSpec((1, bn), lambda j: (0, j))],
        out_specs=pl.BlockSpec((rows, bn), lambda j: (0, j)),
        out_shape=jax.ShapeDtypeStruct((rows, n_out), jnp.float32),
        compiler_params=pltpu.CompilerParams(vmem_limit_bytes=V7X_VMEM_LIMIT_BYTES),
        name="modulation",
    )(cvec, w_ada, b_ada)


_LATENT_GROUPS = ((0, "rope_q", jnp.bfloat16), (1, "rope", jnp.bfloat16), (2, "transposed", jnp.bfloat16),
                  (3, "plain", jnp.bfloat16), (4, "silu", jnp.float32), (5, "plain", jnp.bfloat16),
                  (6, "plain", jnp.float32), (7, "plain", jnp.float32), (8, "plain", jnp.bfloat16))
_CTX_GROUPS = ((1, "plain", jnp.bfloat16), (2, "transposed", jnp.bfloat16), (4, "silu", jnp.float32),
               (5, "plain", jnp.bfloat16), (6, "plain", jnp.float32), (7, "plain", jnp.float32))
Q_SCALE = math.log2(math.e) / math.sqrt(MAP_D)


def _proj_kernel(groups, use_rope, *refs):
    if use_rope:
        x_ref, s1p_ref, sh_ref, w_ref, cos_row_ref, sin_row_ref, cos_col_ref, sin_col_ref = refs[:8]
        out_refs = refs[8:]
    else:
        x_ref, s1p_ref, sh_ref, w_ref = refs[:4]
        out_refs = refs[4:]
    h = (x_ref[0] * s1p_ref[0] + sh_ref[0]).astype(jnp.bfloat16)
    if use_rope:
        by_row = (lax.broadcasted_iota(jnp.int32, (GRID_W, HEAD_W), 1) & (MAP_D - 1)) < 2 * ROPE_FREQS

        def token_table(row_ref, col_ref):
            grid_rows = [jnp.where(by_row, row_ref[r:r + 1, :], col_ref[...]) for r in range(row_ref.shape[0])]
            return jnp.concatenate([jnp.concatenate(grid_rows, axis=0)] * HEADS, axis=1)

        cos = token_table(cos_row_ref, cos_col_ref)
        sin = token_table(sin_row_ref, sin_col_ref)
        lane = lax.broadcasted_iota(jnp.int32, cos.shape, 1)
        first_half = (lane & (2 * ROPE_FREQS - 1)) < ROPE_FREQS
    for (g, kind, dt), o_ref in zip(groups, out_refs):
        p = jnp.dot(h, w_ref[0, :, g * BRANCH_W:(g + 1) * BRANCH_W].astype(jnp.bfloat16),
                    preferred_element_type=jnp.float32)
        if kind in ("rope", "rope_q"):
            partner = jnp.where(first_half,
                                pltpu.roll(p, BRANCH_W - ROPE_FREQS, axis=1),
                                pltpu.roll(p, ROPE_FREQS, axis=1))
            p = p * cos + partner * sin
            if kind == "rope_q":
                p = p * Q_SCALE
        elif kind == "silu":
            p = _silu(p)
        elif kind == "transposed":
            p = p.T
        o_ref[0] = p.astype(dt)


def _in_projection(x, s1p, sh, w_in, groups, tm, rope_tables=None):
    B, T, D = x.shape
    use_rope = rope_tables is not None
    in_specs = [pl.BlockSpec((1, tm, D), lambda b, i: (b, i, 0)),
                pl.BlockSpec((1, 1, D), lambda b, i: (b, 0, 0)),
                pl.BlockSpec((1, 1, D), lambda b, i: (b, 0, 0)),
                pl.BlockSpec(w_in.shape, lambda b, i: (0, 0, 0), pipeline_mode=pl.Buffered(1))]
    args = [x, s1p, sh, w_in]
    if use_rope:
        assert tm % (GRID_W * SUBLANES) == 0
        in_specs += [pl.BlockSpec((tm // GRID_W, HEAD_W), lambda b, i: (i, 0))] * 2
        in_specs += [pl.BlockSpec((GRID_W, HEAD_W), lambda b, i: (0, 0))] * 2
        args += list(rope_tables)
    out_specs = [pl.BlockSpec((1, BRANCH_W, tm), lambda b, i: (b, 0, i)) if kind == "transposed"
                 else pl.BlockSpec((1, tm, BRANCH_W), lambda b, i: (b, i, 0)) for _, kind, _ in groups]
    out_shape = [jax.ShapeDtypeStruct((B, BRANCH_W, T) if kind == "transposed" else (B, T, BRANCH_W), dt)
                 for _, kind, dt in groups]
    return pl.pallas_call(
        functools.partial(_proj_kernel, groups, use_rope),
        grid=(B, T // tm),
        in_specs=in_specs, out_specs=out_specs, out_shape=out_shape,
        compiler_params=pltpu.CompilerParams(
            dimension_semantics=("parallel", "parallel"), vmem_limit_bytes=V7X_VMEM_LIMIT_BYTES),
        name="in_proj_rope" if use_rope else "in_proj_ctx",
    )(*args)


def _rope_tables(n_tokens):
    inv_freq = jnp.asarray(ROPE_BASE, jnp.float32) ** (-jnp.arange(ROPE_FREQS, dtype=jnp.float32) / ROPE_FREQS)
    freq = jnp.tile(inv_freq, HEAD_W // ROPE_FREQS)
    sign = np.where((np.arange(HEAD_W) % (2 * ROPE_FREQS)) < ROPE_FREQS, -1.0, 1.0).astype(np.float32)
    row_ang = jnp.arange(n_tokens // GRID_W, dtype=jnp.float32)[:, None] * freq
    col_ang = jnp.arange(GRID_W, dtype=jnp.float32)[:, None] * freq
    return jnp.cos(row_ang), jnp.sin(row_ang) * sign, jnp.cos(col_ang), jnp.sin(col_ang) * sign


def _attn_kernel(tk, q_ref, kc_ref, vtc_ref, kl_ref, vtl_ref, lam_ref, gain_ref, o_ref,
                 k_ref, vt_ref, s_ref, p_ref, acc_ref):
    t_ctx = kc_ref.shape[1]

    @pl.when(pl.program_id(2) == 0)
    def _():
        k_ref[:t_ctx, :] = kc_ref[0]
        k_ref[t_ctx:, :] = kl_ref[0]
        vt_ref[:HEAD_W, :t_ctx] = vtc_ref[0]
        vt_ref[:HEAD_W, t_ctx:] = vtl_ref[0]
        extra = lax.broadcasted_iota(jnp.int32, (SUM_ROWS, vt_ref.shape[1]), 0) == 0
        vt_ref[HEAD_W:, :] = jnp.where(extra, 1.0, 0.0).astype(vt_ref.dtype)

    qt = q_ref[0].astype(jnp.float32).T
    row = lax.broadcasted_iota(jnp.int32, qt.shape, 0)
    qm = tuple(jnp.where(sel, qt, 0.0).astype(jnp.bfloat16) for sel in (row < MAP_D, row >= MAP_D))
    tq = qt.shape[1]
    n_keys = k_ref.shape[0]
    n_main, tail = n_keys // tk, n_keys % tk

    def scores(start, size):
        kb = k_ref[pl.ds(start, size), :]
        smax = []
        for m in range(2):
            s = jnp.dot(kb, qm[m], preferred_element_type=jnp.float32)
            s_ref[m, :size, :] = s
            smax.append(jnp.max(s, axis=0, keepdims=True))
        return tuple(smax)

    def weights(size, smax, mx):
        new_mx, corrs = [], []
        for m in range(2):
            mx_new = jnp.maximum(mx[m], smax[m])
            p_ref[m, :size, :] = jnp.exp2(s_ref[m, :size, :] - mx_new).astype(jnp.bfloat16)
            new_mx.append(mx_new)
            corrs.append(jnp.exp2(mx[m] - mx_new))
        return tuple(new_mx), tuple(corrs)

    def values(start, size, corr):
        vtb = vt_ref[:, pl.ds(start, size)]
        for m in range(2):
            acc_ref[m] = acc_ref[m] * corr[m] + jnp.dot(vtb, p_ref[m, :size, :], preferred_element_type=jnp.float32)

    def at(j):
        return pl.multiple_of(j * tk, LANES)

    mx = tuple(jnp.full((1, tq), -jnp.inf, jnp.float32) for _ in range(2))
    acc_ref[...] = jnp.zeros_like(acc_ref)
    mx, corr = weights(tk, scores(0, tk), mx)
    smax = scores(tk, tk)

    def body(j, carry):
        mx, smax, corr_prev = carry
        values(at(j - 1), tk, corr_prev)
        mx, corr = weights(tk, smax, mx)
        return mx, scores(at(j + 1), tk), corr

    mx, smax, corr = lax.fori_loop(1, n_main - 1, body, (mx, smax, corr))
    values((n_main - 2) * tk, tk, corr)
    mx, corr = weights(tk, smax, mx)
    if tail:
        smax = scores(n_main * tk, tail)
    values((n_main - 1) * tk, tk, corr)
    if tail:
        mx, corr = weights(tail, smax, mx)
        values(n_main * tk, tail, corr)

    lp = lam_ref[0]
    lam = (jnp.exp(jnp.sum(lp[0:1] * lp[1:2], axis=-1, keepdims=True))
           - jnp.exp(jnp.sum(lp[2:3] * lp[3:4], axis=-1, keepdims=True)) + LAMBDA_INIT)
    l0, l1 = acc_ref[0, HEAD_W:HEAD_W + 1, :], acc_ref[1, HEAD_W:HEAD_W + 1, :]
    ot = acc_ref[0, :HEAD_W, :] / l0 - lam * (acc_ref[1, :HEAD_W, :] / l1)
    ot = ot * lax.rsqrt(jnp.mean(ot * ot, axis=0, keepdims=True) + EPS)
    o_ref[0] = (ot.T * gain_ref[...] * (1.0 - LAMBDA_INIT)).astype(o_ref.dtype)


def _diff_attention(q, k_ctx, vt_ctx, k, vt, diff_lambda, gain):
    B, N, _ = q.shape
    T_ctx = k_ctx.shape[1]
    Tk = T_ctx + N
    tq, tk = ATTN_TQ, ATTN_TK
    assert N % tq == 0 and Tk // tk >= 3 and T_ctx % LANES == 0 and (Tk % tk) % LANES == 0
    return pl.pallas_call(
        functools.partial(_attn_kernel, tk),
        grid=(B, HEADS, N // tq),
        in_specs=[pl.BlockSpec((1, tq, HEAD_W), lambda b, h, i: (b, i, h)),
                  pl.BlockSpec((1, T_ctx, HEAD_W), lambda b, h, i: (b, 0, h)),
                  pl.BlockSpec((1, HEAD_W, T_ctx), lambda b, h, i: (b, h, 0)),
                  pl.BlockSpec((1, N, HEAD_W), lambda b, h, i: (b, 0, h), pipeline_mode=pl.Buffered(1)),
                  pl.BlockSpec((1, HEAD_W, N), lambda b, h, i: (b, h, 0), pipeline_mode=pl.Buffered(1)),
                  pl.BlockSpec((1, 4, MAP_D), lambda b, h, i: (0, 0, 0)),
                  pl.BlockSpec((1, HEAD_W), lambda b, h, i: (0, 0))],
        out_specs=pl.BlockSpec((1, tq, HEAD_W), lambda b, h, i: (b, i, h)),
        out_shape=jax.ShapeDtypeStruct((B, N, BRANCH_W), jnp.bfloat16),
        scratch_shapes=[pltpu.VMEM((Tk, HEAD_W), jnp.bfloat16), pltpu.VMEM((HEAD_W + SUM_ROWS, Tk), jnp.bfloat16),
                        pltpu.VMEM((2, tk, tq), jnp.float32), pltpu.VMEM((2, tk, tq), jnp.bfloat16),
                        pltpu.VMEM((2, HEAD_W + SUM_ROWS, tq), jnp.float32)],
        compiler_params=pltpu.CompilerParams(
            dimension_semantics=("parallel", "parallel", "arbitrary"),
            vmem_limit_bytes=V7X_VMEM_LIMIT_BYTES),
        name="diff_attention",
    )(q, k_ctx, vt_ctx, k, vt, diff_lambda, gain)


def _level_maps():
    t = np.arange(HGRN_CHUNK)[:, None]
    s = np.arange(HGRN_CHUNK)[None, :]
    x = t ^ s
    lvl = np.where(x > 0, np.floor(np.log2(np.maximum(x, 1))).astype(np.int32) + 1, 0)
    lvl = np.where(s > t, -1, lvl).astype(np.int32)
    return np.stack([lvl, lvl.T])


def _shift_down(x, s):
    return pltpu.roll(x, s, axis=1)


def _shift_up(x, s):
    return pltpu.roll(x, SUBLANES - s, axis=1)


def _neg_abs(x):
    bits = lax.bitcast_convert_type(x, jnp.uint32) | jnp.uint32(0x80000000)
    return lax.bitcast_convert_type(bits, jnp.float32)


def _hgrn_chain(q, z, v_bf, lb, st, masks, reverse):
    C = HGRN_CHUNK
    groups = C // SUBLANES
    sig = jax.nn.sigmoid(z)
    g = jnp.log2(lb + (1.0 - lb) * sig)
    kk = (1.0 - lb) * (1.0 - sig)
    grouped = (groups, SUBLANES, HEAD_W)
    sub = lax.broadcasted_iota(jnp.int32, grouped, 1)

    bg = g.reshape(grouped)
    for s in (1, 2, 4):
        if reverse:
            bg = bg + jnp.where(sub < SUBLANES - s, _shift_up(bg, s), 0.0)
        else:
            bg = bg + jnp.where(sub >= s, _shift_down(bg, s), 0.0)
    parts = [bg[r] for r in range(groups)]
    order = range(groups - 2, -1, -1) if reverse else range(1, groups)
    for r in order:
        prev = parts[r + 1][0:1, :] if reverse else parts[r - 1][SUBLANES - 1:SUBLANES, :]
        parts[r] = parts[r] + prev
    b = jnp.concatenate(parts, axis=0)
    bg = b.reshape(grouped)

    q_bf = q.astype(jnp.bfloat16)
    kk_bf = kk.astype(jnp.bfloat16)

    def rows(x, r):
        return x[r * SUBLANES:(r + 1) * SUBLANES, :]

    a_parts = [None] * groups

    own = bg
    nbr = _shift_up(bg, 1) if reverse else _shift_down(bg, 1)
    for j in range(1, HGRN_LEVELS + 1):
        h = 1 << (j - 1)
        if 2 * h <= SUBLANES:
            upper = (sub & h) != 0
            ref = jnp.where(upper, own, nbr) if reverse else jnp.where(upper, nbr, own)
            w = jnp.exp2(_neg_abs(bg - ref)).reshape(C, HEAD_W).astype(jnp.bfloat16)
            if 4 * h <= SUBLANES:
                if reverse:
                    own = jnp.where(upper, _shift_down(own, h), own)
                    nbr = jnp.where(upper, nbr, _shift_up(nbr, h))
                else:
                    own = jnp.where(upper, own, _shift_up(own, h))
                    nbr = jnp.where(upper, _shift_down(nbr, h), nbr)
            pm = lax.dot_general(q_bf * w, kk_bf * w, _NT, preferred_element_type=jnp.float32)
            a_parts = [jnp.where(rows(masks[j], r), rows(pm, r), 0.0 if a_parts[r] is None else a_parts[r])
                       for r in range(groups)]
        else:
            n_blk, half_groups = C // (2 * h), h // SUBLANES
            blocked = (n_blk, 2 * h, HEAD_W)
            b_blk = b.reshape(blocked)
            ref = b_blk[:, h:h + 1, :] if reverse else b_blk[:, h - 1:h, :]
            w_blk = jnp.exp2(_neg_abs(b_blk - ref))
            q_half, k_half = (slice(0, h), slice(h, 2 * h)) if reverse else (slice(h, 2 * h), slice(0, h))
            q_t = (q.reshape(blocked)[:, q_half] * w_blk[:, q_half]).reshape(C // 2, HEAD_W)
            k_t = kk.reshape(blocked)[:, k_half] * w_blk[:, k_half]
            zeros = jnp.zeros_like(k_t)
            k_t = jnp.concatenate([zeros, k_t] if reverse else [k_t, zeros], axis=1).reshape(C, HEAD_W)
            pm = lax.dot_general(q_t.astype(jnp.bfloat16), k_t.astype(jnp.bfloat16), _NT,
                                 preferred_element_type=jnp.float32)
            for blk in range(n_blk):
                for i in range(half_groups):
                    r = blk * 2 * half_groups + (0 if reverse else half_groups) + i
                    piece = rows(pm, blk * half_groups + i)
                    if n_blk == 1:
                        a_parts[r] = a_parts[r] + piece
                    else:
                        a_parts[r] = jnp.where(rows(masks[j], r), piece, a_parts[r])
    a = jnp.concatenate(a_parts, axis=0)

    o = jnp.dot(a.astype(jnp.bfloat16), v_bf, preferred_element_type=jnp.float32)
    o = o + lax.dot_general(q_bf * jnp.exp2(b).astype(jnp.bfloat16), st.astype(jnp.bfloat16), _NT,
                            preferred_element_type=jnp.float32)
    o = o + jnp.sum(q * kk, axis=-1, keepdims=True) * v_bf.astype(jnp.float32)
    b_tot = b[0:1, :] if reverse else b[C - 1:C, :]
    k_hat = kk_bf * jnp.exp2(b_tot - b).astype(jnp.bfloat16)
    st_new = st * jnp.exp2(b_tot) + lax.dot_general(v_bf, k_hat, _TN, preferred_element_type=jnp.float32)
    return o, st_new


def _hgrn_kernel(n_sub, has_s0, qf_ref, vf_ref, zf_ref, qb_ref, vb_ref, zb_ref, lbp_ref, lvl_ref, *refs):
    s0_ref, (of_ref, ob_ref, sfin_ref, st_ref) = (refs[0], refs[1:]) if has_s0 else (None, refs)
    i = pl.program_id(1)
    C = HGRN_CHUNK

    @pl.when(i == 0)
    def _():
        st_ref[...] = jnp.zeros_like(st_ref) if s0_ref is None else s0_ref[0]

    dirs = ((qf_ref, vf_ref, zf_ref, of_ref), (qb_ref, vb_ref, zb_ref, ob_ref))
    masks, lbs = [], []
    for d in range(2):
        lvl = lvl_ref[d]
        masks.append([lvl == j for j in range(HGRN_LEVELS + 1)])
        p0, p1 = lbp_ref[d, 0:1, :], lbp_ref[d, 1:2, :]
        pm = jnp.maximum(p0, p1)
        e0 = jnp.exp(p0 - pm)
        lbs.append(e0 / (e0 + jnp.exp(p1 - pm)))
    states = [[st_ref[d, hh] for hh in range(HEADS)] for d in range(2)]
    for sub in range(n_sub):
        for d, (q_ref, v_ref, z_ref, o_ref) in enumerate(dirs):
            c = n_sub - 1 - sub if d == 1 else sub
            tok = slice(c * C, (c + 1) * C)
            for hh in range(HEADS):
                cols = slice(hh * HEAD_W, (hh + 1) * HEAD_W)
                o, states[d][hh] = _hgrn_chain(q_ref[0, tok, cols], z_ref[0, tok, cols], v_ref[0, tok, cols],
                                               lbs[d][:, cols], states[d][hh], masks[d], reverse=(d == 1))
                o_ref[0, tok, cols] = o.astype(o_ref.dtype)
    for d in range(2):
        for hh in range(HEADS):
            st_ref[d, hh] = states[d][hh]

    @pl.when(i == pl.num_programs(1) - 1)
    def _():
        sfin_ref[0] = st_ref[...]


def _hgrn2_bidir(hq, hi, hff, hfb, lb_param, s0, n_sub):
    B, T, _ = hq.shape
    rows = HGRN_CHUNK * n_sub
    n = T // rows
    assert n * rows == T
    fwd = pl.BlockSpec((1, rows, BRANCH_W), lambda b, i: (b, i, 0))
    bwd = pl.BlockSpec((1, rows, BRANCH_W), lambda b, i: (b, n - 1 - i, 0))
    st_spec = pl.BlockSpec((1, 2, HEADS, HEAD_W, HEAD_W), lambda b, i: (b, 0, 0, 0, 0))
    lvl = jnp.asarray(_level_maps())
    return pl.pallas_call(
        functools.partial(_hgrn_kernel, n_sub, s0 is not None),
        grid=(B, n),
        in_specs=[fwd, fwd, fwd, bwd, bwd, bwd,
                  pl.BlockSpec(lb_param.shape, lambda b, i: (0, 0, 0)),
                  pl.BlockSpec(lvl.shape, lambda b, i: (0, 0, 0))] + ([] if s0 is None else [st_spec]),
        out_specs=[fwd, bwd, st_spec],
        out_shape=[jax.ShapeDtypeStruct((B, T, BRANCH_W), jnp.bfloat16),
                   jax.ShapeDtypeStruct((B, T, BRANCH_W), jnp.bfloat16),
                   jax.ShapeDtypeStruct((B, 2, HEADS, HEAD_W, HEAD_W), jnp.float32)],
        scratch_shapes=[pltpu.VMEM((2, HEADS, HEAD_W, HEAD_W), jnp.float32)],
        compiler_params=pltpu.CompilerParams(
            dimension_semantics=("parallel", "arbitrary"), vmem_limit_bytes=V7X_VMEM_LIMIT_BYTES),
        name="hgrn2_bidir",
    )(hq, hi, hff, hq, hi, hfb, lb_param, lvl, *([] if s0 is None else [s0]))


def _merge_kernel(att_ref, ag_ref, of_ref, ob_ref, hg_ref, x_ref, gate_ref, w_ref, hgain_ref, lng_ref, lnb_ref,
                  o_ref):
    f32 = jnp.float32
    att = att_ref[0].astype(f32) * _silu(ag_ref[0].astype(f32))
    o = of_ref[0].astype(f32) + ob_ref[0].astype(f32)
    parts = []
    for hh in range(HEADS):
        oh = o[:, hh * HEAD_W:(hh + 1) * HEAD_W]
        parts.append(oh * lax.rsqrt(jnp.mean(oh * oh, axis=-1, keepdims=True) + EPS) * hgain_ref[...])
    hg = jnp.concatenate(parts, axis=1) * _silu(hg_ref[0].astype(f32))
    y_in = jnp.concatenate([att, hg], axis=1).astype(jnp.bfloat16)
    y = jnp.dot(y_in, w_ref[...], preferred_element_type=jnp.float32)
    u = ALPHA * x_ref[0] + gate_ref[0] * y
    mu = jnp.mean(u, axis=-1, keepdims=True)
    uc = u - mu
    var = jnp.mean(uc * uc, axis=-1, keepdims=True)
    o_ref[0] = uc * lax.rsqrt(var + EPS) * lng_ref[...] + lnb_ref[...]


def _merge(att, ag, o_f, o_b, hg, x, gate, w_out_bf, hgain, ln_g, ln_b, tm):
    B, N, D = x.shape
    half = pl.BlockSpec((1, tm, BRANCH_W), lambda b, i: (b, i, 0))
    full = pl.BlockSpec((1, tm, D), lambda b, i: (b, i, 0))
    row = lambda w: pl.BlockSpec((1, w), lambda b, i: (0, 0))
    return pl.pallas_call(
        _merge_kernel,
        grid=(B, N // tm),
        in_specs=[half, half, half, half, half, full,
                  pl.BlockSpec((1, 1, D), lambda b, i: (b, 0, 0)),
                  pl.BlockSpec(w_out_bf.shape, lambda b, i: (0, 0)),
                  row(HEAD_W), row(D), row(D)],
        out_specs=full,
        out_shape=jax.ShapeDtypeStruct((B, N, D), jnp.float32),
        compiler_params=pltpu.CompilerParams(
            dimension_semantics=("parallel", "parallel"), vmem_limit_bytes=V7X_VMEM_LIMIT_BYTES),
        name="merge_out_proj_ln",
    )(att, ag, o_f, o_b, hg, x, gate, w_out_bf, hgain, ln_g, ln_b)


def kernel(x, c, ctx, c_ctx, w_ada, b_ada, w_in, w_out, diff_lambda, diff_subln_gain, hgrn_lower_bound,
           hgrn_norm_gain, ln_gain, ln_bias):
    B, N, D = x.shape
    assert DEPTH == 1 and w_ada.shape[0] == 1
    cvec = jnp.concatenate([c, c_ctx[None, :], jnp.zeros((SUBLANES - B - 1, D), c.dtype)], axis=0)
    mod = _modulation(cvec, w_ada, b_ada)
    shift, scale, gate = mod[:, :D], mod[:, D:2 * D], mod[:, 2 * D:]
    s1p = (1.0 + scale)[:, None, :]
    shift = shift[:, None, :]
    ctx_rows = jnp.full((B,), B, jnp.int32)

    k_c, v_c, hq_c, hi_c, hff_c, hfb_c = _in_projection(
        ctx, s1p[ctx_rows], shift[ctx_rows], w_in, _CTX_GROUPS, tm=ctx.shape[1])
    q, k, v, ag, hq, hi, hff, hfb, hg = _in_projection(
        x, s1p[:B], shift[:B], w_in, _LATENT_GROUPS, tm=ROW_TILE, rope_tables=_rope_tables(N))

    att = _diff_attention(q, k_c, v_c, k, v, diff_lambda, diff_subln_gain)

    _, _, s_ctx = _hgrn2_bidir(hq_c, hi_c, hff_c, hfb_c, hgrn_lower_bound, None, n_sub=ctx.shape[1] // HGRN_CHUNK)
    o_f, o_b, _ = _hgrn2_bidir(hq, hi, hff, hfb, hgrn_lower_bound, s_ctx, n_sub=HGRN_CHUNKS_PER_STEP)

    return _merge(att, ag, o_f, o_b, hg, x, gate[:B, None, :], w_out[0].astype(jnp.bfloat16),
                  hgrn_norm_gain, ln_gain, ln_bias, tm=ROW_TILE)
```

```python
import functools
import math

import numpy as np
import jax
import jax.numpy as jnp
from jax import lax
from jax.experimental import pallas as pl
from jax.experimental.pallas import tpu as pltpu

DEPTH = 1
GRID_W = 64
HEADS = 4
HEAD_W = 128
MAP_D = 64
BRANCH_W = HEADS * HEAD_W
ROPE_BASE = 10000.0
ROPE_FREQS = MAP_D // 4
EPS = 1e-5
LAMBDA_INIT = 0.8 - 0.6 * math.exp(-0.3 * 0)
ALPHA = (2.0 * DEPTH) ** 0.25

V7X_VMEM_LIMIT_BYTES = 56 * 1024 * 1024
SUBLANES = 8
LANES = 128
SUM_ROWS = 16
ROW_TILE = 512
MERGE_TILE = 1024
ATTN_TQ = 2048
ATTN_MAX_TK = 1408
HGRN_CHUNKS_PER_STEP = 4
HGRN_CHUNK = 128
HGRN_LEVELS = 7

_NT = (((1,), (1,)), ((), ()))
_TN = (((0,), (0,)), ((), ()))


def _silu(x):
    return x * jax.nn.sigmoid(x)


def _mod_kernel(c_ref, w_ref, b_ref, o_ref):
    a = _silu(c_ref[...]).astype(jnp.bfloat16)
    o_ref[...] = jnp.dot(a, w_ref[0].astype(jnp.bfloat16), preferred_element_type=jnp.float32) + b_ref[...]


def _modulation(cvec, w_ada, b_ada):
    rows, d = cvec.shape
    n_out = w_ada.shape[2]
    bn = 1024
    return pl.pallas_call(
        _mod_kernel,
        grid=(n_out // bn,),
        in_specs=[pl.BlockSpec((rows, d), lambda j: (0, 0)),
                  pl.BlockSpec((1, d, bn), lambda j: (0, 0, j)),
                  pl.BlockSpec((1, bn), lambda j: (0, j))],
        out_specs=pl.BlockSpec((rows, bn), lambda j: (0, j)),
        out_shape=jax.ShapeDtypeStruct((rows, n_out), jnp.float32),
        compiler_params=pltpu.CompilerParams(vmem_limit_bytes=V7X_VMEM_LIMIT_BYTES),
        name="modulation",
    )(cvec, w_ada, b_ada)


_LATENT_GROUPS = ((0, "rope_q", jnp.bfloat16), (1, "rope", jnp.bfloat16), (2, "transposed", jnp.bfloat16),
                  (3, "plain", jnp.bfloat16), (4, "silu", jnp.float32), (5, "plain", jnp.bfloat16),
                  (6, "plain", jnp.float32), (7, "plain", jnp.float32), (8, "plain", jnp.bfloat16))
_CTX_GROUPS = ((1, "plain", jnp.bfloat16), (2, "transposed", jnp.bfloat16), (4, "silu", jnp.float32),
               (5, "plain", jnp.bfloat16), (6, "plain", jnp.float32), (7, "plain", jnp.float32))
Q_SCALE = math.log2(math.e) / math.sqrt(MAP_D)


def _proj_kernel(groups, use_rope, *refs):
    if use_rope:
        x_ref, s1p_ref, sh_ref, w_ref, cos_row_ref, sin_row_ref, cos_col_ref, sin_col_ref = refs[:8]
        out_refs = refs[8:]
    else:
        x_ref, s1p_ref, sh_ref, w_ref = refs[:4]
        out_refs = refs[4:]
    h = (x_ref[0] * s1p_ref[0] + sh_ref[0]).astype(jnp.bfloat16)
    if use_rope:
        by_row = (lax.broadcasted_iota(jnp.int32, (GRID_W, HEAD_W), 1) & (MAP_D - 1)) < 2 * ROPE_FREQS

        def token_table(row_ref, col_ref):
            grid_rows = [jnp.where(by_row, row_ref[r:r + 1, :], col_ref[...]) for r in range(row_ref.shape[0])]
            return jnp.concatenate([jnp.concatenate(grid_rows, axis=0)] * HEADS, axis=1)

        cos = token_table(cos_row_ref, cos_col_ref)
        sin = token_table(sin_row_ref, sin_col_ref)
        lane = lax.broadcasted_iota(jnp.int32, cos.shape, 1)
        first_half = (lane & (2 * ROPE_FREQS - 1)) < ROPE_FREQS
    for (g, kind, dt), o_ref in zip(groups, out_refs):
        p = jnp.dot(h, w_ref[0, :, g * BRANCH_W:(g + 1) * BRANCH_W].astype(jnp.bfloat16),
                    preferred_element_type=jnp.float32)
        if kind in ("rope", "rope_q"):
            partner = jnp.where(first_half,
                                pltpu.roll(p, BRANCH_W - ROPE_FREQS, axis=1),
                                pltpu.roll(p, ROPE_FREQS, axis=1))
            p = p * cos + partner * sin
            if kind == "rope_q":
                p = p * Q_SCALE
        elif kind == "silu":
            p = _silu(p)
        elif kind == "transposed":
            p = p.T
        o_ref[0] = p.astype(dt)


def _in_projection(x, s1p, sh, w_in, groups, tm, rope_tables=None):
    B, T, D = x.shape
    use_rope = rope_tables is not None
    in_specs = [pl.BlockSpec((1, tm, D), lambda b, i: (b, i, 0)),
                pl.BlockSpec((1, 1, D), lambda b, i: (b, 0, 0)),
                pl.BlockSpec((1, 1, D), lambda b, i: (b, 0, 0)),
                pl.BlockSpec(w_in.shape, lambda b, i: (0, 0, 0), pipeline_mode=pl.Buffered(1))]
    args = [x, s1p, sh, w_in]
    if use_rope:
        assert tm % (GRID_W * SUBLANES) == 0
        in_specs += [pl.BlockSpec((tm // GRID_W, HEAD_W), lambda b, i: (i, 0))] * 2
        in_specs += [pl.BlockSpec((GRID_W, HEAD_W), lambda b, i: (0, 0))] * 2
        args += list(rope_tables)
    out_specs = [pl.BlockSpec((1, BRANCH_W, tm), lambda b, i: (b, 0, i)) if kind == "transposed"
                 else pl.BlockSpec((1, tm, BRANCH_W), lambda b, i: (b, i, 0)) for _, kind, _ in groups]
    out_shape = [jax.ShapeDtypeStruct((B, BRANCH_W, T) if kind == "transposed" else (B, T, BRANCH_W), dt)
                 for _, kind, dt in groups]
    return pl.pallas_call(
        functools.partial(_proj_kernel, groups, use_rope),
        grid=(B, T // tm),
        in_specs=in_specs, out_specs=out_specs, out_shape=out_shape,
        compiler_params=pltpu.CompilerParams(
            dimension_semantics=("parallel", "parallel"), vmem_limit_bytes=V7X_VMEM_LIMIT_BYTES),
        name="in_proj_rope" if use_rope else "in_proj_ctx",
    )(*args)


def _rope_tables(n_tokens):
    inv_freq = jnp.asarray(ROPE_BASE, jnp.float32) ** (-jnp.arange(ROPE_FREQS, dtype=jnp.float32) / ROPE_FREQS)
    freq = jnp.tile(inv_freq, HEAD_W // ROPE_FREQS)
    sign = np.where((np.arange(HEAD_W) % (2 * ROPE_FREQS)) < ROPE_FREQS, -1.0, 1.0).astype(np.float32)
    row_ang = jnp.arange(n_tokens // GRID_W, dtype=jnp.float32)[:, None] * freq
    col_ang = jnp.arange(GRID_W, dtype=jnp.float32)[:, None] * freq
    return jnp.cos(row_ang), jnp.sin(row_ang) * sign, jnp.cos(col_ang), jnp.sin(col_ang) * sign


def _attn_kernel(tk, q_ref, kc_ref, vtc_ref, kl_ref, vtl_ref, lam_ref, gain_ref, o_ref,
                 k_ref, vt_ref, s_ref, p_ref, acc_ref):
    t_ctx = kc_ref.shape[1]

    @pl.when(pl.program_id(2) == 0)
    def _():
        k_ref[:t_ctx, :] = kc_ref[0]
        k_ref[t_ctx:, :] = kl_ref[0]
        vt_ref[:HEAD_W, :t_ctx] = vtc_ref[0]
        vt_ref[:HEAD_W, t_ctx:] = vtl_ref[0]
        extra = lax.broadcasted_iota(jnp.int32, (SUM_ROWS, vt_ref.shape[1]), 0) == 0
        vt_ref[HEAD_W:, :] = jnp.where(extra, 1.0, 0.0).astype(vt_ref.dtype)

    qt = q_ref[0].astype(jnp.float32).T
    row = lax.broadcasted_iota(jnp.int32, qt.shape, 0)
    qm = tuple(jnp.where(sel, qt, 0.0).astype(jnp.bfloat16) for sel in (row < MAP_D, row >= MAP_D))
    tq = qt.shape[1]
    n_chunks = k_ref.shape[0] // tk

    def scores(j):
        kb = k_ref[pl.ds(pl.multiple_of(j * tk, tk), tk), :]
        smax = []
        for m in range(2):
            s = jnp.dot(kb, qm[m], preferred_element_type=jnp.float32)
            s_ref[m] = s
            smax.append(jnp.max(s, axis=0, keepdims=True))
        return tuple(smax)

    def weights(smax, mx):
        new_mx, corrs = [], []
        for m in range(2):
            mx_new = jnp.maximum(mx[m], smax[m])
            p_ref[m] = jnp.exp2(s_ref[m] - mx_new).astype(jnp.bfloat16)
            new_mx.append(mx_new)
            corrs.append(jnp.exp2(mx[m] - mx_new))
        return tuple(new_mx), tuple(corrs)

    def values(j, corr):
        vtb = vt_ref[:, pl.ds(pl.multiple_of(j * tk, tk), tk)]
        for m in range(2):
            acc_ref[m] = acc_ref[m] * corr[m] + jnp.dot(vtb, p_ref[m], preferred_element_type=jnp.float32)

    mx = tuple(jnp.full((1, tq), -jnp.inf, jnp.float32) for _ in range(2))
    acc_ref[...] = jnp.zeros_like(acc_ref)
    mx, corr = weights(scores(0), mx)
    smax = scores(1)

    def body(j, carry):
        mx, smax, corr_prev = carry
        values(j - 1, corr_prev)
        mx, corr = weights(smax, mx)
        return mx, scores(j + 1), corr

    mx, smax, corr = lax.fori_loop(1, n_chunks - 1, body, (mx, smax, corr))
    values(n_chunks - 2, corr)
    mx, corr = weights(smax, mx)
    values(n_chunks - 1, corr)

    lp = lam_ref[0]
    lam = (jnp.exp(jnp.sum(lp[0:1] * lp[1:2], axis=-1, keepdims=True))
           - jnp.exp(jnp.sum(lp[2:3] * lp[3:4], axis=-1, keepdims=True)) + LAMBDA_INIT)
    l0, l1 = acc_ref[0, HEAD_W:HEAD_W + 1, :], acc_ref[1, HEAD_W:HEAD_W + 1, :]
    ot = acc_ref[0, :HEAD_W, :] / l0 - lam * (acc_ref[1, :HEAD_W, :] / l1)
    ot = ot * lax.rsqrt(jnp.mean(ot * ot, axis=0, keepdims=True) + EPS)
    o_ref[0] = (ot.T * gain_ref[...] * (1.0 - LAMBDA_INIT)).astype(o_ref.dtype)


def _key_chunk(n_keys):
    return max(t for t in range(LANES, ATTN_MAX_TK + 1, LANES) if n_keys % t == 0)


def _diff_attention(q, k_ctx, vt_ctx, k, vt, diff_lambda, gain):
    B, N, _ = q.shape
    T_ctx = k_ctx.shape[1]
    Tk = T_ctx + N
    tq, tk = ATTN_TQ, _key_chunk(Tk)
    assert N % tq == 0 and Tk // tk >= 3 and T_ctx % LANES == 0
    return pl.pallas_call(
        functools.partial(_attn_kernel, tk),
        grid=(B, HEADS, N // tq),
        in_specs=[pl.BlockSpec((1, tq, HEAD_W), lambda b, h, i: (b, i, h)),
                  pl.BlockSpec((1, T_ctx, HEAD_W), lambda b, h, i: (b, 0, h)),
                  pl.BlockSpec((1, HEAD_W, T_ctx), lambda b, h, i: (b, h, 0)),
                  pl.BlockSpec((1, N, HEAD_W), lambda b, h, i: (b, 0, h)),
                  pl.BlockSpec((1, HEAD_W, N), lambda b, h, i: (b, h, 0)),
                  pl.BlockSpec((1, 4, MAP_D), lambda b, h, i: (0, 0, 0)),
                  pl.BlockSpec((1, HEAD_W), lambda b, h, i: (0, 0))],
        out_specs=pl.BlockSpec((1, tq, HEAD_W), lambda b, h, i: (b, i, h)),
        out_shape=jax.ShapeDtypeStruct((B, N, BRANCH_W), jnp.bfloat16),
        scratch_shapes=[pltpu.VMEM((Tk, HEAD_W), jnp.bfloat16), pltpu.VMEM((HEAD_W + SUM_ROWS, Tk), jnp.bfloat16),
                        pltpu.VMEM((2, tk, tq), jnp.float32), pltpu.VMEM((2, tk, tq), jnp.bfloat16),
                        pltpu.VMEM((2, HEAD_W + SUM_ROWS, tq), jnp.float32)],
        compiler_params=pltpu.CompilerParams(
            dimension_semantics=("parallel", "parallel", "arbitrary"),
            vmem_limit_bytes=V7X_VMEM_LIMIT_BYTES),
        name="diff_attention",
    )(q, k_ctx, vt_ctx, k, vt, diff_lambda, gain)


def _level_maps():
    t = np.arange(HGRN_CHUNK)[:, None]
    s = np.arange(HGRN_CHUNK)[None, :]
    x = t ^ s
    lvl = np.where(x > 0, np.floor(np.log2(np.maximum(x, 1))).astype(np.int32) + 1, 0)
    lvl = np.where(s > t, -1, lvl).astype(np.int32)
    return np.stack([lvl, lvl.T])


def _shift_down(x, s):
    return pltpu.roll(x, s, axis=1)


def _shift_up(x, s):
    return pltpu.roll(x, SUBLANES - s, axis=1)


def _neg_abs(x):
    bits = lax.bitcast_convert_type(x, jnp.uint32) | jnp.uint32(0x80000000)
    return lax.bitcast_convert_type(bits, jnp.float32)


def _hgrn_chain(q, z, v_bf, lb, st, masks, reverse):
    C = HGRN_CHUNK
    groups = C // SUBLANES
    sig = jax.nn.sigmoid(z)
    g = jnp.log2(lb + (1.0 - lb) * sig)
    kk = (1.0 - lb) * (1.0 - sig)
    grouped = (groups, SUBLANES, HEAD_W)
    sub = lax.broadcasted_iota(jnp.int32, grouped, 1)

    bg = g.reshape(grouped)
    for s in (1, 2, 4):
        if reverse:
            bg = bg + jnp.where(sub < SUBLANES - s, _shift_up(bg, s), 0.0)
        else:
            bg = bg + jnp.where(sub >= s, _shift_down(bg, s), 0.0)
    parts = [bg[r] for r in range(groups)]
    order = range(groups - 2, -1, -1) if reverse else range(1, groups)
    for r in order:
        prev = parts[r + 1][0:1, :] if reverse else parts[r - 1][SUBLANES - 1:SUBLANES, :]
        parts[r] = parts[r] + prev
    b = jnp.concatenate(parts, axis=0)
    bg = b.reshape(grouped)

    q_bf = q.astype(jnp.bfloat16)
    kk_bf = kk.astype(jnp.bfloat16)

    def rows(x, r):
        return x[r * SUBLANES:(r + 1) * SUBLANES, :]

    a_parts = [None] * groups

    own = bg
    nbr = _shift_up(bg, 1) if reverse else _shift_down(bg, 1)
    for j in range(1, HGRN_LEVELS + 1):
        h = 1 << (j - 1)
        if 2 * h <= SUBLANES:
            upper = (sub & h) != 0
            ref = jnp.where(upper, own, nbr) if reverse else jnp.where(upper, nbr, own)
            w = jnp.exp2(_neg_abs(bg - ref)).reshape(C, HEAD_W).astype(jnp.bfloat16)
            if 4 * h <= SUBLANES:
                if reverse:
                    own = jnp.where(upper, _shift_down(own, h), own)
                    nbr = jnp.where(upper, nbr, _shift_up(nbr, h))
                else:
                    own = jnp.where(upper, own, _shift_up(own, h))
                    nbr = jnp.where(upper, _shift_down(nbr, h), nbr)
            pm = lax.dot_general(q_bf * w, kk_bf * w, _NT, preferred_element_type=jnp.float32)
            a_parts = [jnp.where(rows(masks[j], r), rows(pm, r), 0.0 if a_parts[r] is None else a_parts[r])
                       for r in range(groups)]
        else:
            n_blk, half_groups = C // (2 * h), h // SUBLANES
            blocked = (n_blk, 2 * h, HEAD_W)
            b_blk = b.reshape(blocked)
            ref = b_blk[:, h:h + 1, :] if reverse else b_blk[:, h - 1:h, :]
            w_blk = jnp.exp2(_neg_abs(b_blk - ref))
            q_half, k_half = (slice(0, h), slice(h, 2 * h)) if reverse else (slice(h, 2 * h), slice(0, h))
            q_t = (q.reshape(blocked)[:, q_half] * w_blk[:, q_half]).reshape(C // 2, HEAD_W)
            k_t = kk.reshape(blocked)[:, k_half] * w_blk[:, k_half]
            zeros = jnp.zeros_like(k_t)
            k_t = jnp.concatenate([zeros, k_t] if reverse else [k_t, zeros], axis=1).reshape(C, HEAD_W)
            pm = lax.dot_general(q_t.astype(jnp.bfloat16), k_t.astype(jnp.bfloat16), _NT,
                                 preferred_element_type=jnp.float32)
            for blk in range(n_blk):
                for i in range(half_groups):
                    r = blk * 2 * half_groups + (0 if reverse else half_groups) + i
                    piece = rows(pm, blk * half_groups + i)
                    if n_blk == 1:
                        a_parts[r] = a_parts[r] + piece
                    else:
                        a_parts[r] = jnp.where(rows(masks[j], r), piece, a_parts[r])
    a = jnp.concatenate(a_parts, axis=0)

    o = jnp.dot(a.astype(jnp.bfloat16), v_bf, preferred_element_type=jnp.float32)
    o = o + lax.dot_general(q_bf * jnp.exp2(b).astype(jnp.bfloat16), st.astype(jnp.bfloat16), _NT,
                            preferred_element_type=jnp.float32)
    o = o + jnp.sum(q * kk, axis=-1, keepdims=True) * v_bf.astype(jnp.float32)
    b_tot = b[0:1, :] if reverse else b[C - 1:C, :]
    k_hat = kk_bf * jnp.exp2(b_tot - b).astype(jnp.bfloat16)
    st_new = st * jnp.exp2(b_tot) + lax.dot_general(v_bf, k_hat, _TN, preferred_element_type=jnp.float32)
    return o, st_new


def _hgrn_kernel(n_sub, has_s0, qf_ref, vf_ref, zf_ref, qb_ref, vb_ref, zb_ref, lbp_ref, lvl_ref, *refs):
    s0_ref, (of_ref, ob_ref, sfin_ref, st_ref) = (refs[0], refs[1:]) if has_s0 else (None, refs)
    i = pl.program_id(1)
    C = HGRN_CHUNK

    @pl.when(i == 0)
    def _():
        st_ref[...] = jnp.zeros_like(st_ref) if s0_ref is None else s0_ref[0]

    dirs = ((qf_ref, vf_ref, zf_ref, of_ref), (qb_ref, vb_ref, zb_ref, ob_ref))
    masks, lbs = [], []
    for d in range(2):
        lvl = lvl_ref[d]
        masks.append([lvl == j for j in range(HGRN_LEVELS + 1)])
        p0, p1 = lbp_ref[d, 0:1, :], lbp_ref[d, 1:2, :]
        pm = jnp.maximum(p0, p1)
        e0 = jnp.exp(p0 - pm)
        lbs.append(e0 / (e0 + jnp.exp(p1 - pm)))
    states = [[st_ref[d, hh] for hh in range(HEADS)] for d in range(2)]
    for sub in range(n_sub):
        for d, (q_ref, v_ref, z_ref, o_ref) in enumerate(dirs):
            c = n_sub - 1 - sub if d == 1 else sub
            tok = slice(c * C, (c + 1) * C)
            for hh in range(HEADS):
                cols = slice(hh * HEAD_W, (hh + 1) * HEAD_W)
                o, states[d][hh] = _hgrn_chain(q_ref[0, tok, cols], z_ref[0, tok, cols], v_ref[0, tok, cols],
                                               lbs[d][:, cols], states[d][hh], masks[d], reverse=(d == 1))
                o_ref[0, tok, cols] = o.astype(o_ref.dtype)
    for d in range(2):
        for hh in range(HEADS):
            st_ref[d, hh] = states[d][hh]

    @pl.when(i == pl.num_programs(1) - 1)
    def _():
        sfin_ref[0] = st_ref[...]


def _hgrn2_bidir(hq, hi, hff, hfb, lb_param, s0, n_sub):
    B, T, _ = hq.shape
    rows = HGRN_CHUNK * n_sub
    n = T // rows
    assert n * rows == T
    fwd = pl.BlockSpec((1, rows, BRANCH_W), lambda b, i: (b, i, 0))
    bwd = pl.BlockSpec((1, rows, BRANCH_W), lambda b, i: (b, n - 1 - i, 0))
    st_spec = pl.BlockSpec((1, 2, HEADS, HEAD_W, HEAD_W), lambda b, i: (b, 0, 0, 0, 0))
    lvl = jnp.asarray(_level_maps())
    return pl.pallas_call(
        functools.partial(_hgrn_kernel, n_sub, s0 is not None),
        grid=(B, n),
        in_specs=[fwd, fwd, fwd, bwd, bwd, bwd,
                  pl.BlockSpec(lb_param.shape, lambda b, i: (0, 0, 0)),
                  pl.BlockSpec(lvl.shape, lambda b, i: (0, 0, 0))] + ([] if s0 is None else [st_spec]),
        out_specs=[fwd, bwd, st_spec],
        out_shape=[jax.ShapeDtypeStruct((B, T, BRANCH_W), jnp.bfloat16),
                   jax.ShapeDtypeStruct((B, T, BRANCH_W), jnp.bfloat16),
                   jax.ShapeDtypeStruct((B, 2, HEADS, HEAD_W, HEAD_W), jnp.float32)],
        scratch_shapes=[pltpu.VMEM((2, HEADS, HEAD_W, HEAD_W), jnp.float32)],
        compiler_params=pltpu.CompilerParams(
            dimension_semantics=("parallel", "arbitrary"), vmem_limit_bytes=V7X_VMEM_LIMIT_BYTES),
        name="hgrn2_bidir",
    )(hq, hi, hff, hq, hi, hfb, lb_param, lvl, *([] if s0 is None else [s0]))


def _merge_kernel(att_ref, ag_ref, of_ref, ob_ref, hg_ref, x_ref, gate_ref, w_ref, hgain_ref, lng_ref, lnb_ref,
                  o_ref):
    f32 = jnp.float32
    att = att_ref[0].astype(f32) * _silu(ag_ref[0].astype(f32))
    o = of_ref[0].astype(f32) + ob_ref[0].astype(f32)
    parts = []
    for hh in range(HEADS):
        oh = o[:, hh * HEAD_W:(hh + 1) * HEAD_W]
        parts.append(oh * lax.rsqrt(jnp.mean(oh * oh, axis=-1, keepdims=True) + EPS) * hgain_ref[...])
    hg = jnp.concatenate(parts, axis=1) * _silu(hg_ref[0].astype(f32))
    y_in = jnp.concatenate([att, hg], axis=1).astype(jnp.bfloat16)
    y = jnp.dot(y_in, w_ref[...], preferred_element_type=jnp.float32)
    u = ALPHA * x_ref[0] + gate_ref[0] * y
    mu = jnp.mean(u, axis=-1, keepdims=True)
    uc = u - mu
    var = jnp.mean(uc * uc, axis=-1, keepdims=True)
    o_ref[0] = uc * lax.rsqrt(var + EPS) * lng_ref[...] + lnb_ref[...]


def _merge(att, ag, o_f, o_b, hg, x, gate, w_out_bf, hgain, ln_g, ln_b, tm):
    B, N, D = x.shape
    half = pl.BlockSpec((1, tm, BRANCH_W), lambda b, i: (b, i, 0))
    full = pl.BlockSpec((1, tm, D), lambda b, i: (b, i, 0))
    row = lambda w: pl.BlockSpec((1, w), lambda b, i: (0, 0))
    return pl.pallas_call(
        _merge_kernel,
        grid=(B, N // tm),
        in_specs=[half, half, half, half, half, full,
                  pl.BlockSpec((1, 1, D), lambda b, i: (b, 0, 0)),
                  pl.BlockSpec(w_out_bf.shape, lambda b, i: (0, 0)),
                  row(HEAD_W), row(D), row(D)],
        out_specs=full,
        out_shape=jax.ShapeDtypeStruct((B, N, D), jnp.float32),
        compiler_params=pltpu.CompilerParams(
            dimension_semantics=("parallel", "parallel"), vmem_limit_bytes=V7X_VMEM_LIMIT_BYTES),
        name="merge_out_proj_ln",
    )(att, ag, o_f, o_b, hg, x, gate, w_out_bf, hgain, ln_g, ln_b)


def kernel(x, c, ctx, c_ctx, w_ada, b_ada, w_in, w_out, diff_lambda, diff_subln_gain, hgrn_lower_bound,
           hgrn_norm_gain, ln_gain, ln_bias):
    B, N, D = x.shape
    assert DEPTH == 1 and w_ada.shape[0] == 1
    cvec = jnp.concatenate([c, c_ctx[None, :], jnp.zeros((SUBLANES - B - 1, D), c.dtype)], axis=0)
    mod = _modulation(cvec, w_ada, b_ada)
    shift, scale, gate = mod[:, :D], mod[:, D:2 * D], mod[:, 2 * D:]
    s1p = (1.0 + scale)[:, None, :]
    shift = shift[:, None, :]
    ctx_rows = jnp.full((B,), B, jnp.int32)

    k_c, v_c, hq_c, hi_c, hff_c, hfb_c = _in_projection(
        ctx, s1p[ctx_rows], shift[ctx_rows], w_in, _CTX_GROUPS, tm=ctx.shape[1])
    q, k, v, ag, hq, hi, hff, hfb, hg = _in_projection(
        x, s1p[:B], shift[:B], w_in, _LATENT_GROUPS, tm=ROW_TILE, rope_tables=_rope_tables(N))

    att = _diff_attention(q, k_c, v_c, k, v, diff_lambda, diff_subln_gain)

    _, _, s_ctx = _hgrn2_bidir(hq_c, hi_c, hff_c, hfb_c, hgrn_lower_bound, None, n_sub=ctx.shape[1] // HGRN_CHUNK)
    o_f, o_b, _ = _hgrn2_bidir(hq, hi, hff, hfb, hgrn_lower_bound, s_ctx, n_sub=HGRN_CHUNKS_PER_STEP)

    return _merge(att, ag, o_f, o_b, hg, x, gate[:B, None, :], w_out[0].astype(jnp.bfloat16),
                  hgrn_norm_gain, ln_gain, ln_bias, tm=MERGE_TILE)
```

```python
import functools
import math

import numpy as np
import jax
import jax.numpy as jnp
from jax import lax
from jax.experimental import pallas as pl
from jax.experimental.pallas import tpu as pltpu

DEPTH = 1
GRID_W = 64
HEADS = 4
HEAD_W = 128
MAP_D = 64
BRANCH_W = HEADS * HEAD_W
ROPE_BASE = 10000.0
ROPE_FREQS = MAP_D // 4
EPS = 1e-5
LAMBDA_INIT = 0.8 - 0.6 * math.exp(-0.3 * 0)
ALPHA = (2.0 * DEPTH) ** 0.25

V7X_VMEM_LIMIT_BYTES = 56 * 1024 * 1024
SUBLANES = 8
LANES = 128
SUM_ROWS = 16
ROW_TILE = 512
MERGE_TILE = 1024
ATTN_TQ = 2048
ATTN_MAX_TK = 1408
HGRN_CHUNKS_PER_STEP = 4
HGRN_CHUNK = 128
HGRN_LEVELS = 7

_NT = (((1,), (1,)), ((), ()))
_TN = (((0,), (0,)), ((), ()))


def _silu(x):
    return x * jax.nn.sigmoid(x)


def _mod_kernel(c_ref, w_ref, b_ref, o_ref):
    a = _silu(c_ref[...]).astype(jnp.bfloat16)
    o_ref[...] = jnp.dot(a, w_ref[0].astype(jnp.bfloat16), preferred_element_type=jnp.float32) + b_ref[...]


def _modulation(cvec, w_ada, b_ada):
    rows, d = cvec.shape
    n_out = w_ada.shape[2]
    bn = 1024
    return pl.pallas_call(
        _mod_kernel,
        grid=(n_out // bn,),
        in_specs=[pl.BlockSpec((rows, d), lambda j: (0, 0)),
                  pl.BlockSpec((1, d, bn), lambda j: (0, 0, j)),
                  pl.BlockSpec((1, bn), lambda j: (0, j))],
        out_specs=pl.BlockSpec((rows, bn), lambda j: (0, j)),
        out_shape=jax.ShapeDtypeStruct((rows, n_out), jnp.float32),
        compiler_params=pltpu.CompilerParams(vmem_limit_bytes=V7X_VMEM_LIMIT_BYTES),
        name="modulation",
    )(cvec, w_ada, b_ada)


_LATENT_GROUPS = ((0, "rope_q", jnp.bfloat16), (1, "rope", jnp.bfloat16), (2, "transposed", jnp.bfloat16),
                  (3, "plain", jnp.bfloat16), (4, "silu", jnp.float32), (5, "plain", jnp.bfloat16),
                  (6, "plain", jnp.float32), (7, "plain", jnp.float32), (8, "plain", jnp.bfloat16))
_CTX_GROUPS = ((1, "plain", jnp.bfloat16), (2, "transposed", jnp.bfloat16), (4, "silu", jnp.float32),
               (5, "plain", jnp.bfloat16), (6, "plain", jnp.float32), (7, "plain", jnp.float32))
Q_SCALE = math.log2(math.e) / math.sqrt(MAP_D)


def _proj_kernel(groups, use_rope, *refs):
    if use_rope:
        x_ref, s1p_ref, sh_ref, w_ref, cos_row_ref, sin_row_ref, cos_col_ref, sin_col_ref = refs[:8]
        out_refs = refs[8:]
    else:
        x_ref, s1p_ref, sh_ref, w_ref = refs[:4]
        out_refs = refs[4:]
    h = (x_ref[0] * s1p_ref[0] + sh_ref[0]).astype(jnp.bfloat16)
    if use_rope:
        by_row = (lax.broadcasted_iota(jnp.int32, (GRID_W, HEAD_W), 1) & (MAP_D - 1)) < 2 * ROPE_FREQS

        def token_table(row_ref, col_ref):
            grid_rows = [jnp.where(by_row, row_ref[r:r + 1, :], col_ref[...]) for r in range(row_ref.shape[0])]
            return jnp.concatenate([jnp.concatenate(grid_rows, axis=0)] * HEADS, axis=1)

        cos = token_table(cos_row_ref, cos_col_ref)
        sin = token_table(sin_row_ref, sin_col_ref)
        lane = lax.broadcasted_iota(jnp.int32, cos.shape, 1)
        first_half = (lane & (2 * ROPE_FREQS - 1)) < ROPE_FREQS
    for (g, kind, dt), o_ref in zip(groups, out_refs):
        p = jnp.dot(h, w_ref[0, :, g * BRANCH_W:(g + 1) * BRANCH_W].astype(jnp.bfloat16),
                    preferred_element_type=jnp.float32)
        if kind in ("rope", "rope_q"):
            partner = jnp.where(first_half,
                                pltpu.roll(p, BRANCH_W - ROPE_FREQS, axis=1),
                                pltpu.roll(p, ROPE_FREQS, axis=1))
            p = p * cos + partner * sin
            if kind == "rope_q":
                p = p * Q_SCALE
        elif kind == "silu":
            p = _silu(p)
        elif kind == "transposed":
            p = p.T
        o_ref[0] = p.astype(dt)


def _in_projection(x, s1p, sh, w_in, groups, tm, rope_tables=None):
    B, T, D = x.shape
    use_rope = rope_tables is not None
    in_specs = [pl.BlockSpec((1, tm, D), lambda b, i: (b, i, 0)),
                pl.BlockSpec((1, 1, D), lambda b, i: (b, 0, 0)),
                pl.BlockSpec((1, 1, D), lambda b, i: (b, 0, 0)),
                pl.BlockSpec(w_in.shape, lambda b, i: (0, 0, 0), pipeline_mode=pl.Buffered(1))]
    args = [x, s1p, sh, w_in]
    if use_rope:
        assert tm % (GRID_W * SUBLANES) == 0
        in_specs += [pl.BlockSpec((tm // GRID_W, HEAD_W), lambda b, i: (i, 0))] * 2
        in_specs += [pl.BlockSpec((GRID_W, HEAD_W), lambda b, i: (0, 0))] * 2
        args += list(rope_tables)
    out_specs = [pl.BlockSpec((1, BRANCH_W, tm), lambda b, i: (b, 0, i)) if kind == "transposed"
                 else pl.BlockSpec((1, tm, BRANCH_W), lambda b, i: (b, i, 0)) for _, kind, _ in groups]
    out_shape = [jax.ShapeDtypeStruct((B, BRANCH_W, T) if kind == "transposed" else (B, T, BRANCH_W), dt)
                 for _, kind, dt in groups]
    return pl.pallas_call(
        functools.partial(_proj_kernel, groups, use_rope),
        grid=(B, T // tm),
        in_specs=in_specs, out_specs=out_specs, out_shape=out_shape,
        compiler_params=pltpu.CompilerParams(
            dimension_semantics=("parallel", "parallel"), vmem_limit_bytes=V7X_VMEM_LIMIT_BYTES),
        name="in_proj_rope" if use_rope else "in_proj_ctx",
    )(*args)


def _rope_tables(n_tokens):
    inv_freq = jnp.asarray(ROPE_BASE, jnp.float32) ** (-jnp.arange(ROPE_FREQS, dtype=jnp.float32) / ROPE_FREQS)
    freq = jnp.tile(inv_freq, HEAD_W // ROPE_FREQS)
    sign = np.where((np.arange(HEAD_W) % (2 * ROPE_FREQS)) < ROPE_FREQS, -1.0, 1.0).astype(np.float32)
    row_ang = jnp.arange(n_tokens // GRID_W, dtype=jnp.float32)[:, None] * freq
    col_ang = jnp.arange(GRID_W, dtype=jnp.float32)[:, None] * freq
    return jnp.cos(row_ang), jnp.sin(row_ang) * sign, jnp.cos(col_ang), jnp.sin(col_ang) * sign


def _attn_kernel(tk, q_ref, kc_ref, vtc_ref, kl_ref, vtl_ref, lam_ref, gain_ref, ag_ref, o_ref,
                 k_ref, vt_ref, s_ref, p_ref, acc_ref):
    t_ctx = kc_ref.shape[1]

    @pl.when(pl.program_id(2) == 0)
    def _():
        k_ref[:t_ctx, :] = kc_ref[0]
        k_ref[t_ctx:, :] = kl_ref[0]
        vt_ref[:HEAD_W, :t_ctx] = vtc_ref[0]
        vt_ref[:HEAD_W, t_ctx:] = vtl_ref[0]
        extra = lax.broadcasted_iota(jnp.int32, (SUM_ROWS, vt_ref.shape[1]), 0) == 0
        vt_ref[HEAD_W:, :] = jnp.where(extra, 1.0, 0.0).astype(vt_ref.dtype)

    qt = q_ref[0].astype(jnp.float32).T
    row = lax.broadcasted_iota(jnp.int32, qt.shape, 0)
    qm = tuple(jnp.where(sel, qt, 0.0).astype(jnp.bfloat16) for sel in (row < MAP_D, row >= MAP_D))
    tq = qt.shape[1]
    n_chunks = k_ref.shape[0] // tk

    def scores(j):
        kb = k_ref[pl.ds(pl.multiple_of(j * tk, tk), tk), :]
        smax = []
        for m in range(2):
            s = jnp.dot(kb, qm[m], preferred_element_type=jnp.float32)
            s_ref[m] = s
            smax.append(jnp.max(s, axis=0, keepdims=True))
        return tuple(smax)

    def weights(smax, mx):
        new_mx, corrs = [], []
        for m in range(2):
            mx_new = jnp.maximum(mx[m], smax[m])
            p_ref[m] = jnp.exp2(s_ref[m] - mx_new).astype(jnp.bfloat16)
            new_mx.append(mx_new)
            corrs.append(jnp.exp2(mx[m] - mx_new))
        return tuple(new_mx), tuple(corrs)

    def values(j, corr):
        vtb = vt_ref[:, pl.ds(pl.multiple_of(j * tk, tk), tk)]
        for m in range(2):
            acc_ref[m] = acc_ref[m] * corr[m] + jnp.dot(vtb, p_ref[m], preferred_element_type=jnp.float32)

    mx = tuple(jnp.full((1, tq), -jnp.inf, jnp.float32) for _ in range(2))
    acc_ref[...] = jnp.zeros_like(acc_ref)
    mx, corr = weights(scores(0), mx)
    smax = scores(1)

    def body(j, carry):
        mx, smax, corr_prev = carry
        values(j - 1, corr_prev)
        mx, corr = weights(smax, mx)
        return mx, scores(j + 1), corr

    mx, smax, corr = lax.fori_loop(1, n_chunks - 1, body, (mx, smax, corr))
    values(n_chunks - 2, corr)
    mx, corr = weights(smax, mx)
    values(n_chunks - 1, corr)

    lp = lam_ref[0]
    lam = (jnp.exp(jnp.sum(lp[0:1] * lp[1:2], axis=-1, keepdims=True))
           - jnp.exp(jnp.sum(lp[2:3] * lp[3:4], axis=-1, keepdims=True)) + LAMBDA_INIT)
    l0, l1 = acc_ref[0, HEAD_W:HEAD_W + 1, :], acc_ref[1, HEAD_W:HEAD_W + 1, :]
    ot = acc_ref[0, :HEAD_W, :] / l0 - lam * (acc_ref[1, :HEAD_W, :] / l1)
    ot = ot * lax.rsqrt(jnp.mean(ot * ot, axis=0, keepdims=True) + EPS)
    o = ot.T * gain_ref[...] * (1.0 - LAMBDA_INIT)
    o_ref[0] = (o * _silu(ag_ref[0].astype(jnp.float32))).astype(o_ref.dtype)


def _key_chunk(n_keys):
    return max(t for t in range(LANES, ATTN_MAX_TK + 1, LANES) if n_keys % t == 0)


def _diff_attention(q, k_ctx, vt_ctx, k, vt, diff_lambda, gain, ag):
    B, N, _ = q.shape
    T_ctx = k_ctx.shape[1]
    Tk = T_ctx + N
    tq, tk = ATTN_TQ, _key_chunk(Tk)
    assert N % tq == 0 and Tk // tk >= 3 and T_ctx % LANES == 0
    return pl.pallas_call(
        functools.partial(_attn_kernel, tk),
        grid=(B, HEADS, N // tq),
        in_specs=[pl.BlockSpec((1, tq, HEAD_W), lambda b, h, i: (b, i, h)),
                  pl.BlockSpec((1, T_ctx, HEAD_W), lambda b, h, i: (b, 0, h)),
                  pl.BlockSpec((1, HEAD_W, T_ctx), lambda b, h, i: (b, h, 0)),
                  pl.BlockSpec((1, N, HEAD_W), lambda b, h, i: (b, 0, h)),
                  pl.BlockSpec((1, HEAD_W, N), lambda b, h, i: (b, h, 0)),
                  pl.BlockSpec((1, 4, MAP_D), lambda b, h, i: (0, 0, 0)),
                  pl.BlockSpec((1, HEAD_W), lambda b, h, i: (0, 0)),
                  pl.BlockSpec((1, tq, HEAD_W), lambda b, h, i: (b, i, h))],
        out_specs=pl.BlockSpec((1, tq, HEAD_W), lambda b, h, i: (b, i, h)),
        out_shape=jax.ShapeDtypeStruct((B, N, BRANCH_W), jnp.bfloat16),
        scratch_shapes=[pltpu.VMEM((Tk, HEAD_W), jnp.bfloat16), pltpu.VMEM((HEAD_W + SUM_ROWS, Tk), jnp.bfloat16),
                        pltpu.VMEM((2, tk, tq), jnp.float32), pltpu.VMEM((2, tk, tq), jnp.bfloat16),
                        pltpu.VMEM((2, HEAD_W + SUM_ROWS, tq), jnp.float32)],
        compiler_params=pltpu.CompilerParams(
            dimension_semantics=("parallel", "parallel", "arbitrary"),
            vmem_limit_bytes=V7X_VMEM_LIMIT_BYTES),
        name="diff_attention",
    )(q, k_ctx, vt_ctx, k, vt, diff_lambda, gain, ag)


def _level_maps():
    t = np.arange(HGRN_CHUNK)[:, None]
    s = np.arange(HGRN_CHUNK)[None, :]
    x = t ^ s
    lvl = np.where(x > 0, np.floor(np.log2(np.maximum(x, 1))).astype(np.int32) + 1, 0)
    lvl = np.where(s > t, -1, lvl).astype(np.int32)
    return np.stack([lvl, lvl.T])


def _shift_down(x, s):
    return pltpu.roll(x, s, axis=1)


def _shift_up(x, s):
    return pltpu.roll(x, SUBLANES - s, axis=1)


def _neg_abs(x):
    bits = lax.bitcast_convert_type(x, jnp.uint32) | jnp.uint32(0x80000000)
    return lax.bitcast_convert_type(bits, jnp.float32)


def _hgrn_chain(q, z, v_bf, lb, st, masks, reverse):
    C = HGRN_CHUNK
    groups = C // SUBLANES
    sig = jax.nn.sigmoid(z)
    g = jnp.log2(lb + (1.0 - lb) * sig)
    kk = (1.0 - lb) * (1.0 - sig)
    grouped = (groups, SUBLANES, HEAD_W)
    sub = lax.broadcasted_iota(jnp.int32, grouped, 1)

    bg = g.reshape(grouped)
    for s in (1, 2, 4):
        if reverse:
            bg = bg + jnp.where(sub < SUBLANES - s, _shift_up(bg, s), 0.0)
        else:
            bg = bg + jnp.where(sub >= s, _shift_down(bg, s), 0.0)
    parts = [bg[r] for r in range(groups)]
    order = range(groups - 2, -1, -1) if reverse else range(1, groups)
    for r in order:
        prev = parts[r + 1][0:1, :] if reverse else parts[r - 1][SUBLANES - 1:SUBLANES, :]
        parts[r] = parts[r] + prev
    b = jnp.concatenate(parts, axis=0)
    bg = b.reshape(grouped)

    q_bf = q.astype(jnp.bfloat16)
    kk_bf = kk.astype(jnp.bfloat16)

    def rows(x, r):
        return x[r * SUBLANES:(r + 1) * SUBLANES, :]

    a_parts = [None] * groups

    own = bg
    nbr = _shift_up(bg, 1) if reverse else _shift_down(bg, 1)
    for j in range(1, HGRN_LEVELS + 1):
        h = 1 << (j - 1)
        if 2 * h <= SUBLANES:
            upper = (sub & h) != 0
            ref = jnp.where(upper, own, nbr) if reverse else jnp.where(upper, nbr, own)
            w = jnp.exp2(_neg_abs(bg - ref)).reshape(C, HEAD_W).astype(jnp.bfloat16)
            if 4 * h <= SUBLANES:
                if reverse:
                    own = jnp.where(upper, _shift_down(own, h), own)
                    nbr = jnp.where(upper, nbr, _shift_up(nbr, h))
                else:
                    own = jnp.where(upper, own, _shift_up(own, h))
                    nbr = jnp.where(upper, _shift_down(nbr, h), nbr)
            pm = lax.dot_general(q_bf * w, kk_bf * w, _NT, preferred_element_type=jnp.float32)
            a_parts = [jnp.where(rows(masks[j], r), rows(pm, r), 0.0 if a_parts[r] is None else a_parts[r])
                       for r in range(groups)]
        else:
            n_blk, half_groups = C // (2 * h), h // SUBLANES
            blocked = (n_blk, 2 * h, HEAD_W)
            b_blk = b.reshape(blocked)
            ref = b_blk[:, h:h + 1, :] if reverse else b_blk[:, h - 1:h, :]
            w_blk = jnp.exp2(_neg_abs(b_blk - ref))
            q_half, k_half = (slice(0, h), slice(h, 2 * h)) if reverse else (slice(h, 2 * h), slice(0, h))
            q_t = (q.reshape(blocked)[:, q_half] * w_blk[:, q_half]).reshape(C // 2, HEAD_W)
            k_t = kk.reshape(blocked)[:, k_half] * w_blk[:, k_half]
            zeros = jnp.zeros_like(k_t)
            k_t = jnp.concatenate([zeros, k_t] if reverse else [k_t, zeros], axis=1).reshape(C, HEAD_W)
            pm = lax.dot_general(q_t.astype(jnp.bfloat16), k_t.astype(jnp.bfloat16), _NT,
                                 preferred_element_type=jnp.float32)
            for blk in range(n_blk):
                for i in range(half_groups):
                    r = blk * 2 * half_groups + (0 if reverse else half_groups) + i
                    piece = rows(pm, blk * half_groups + i)
                    if n_blk == 1:
                        a_parts[r] = a_parts[r] + piece
                    else:
                        a_parts[r] = jnp.where(rows(masks[j], r), piece, a_parts[r])
    a = jnp.concatenate(a_parts, axis=0)

    o = jnp.dot(a.astype(jnp.bfloat16), v_bf, preferred_element_type=jnp.float32)
    o = o + lax.dot_general(q_bf * jnp.exp2(b).astype(jnp.bfloat16), st.astype(jnp.bfloat16), _NT,
                            preferred_element_type=jnp.float32)
    o = o + jnp.sum(q * kk, axis=-1, keepdims=True) * v_bf.astype(jnp.float32)
    b_tot = b[0:1, :] if reverse else b[C - 1:C, :]
    k_hat = kk_bf * jnp.exp2(b_tot - b).astype(jnp.bfloat16)
    st_new = st * jnp.exp2(b_tot) + lax.dot_general(v_bf, k_hat, _TN, preferred_element_type=jnp.float32)
    return o, st_new


def _hgrn_kernel(n_sub, has_s0, qf_ref, vf_ref, zf_ref, qb_ref, vb_ref, zb_ref, lbp_ref, lvl_ref, *refs):
    s0_ref, (of_ref, ob_ref, sfin_ref, st_ref) = (refs[0], refs[1:]) if has_s0 else (None, refs)
    i = pl.program_id(1)
    C = HGRN_CHUNK

    @pl.when(i == 0)
    def _():
        st_ref[...] = jnp.zeros_like(st_ref) if s0_ref is None else s0_ref[0]

    dirs = ((qf_ref, vf_ref, zf_ref, of_ref), (qb_ref, vb_ref, zb_ref, ob_ref))
    masks, lbs = [], []
    for d in range(2):
        lvl = lvl_ref[d]
        masks.append([lvl == j for j in range(HGRN_LEVELS + 1)])
        p0, p1 = lbp_ref[d, 0:1, :], lbp_ref[d, 1:2, :]
        pm = jnp.maximum(p0, p1)
        e0 = jnp.exp(p0 - pm)
        lbs.append(e0 / (e0 + jnp.exp(p1 - pm)))
    states = [[st_ref[d, hh] for hh in range(HEADS)] for d in range(2)]
    for sub in range(n_sub):
        for d, (q_ref, v_ref, z_ref, o_ref) in enumerate(dirs):
            c = n_sub - 1 - sub if d == 1 else sub
            tok = slice(c * C, (c + 1) * C)
            for hh in range(HEADS):
                cols = slice(hh * HEAD_W, (hh + 1) * HEAD_W)
                o, states[d][hh] = _hgrn_chain(q_ref[0, tok, cols], z_ref[0, tok, cols], v_ref[0, tok, cols],
                                               lbs[d][:, cols], states[d][hh], masks[d], reverse=(d == 1))
                o_ref[0, tok, cols] = o.astype(o_ref.dtype)
    for d in range(2):
        for hh in range(HEADS):
            st_ref[d, hh] = states[d][hh]

    @pl.when(i == pl.num_programs(1) - 1)
    def _():
        sfin_ref[0] = st_ref[...]


def _hgrn2_bidir(hq, hi, hff, hfb, lb_param, s0, n_sub):
    B, T, _ = hq.shape
    rows = HGRN_CHUNK * n_sub
    n = T // rows
    assert n * rows == T
    fwd = pl.BlockSpec((1, rows, BRANCH_W), lambda b, i: (b, i, 0))
    bwd = pl.BlockSpec((1, rows, BRANCH_W), lambda b, i: (b, n - 1 - i, 0))
    st_spec = pl.BlockSpec((1, 2, HEADS, HEAD_W, HEAD_W), lambda b, i: (b, 0, 0, 0, 0))
    lvl = jnp.asarray(_level_maps())
    return pl.pallas_call(
        functools.partial(_hgrn_kernel, n_sub, s0 is not None),
        grid=(B, n),
        in_specs=[fwd, fwd, fwd, bwd, bwd, bwd,
                  pl.BlockSpec(lb_param.shape, lambda b, i: (0, 0, 0)),
                  pl.BlockSpec(lvl.shape, lambda b, i: (0, 0, 0))] + ([] if s0 is None else [st_spec]),
        out_specs=[fwd, bwd, st_spec],
        out_shape=[jax.ShapeDtypeStruct((B, T, BRANCH_W), jnp.bfloat16),
                   jax.ShapeDtypeStruct((B, T, BRANCH_W), jnp.bfloat16),
                   jax.ShapeDtypeStruct((B, 2, HEADS, HEAD_W, HEAD_W), jnp.float32)],
        scratch_shapes=[pltpu.VMEM((2, HEADS, HEAD_W, HEAD_W), jnp.float32)],
        compiler_params=pltpu.CompilerParams(
            dimension_semantics=("parallel", "arbitrary"), vmem_limit_bytes=V7X_VMEM_LIMIT_BYTES),
        name="hgrn2_bidir",
    )(hq, hi, hff, hq, hi, hfb, lb_param, lvl, *([] if s0 is None else [s0]))


def _merge_kernel(att_ref, of_ref, ob_ref, hg_ref, x_ref, gate_ref, w_ref, hgain_ref, lng_ref, lnb_ref, o_ref):
    f32 = jnp.float32
    att = att_ref[0]
    o = of_ref[0].astype(f32) + ob_ref[0].astype(f32)
    parts = []
    for hh in range(HEADS):
        oh = o[:, hh * HEAD_W:(hh + 1) * HEAD_W]
        parts.append(oh * lax.rsqrt(jnp.mean(oh * oh, axis=-1, keepdims=True) + EPS) * hgain_ref[...])
    hg = jnp.concatenate(parts, axis=1) * _silu(hg_ref[0].astype(f32))
    y_in = jnp.concatenate([att, hg.astype(jnp.bfloat16)], axis=1)
    y = jnp.dot(y_in, w_ref[...], preferred_element_type=jnp.float32)
    u = ALPHA * x_ref[0] + gate_ref[0] * y
    mu = jnp.mean(u, axis=-1, keepdims=True)
    uc = u - mu
    var = jnp.mean(uc * uc, axis=-1, keepdims=True)
    o_ref[0] = uc * lax.rsqrt(var + EPS) * lng_ref[...] + lnb_ref[...]


def _merge(att, o_f, o_b, hg, x, gate, w_out_bf, hgain, ln_g, ln_b, tm):
    B, N, D = x.shape
    half = pl.BlockSpec((1, tm, BRANCH_W), lambda b, i: (b, i, 0))
    full = pl.BlockSpec((1, tm, D), lambda b, i: (b, i, 0))
    row = lambda w: pl.BlockSpec((1, w), lambda b, i: (0, 0))
    return pl.pallas_call(
        _merge_kernel,
        grid=(B, N // tm),
        in_specs=[half, half, half, half, full,
                  pl.BlockSpec((1, 1, D), lambda b, i: (b, 0, 0)),
                  pl.BlockSpec(w_out_bf.shape, lambda b, i: (0, 0)),
                  row(HEAD_W), row(D), row(D)],
        out_specs=full,
        out_shape=jax.ShapeDtypeStruct((B, N, D), jnp.float32),
        compiler_params=pltpu.CompilerParams(
            dimension_semantics=("parallel", "parallel"), vmem_limit_bytes=V7X_VMEM_LIMIT_BYTES),
        name="merge_out_proj_ln",
    )(att, o_f, o_b, hg, x, gate, w_out_bf, hgain, ln_g, ln_b)


def kernel(x, c, ctx, c_ctx, w_ada, b_ada, w_in, w_out, diff_lambda, diff_subln_gain, hgrn_lower_bound,
           hgrn_norm_gain, ln_gain, ln_bias):
    B, N, D = x.shape
    assert DEPTH == 1 and w_ada.shape[0] == 1
    cvec = jnp.concatenate([c, c_ctx[None, :], jnp.zeros((SUBLANES - B - 1, D), c.dtype)], axis=0)
    mod = _modulation(cvec, w_ada, b_ada)
    shift, scale, gate = mod[:, :D], mod[:, D:2 * D], mod[:, 2 * D:]
    s1p = (1.0 + scale)[:, None, :]
    shift = shift[:, None, :]
    ctx_rows = jnp.full((B,), B, jnp.int32)

    k_c, v_c, hq_c, hi_c, hff_c, hfb_c = _in_projection(
        ctx, s1p[ctx_rows], shift[ctx_rows], w_in, _CTX_GROUPS, tm=ctx.shape[1])
    q, k, v, ag, hq, hi, hff, hfb, hg = _in_projection(
        x, s1p[:B], shift[:B], w_in, _LATENT_GROUPS, tm=ROW_TILE, rope_tables=_rope_tables(N))

    att = _diff_attention(q, k_c, v_c, k, v, diff_lambda, diff_subln_gain, ag)

    _, _, s_ctx = _hgrn2_bidir(hq_c, hi_c, hff_c, hfb_c, hgrn_lower_bound, None, n_sub=ctx.shape[1] // HGRN_CHUNK)
    o_f, o_b, _ = _hgrn2_bidir(hq, hi, hff, hfb, hgrn_lower_bound, s_ctx, n_sub=HGRN_CHUNKS_PER_STEP)

    return _merge(att, o_f, o_b, hg, x, gate[:B, None, :], w_out[0].astype(jnp.bfloat16),
                  hgrn_norm_gain, ln_gain, ln_bias, tm=MERGE_TILE)
```

```python
import functools
import math

import numpy as np
import jax
import jax.numpy as jnp
from jax import lax
from jax.experimental import pallas as pl
from jax.experimental.pallas import tpu as pltpu

DEPTH = 1
GRID_W = 64
HEADS = 4
HEAD_W = 128
MAP_D = 64
BRANCH_W = HEADS * HEAD_W
ROPE_BASE = 10000.0
ROPE_FREQS = MAP_D // 4
EPS = 1e-5
LAMBDA_INIT = 0.8 - 0.6 * math.exp(-0.3 * 0)
ALPHA = (2.0 * DEPTH) ** 0.25

V7X_VMEM_LIMIT_BYTES = 60 * 1024 * 1024
SUBLANES = 8
LANES = 128
SUM_ROWS = 16
ROW_TILE = 1024
MERGE_TILE = 1024
ATTN_TQ = 2048
ATTN_MAX_TK = 1408
HGRN_CHUNKS_PER_STEP = 4
HGRN_CHUNK = 128
HGRN_LEVELS = 7

_NT = (((1,), (1,)), ((), ()))
_TN = (((0,), (0,)), ((), ()))


def _silu(x):
    return x * jax.nn.sigmoid(x)


def _mod_kernel(c_ref, w_ref, b_ref, o_ref):
    a = _silu(c_ref[...]).astype(jnp.bfloat16)
    o_ref[...] = jnp.dot(a, w_ref[0].astype(jnp.bfloat16), preferred_element_type=jnp.float32) + b_ref[...]


def _modulation(cvec, w_ada, b_ada):
    rows, d = cvec.shape
    n_out = w_ada.shape[2]
    bn = 1024
    return pl.pallas_call(
        _mod_kernel,
        grid=(n_out // bn,),
        in_specs=[pl.BlockSpec((rows, d), lambda j: (0, 0)),
                  pl.BlockSpec((1, d, bn), lambda j: (0, 0, j)),
                  pl.BlockSpec((1, bn), lambda j: (0, j))],
        out_specs=pl.BlockSpec((rows, bn), lambda j: (0, j)),
        out_shape=jax.ShapeDtypeStruct((rows, n_out), jnp.float32),
        compiler_params=pltpu.CompilerParams(vmem_limit_bytes=V7X_VMEM_LIMIT_BYTES),
        name="modulation",
    )(cvec, w_ada, b_ada)


_LATENT_GROUPS = ((0, "rope_q", jnp.bfloat16), (1, "rope", jnp.bfloat16), (2, "transposed", jnp.bfloat16),
                  (3, "plain", jnp.bfloat16), (4, "silu", jnp.float32), (5, "plain", jnp.bfloat16),
                  (6, "plain", jnp.float32), (7, "plain", jnp.float32), (8, "plain", jnp.bfloat16))
_CTX_GROUPS = ((1, "plain", jnp.bfloat16), (2, "transposed", jnp.bfloat16), (4, "silu", jnp.float32),
               (5, "plain", jnp.bfloat16), (6, "plain", jnp.float32), (7, "plain", jnp.float32))
Q_SCALE = math.log2(math.e) / math.sqrt(MAP_D)


def _proj_kernel(groups, use_rope, *refs):
    if use_rope:
        x_ref, s1p_ref, sh_ref, w_ref, cos_row_ref, sin_row_ref, cos_col_ref, sin_col_ref = refs[:8]
        out_refs = refs[8:]
    else:
        x_ref, s1p_ref, sh_ref, w_ref = refs[:4]
        out_refs = refs[4:]
    h = (x_ref[0] * s1p_ref[0] + sh_ref[0]).astype(jnp.bfloat16)
    if use_rope:
        by_row = (lax.broadcasted_iota(jnp.int32, (GRID_W, HEAD_W), 1) & (MAP_D - 1)) < 2 * ROPE_FREQS

        def token_table(row_ref, col_ref):
            grid_rows = [jnp.where(by_row, row_ref[r:r + 1, :], col_ref[...]) for r in range(row_ref.shape[0])]
            return jnp.concatenate([jnp.concatenate(grid_rows, axis=0)] * HEADS, axis=1)

        cos = token_table(cos_row_ref, cos_col_ref)
        sin = token_table(sin_row_ref, sin_col_ref)
        lane = lax.broadcasted_iota(jnp.int32, cos.shape, 1)
        first_half = (lane & (2 * ROPE_FREQS - 1)) < ROPE_FREQS
    for (g, kind, dt), o_ref in zip(groups, out_refs):
        p = jnp.dot(h, w_ref[0, :, g * BRANCH_W:(g + 1) * BRANCH_W].astype(jnp.bfloat16),
                    preferred_element_type=jnp.float32)
        if kind in ("rope", "rope_q"):
            partner = jnp.where(first_half,
                                pltpu.roll(p, BRANCH_W - ROPE_FREQS, axis=1),
                                pltpu.roll(p, ROPE_FREQS, axis=1))
            p = p * cos + partner * sin
            if kind == "rope_q":
                p = p * Q_SCALE
        elif kind == "silu":
            p = _silu(p)
        elif kind == "transposed":
            p = p.T
        o_ref[0] = p.astype(dt)


def _in_projection(x, s1p, sh, w_in, groups, tm, rope_tables=None):
    B, T, D = x.shape
    use_rope = rope_tables is not None
    in_specs = [pl.BlockSpec((1, tm, D), lambda b, i: (b, i, 0)),
                pl.BlockSpec((1, 1, D), lambda b, i: (b, 0, 0)),
                pl.BlockSpec((1, 1, D), lambda b, i: (b, 0, 0)),
                pl.BlockSpec(w_in.shape, lambda b, i: (0, 0, 0), pipeline_mode=pl.Buffered(1))]
    args = [x, s1p, sh, w_in]
    if use_rope:
        assert tm % (GRID_W * SUBLANES) == 0
        in_specs += [pl.BlockSpec((tm // GRID_W, HEAD_W), lambda b, i: (i, 0))] * 2
        in_specs += [pl.BlockSpec((GRID_W, HEAD_W), lambda b, i: (0, 0))] * 2
        args += list(rope_tables)
    out_specs = [pl.BlockSpec((1, BRANCH_W, tm), lambda b, i: (b, 0, i)) if kind == "transposed"
                 else pl.BlockSpec((1, tm, BRANCH_W), lambda b, i: (b, i, 0)) for _, kind, _ in groups]
    out_shape = [jax.ShapeDtypeStruct((B, BRANCH_W, T) if kind == "transposed" else (B, T, BRANCH_W), dt)
                 for _, kind, dt in groups]
    return pl.pallas_call(
        functools.partial(_proj_kernel, groups, use_rope),
        grid=(B, T // tm),
        in_specs=in_specs, out_specs=out_specs, out_shape=out_shape,
        compiler_params=pltpu.CompilerParams(
            dimension_semantics=("parallel", "parallel"), vmem_limit_bytes=V7X_VMEM_LIMIT_BYTES),
        name="in_proj_rope" if use_rope else "in_proj_ctx",
    )(*args)


def _rope_tables(n_tokens):
    inv_freq = jnp.asarray(ROPE_BASE, jnp.float32) ** (-jnp.arange(ROPE_FREQS, dtype=jnp.float32) / ROPE_FREQS)
    freq = jnp.tile(inv_freq, HEAD_W // ROPE_FREQS)
    sign = np.where((np.arange(HEAD_W) % (2 * ROPE_FREQS)) < ROPE_FREQS, -1.0, 1.0).astype(np.float32)
    row_ang = jnp.arange(n_tokens // GRID_W, dtype=jnp.float32)[:, None] * freq
    col_ang = jnp.arange(GRID_W, dtype=jnp.float32)[:, None] * freq
    return jnp.cos(row_ang), jnp.sin(row_ang) * sign, jnp.cos(col_ang), jnp.sin(col_ang) * sign


def _attn_kernel(tk, q_ref, kc_ref, vtc_ref, kl_ref, vtl_ref, lam_ref, gain_ref, o_ref,
                 k_ref, vt_ref, s_ref, p_ref, acc_ref):
    t_ctx = kc_ref.shape[1]

    @pl.when(pl.program_id(2) == 0)
    def _():
        k_ref[:t_ctx, :] = kc_ref[0]
        k_ref[t_ctx:, :] = kl_ref[0]
        vt_ref[:HEAD_W, :t_ctx] = vtc_ref[0]
        vt_ref[:HEAD_W, t_ctx:] = vtl_ref[0]
        extra = lax.broadcasted_iota(jnp.int32, (SUM_ROWS, vt_ref.shape[1]), 0) == 0
        vt_ref[HEAD_W:, :] = jnp.where(extra, 1.0, 0.0).astype(vt_ref.dtype)

    qt = q_ref[0].astype(jnp.float32).T
    row = lax.broadcasted_iota(jnp.int32, qt.shape, 0)
    qm = tuple(jnp.where(sel, qt, 0.0).astype(jnp.bfloat16) for sel in (row < MAP_D, row >= MAP_D))
    tq = qt.shape[1]
    n_chunks = k_ref.shape[0] // tk

    def scores(j):
        kb = k_ref[pl.ds(pl.multiple_of(j * tk, tk), tk), :]
        smax = []
        for m in range(2):
            s = jnp.dot(kb, qm[m], preferred_element_type=jnp.float32)
            s_ref[m] = s
            smax.append(jnp.max(s, axis=0, keepdims=True))
        return tuple(smax)

    def weights(smax, mx):
        new_mx, corrs = [], []
        for m in range(2):
            mx_new = jnp.maximum(mx[m], smax[m])
            p_ref[m] = jnp.exp2(s_ref[m] - mx_new).astype(jnp.bfloat16)
            new_mx.append(mx_new)
            corrs.append(jnp.exp2(mx[m] - mx_new))
        return tuple(new_mx), tuple(corrs)

    def values(j, corr):
        vtb = vt_ref[:, pl.ds(pl.multiple_of(j * tk, tk), tk)]
        for m in range(2):
            acc_ref[m] = acc_ref[m] * corr[m] + jnp.dot(vtb, p_ref[m], preferred_element_type=jnp.float32)

    mx = tuple(jnp.full((1, tq), -jnp.inf, jnp.float32) for _ in range(2))
    acc_ref[...] = jnp.zeros_like(acc_ref)
    mx, corr = weights(scores(0), mx)
    smax = scores(1)

    def body(j, carry):
        mx, smax, corr_prev = carry
        values(j - 1, corr_prev)
        mx, corr = weights(smax, mx)
        return mx, scores(j + 1), corr

    mx, smax, corr = lax.fori_loop(1, n_chunks - 1, body, (mx, smax, corr))
    values(n_chunks - 2, corr)
    mx, corr = weights(smax, mx)
    values(n_chunks - 1, corr)

    lp = lam_ref[0]
    lam = (jnp.exp(jnp.sum(lp[0:1] * lp[1:2], axis=-1, keepdims=True))
           - jnp.exp(jnp.sum(lp[2:3] * lp[3:4], axis=-1, keepdims=True)) + LAMBDA_INIT)
    l0, l1 = acc_ref[0, HEAD_W:HEAD_W + 1, :], acc_ref[1, HEAD_W:HEAD_W + 1, :]
    ot = acc_ref[0, :HEAD_W, :] / l0 - lam * (acc_ref[1, :HEAD_W, :] / l1)
    ot = ot * lax.rsqrt(jnp.mean(ot * ot, axis=0, keepdims=True) + EPS)
    o_ref[0] = (ot.T * gain_ref[...] * (1.0 - LAMBDA_INIT)).astype(o_ref.dtype)


def _key_chunk(n_keys):
    return max(t for t in range(LANES, ATTN_MAX_TK + 1, LANES) if n_keys % t == 0)


def _diff_attention(q, k_ctx, vt_ctx, k, vt, diff_lambda, gain):
    B, N, _ = q.shape
    T_ctx = k_ctx.shape[1]
    Tk = T_ctx + N
    tq, tk = ATTN_TQ, _key_chunk(Tk)
    assert N % tq == 0 and Tk // tk >= 3 and T_ctx % LANES == 0
    return pl.pallas_call(
        functools.partial(_attn_kernel, tk),
        grid=(B, HEADS, N // tq),
        in_specs=[pl.BlockSpec((1, tq, HEAD_W), lambda b, h, i: (b, i, h)),
                  pl.BlockSpec((1, T_ctx, HEAD_W), lambda b, h, i: (b, 0, h)),
                  pl.BlockSpec((1, HEAD_W, T_ctx), lambda b, h, i: (b, h, 0)),
                  pl.BlockSpec((1, N, HEAD_W), lambda b, h, i: (b, 0, h)),
                  pl.BlockSpec((1, HEAD_W, N), lambda b, h, i: (b, h, 0)),
                  pl.BlockSpec((1, 4, MAP_D), lambda b, h, i: (0, 0, 0)),
                  pl.BlockSpec((1, HEAD_W), lambda b, h, i: (0, 0))],
        out_specs=pl.BlockSpec((1, tq, HEAD_W), lambda b, h, i: (b, i, h)),
        out_shape=jax.ShapeDtypeStruct((B, N, BRANCH_W), jnp.bfloat16),
        scratch_shapes=[pltpu.VMEM((Tk, HEAD_W), jnp.bfloat16), pltpu.VMEM((HEAD_W + SUM_ROWS, Tk), jnp.bfloat16),
                        pltpu.VMEM((2, tk, tq), jnp.float32), pltpu.VMEM((2, tk, tq), jnp.bfloat16),
                        pltpu.VMEM((2, HEAD_W + SUM_ROWS, tq), jnp.float32)],
        compiler_params=pltpu.CompilerParams(
            dimension_semantics=("parallel", "parallel", "arbitrary"),
            vmem_limit_bytes=V7X_VMEM_LIMIT_BYTES),
        name="diff_attention",
    )(q, k_ctx, vt_ctx, k, vt, diff_lambda, gain)


def _level_maps():
    t = np.arange(HGRN_CHUNK)[:, None]
    s = np.arange(HGRN_CHUNK)[None, :]
    x = t ^ s
    lvl = np.where(x > 0, np.floor(np.log2(np.maximum(x, 1))).astype(np.int32) + 1, 0)
    lvl = np.where(s > t, -1, lvl).astype(np.int32)
    return np.stack([lvl, lvl.T])


def _shift_down(x, s):
    return pltpu.roll(x, s, axis=1)


def _shift_up(x, s):
    return pltpu.roll(x, SUBLANES - s, axis=1)


def _neg_abs(x):
    bits = lax.bitcast_convert_type(x, jnp.uint32) | jnp.uint32(0x80000000)
    return lax.bitcast_convert_type(bits, jnp.float32)


def _hgrn_chain(q, z, v_bf, lb, st, masks, reverse):
    C = HGRN_CHUNK
    groups = C // SUBLANES
    sig = jax.nn.sigmoid(z)
    g = jnp.log2(lb + (1.0 - lb) * sig)
    kk = (1.0 - lb) * (1.0 - sig)
    grouped = (groups, SUBLANES, HEAD_W)
    sub = lax.broadcasted_iota(jnp.int32, grouped, 1)

    bg = g.reshape(grouped)
    for s in (1, 2, 4):
        if reverse:
            bg = bg + jnp.where(sub < SUBLANES - s, _shift_up(bg, s), 0.0)
        else:
            bg = bg + jnp.where(sub >= s, _shift_down(bg, s), 0.0)
    parts = [bg[r] for r in range(groups)]
    order = range(groups - 2, -1, -1) if reverse else range(1, groups)
    for r in order:
        prev = parts[r + 1][0:1, :] if reverse else parts[r - 1][SUBLANES - 1:SUBLANES, :]
        parts[r] = parts[r] + prev
    b = jnp.concatenate(parts, axis=0)
    bg = b.reshape(grouped)

    q_bf = q.astype(jnp.bfloat16)
    kk_bf = kk.astype(jnp.bfloat16)

    def rows(x, r):
        return x[r * SUBLANES:(r + 1) * SUBLANES, :]

    a_parts = [None] * groups

    own = bg
    nbr = _shift_up(bg, 1) if reverse else _shift_down(bg, 1)
    for j in range(1, HGRN_LEVELS + 1):
        h = 1 << (j - 1)
        if 2 * h <= SUBLANES:
            upper = (sub & h) != 0
            ref = jnp.where(upper, own, nbr) if reverse else jnp.where(upper, nbr, own)
            w = jnp.exp2(_neg_abs(bg - ref)).reshape(C, HEAD_W).astype(jnp.bfloat16)
            if 4 * h <= SUBLANES:
                if reverse:
                    own = jnp.where(upper, _shift_down(own, h), own)
                    nbr = jnp.where(upper, nbr, _shift_up(nbr, h))
                else:
                    own = jnp.where(upper, own, _shift_up(own, h))
                    nbr = jnp.where(upper, _shift_down(nbr, h), nbr)
            pm = lax.dot_general(q_bf * w, kk_bf * w, _NT, preferred_element_type=jnp.float32)
            a_parts = [jnp.where(rows(masks[j], r), rows(pm, r), 0.0 if a_parts[r] is None else a_parts[r])
                       for r in range(groups)]
        else:
            n_blk, half_groups = C // (2 * h), h // SUBLANES
            blocked = (n_blk, 2 * h, HEAD_W)
            b_blk = b.reshape(blocked)
            ref = b_blk[:, h:h + 1, :] if reverse else b_blk[:, h - 1:h, :]
            w_blk = jnp.exp2(_neg_abs(b_blk - ref))
            q_half, k_half = (slice(0, h), slice(h, 2 * h)) if reverse else (slice(h, 2 * h), slice(0, h))
            q_t = (q.reshape(blocked)[:, q_half] * w_blk[:, q_half]).reshape(C // 2, HEAD_W)
            k_t = kk.reshape(blocked)[:, k_half] * w_blk[:, k_half]
            zeros = jnp.zeros_like(k_t)
            k_t = jnp.concatenate([zeros, k_t] if reverse else [k_t, zeros], axis=1).reshape(C, HEAD_W)
            pm = lax.dot_general(q_t.astype(jnp.bfloat16), k_t.astype(jnp.bfloat16), _NT,
                                 preferred_element_type=jnp.float32)
            for blk in range(n_blk):
                for i in range(half_groups):
                    r = blk * 2 * half_groups + (0 if reverse else half_groups) + i
                    piece = rows(pm, blk * half_groups + i)
                    if n_blk == 1:
                        a_parts[r] = a_parts[r] + piece
                    else:
                        a_parts[r] = jnp.where(rows(masks[j], r), piece, a_parts[r])
    a = jnp.concatenate(a_parts, axis=0)

    o = jnp.dot(a.astype(jnp.bfloat16), v_bf, preferred_element_type=jnp.float32)
    o = o + lax.dot_general(q_bf * jnp.exp2(b).astype(jnp.bfloat16), st.astype(jnp.bfloat16), _NT,
                            preferred_element_type=jnp.float32)
    o = o + jnp.sum(q * kk, axis=-1, keepdims=True) * v_bf.astype(jnp.float32)
    b_tot = b[0:1, :] if reverse else b[C - 1:C, :]
    k_hat = kk_bf * jnp.exp2(b_tot - b).astype(jnp.bfloat16)
    st_new = st * jnp.exp2(b_tot) + lax.dot_general(v_bf, k_hat, _TN, preferred_element_type=jnp.float32)
    return o, st_new


def _hgrn_kernel(n_sub, has_s0, qf_ref, vf_ref, zf_ref, qb_ref, vb_ref, zb_ref, lbp_ref, lvl_ref, *refs):
    s0_ref, (of_ref, ob_ref, sfin_ref, st_ref) = (refs[0], refs[1:]) if has_s0 else (None, refs)
    i = pl.program_id(1)
    C = HGRN_CHUNK

    @pl.when(i == 0)
    def _():
        st_ref[...] = jnp.zeros_like(st_ref) if s0_ref is None else s0_ref[0]

    dirs = ((qf_ref, vf_ref, zf_ref, of_ref), (qb_ref, vb_ref, zb_ref, ob_ref))
    masks, lbs = [], []
    for d in range(2):
        lvl = lvl_ref[d]
        masks.append([lvl == j for j in range(HGRN_LEVELS + 1)])
        p0, p1 = lbp_ref[d, 0:1, :], lbp_ref[d, 1:2, :]
        pm = jnp.maximum(p0, p1)
        e0 = jnp.exp(p0 - pm)
        lbs.append(e0 / (e0 + jnp.exp(p1 - pm)))
    states = [[st_ref[d, hh] for hh in range(HEADS)] for d in range(2)]
    for sub in range(n_sub):
        for d, (q_ref, v_ref, z_ref, o_ref) in enumerate(dirs):
            c = n_sub - 1 - sub if d == 1 else sub
            tok = slice(c * C, (c + 1) * C)
            for hh in range(HEADS):
                cols = slice(hh * HEAD_W, (hh + 1) * HEAD_W)
                o, states[d][hh] = _hgrn_chain(q_ref[0, tok, cols], z_ref[0, tok, cols], v_ref[0, tok, cols],
                                               lbs[d][:, cols], states[d][hh], masks[d], reverse=(d == 1))
                o_ref[0, tok, cols] = o.astype(o_ref.dtype)
    for d in range(2):
        for hh in range(HEADS):
            st_ref[d, hh] = states[d][hh]

    @pl.when(i == pl.num_programs(1) - 1)
    def _():
        sfin_ref[0] = st_ref[...]


def _hgrn2_bidir(hq, hi, hff, hfb, lb_param, s0, n_sub):
    B, T, _ = hq.shape
    rows = HGRN_CHUNK * n_sub
    n = T // rows
    assert n * rows == T
    fwd = pl.BlockSpec((1, rows, BRANCH_W), lambda b, i: (b, i, 0))
    bwd = pl.BlockSpec((1, rows, BRANCH_W), lambda b, i: (b, n - 1 - i, 0))
    st_spec = pl.BlockSpec((1, 2, HEADS, HEAD_W, HEAD_W), lambda b, i: (b, 0, 0, 0, 0))
    lvl = jnp.asarray(_level_maps())
    return pl.pallas_call(
        functools.partial(_hgrn_kernel, n_sub, s0 is not None),
        grid=(B, n),
        in_specs=[fwd, fwd, fwd, bwd, bwd, bwd,
                  pl.BlockSpec(lb_param.shape, lambda b, i: (0, 0, 0)),
                  pl.BlockSpec(lvl.shape, lambda b, i: (0, 0, 0))] + ([] if s0 is None else [st_spec]),
        out_specs=[fwd, bwd, st_spec],
        out_shape=[jax.ShapeDtypeStruct((B, T, BRANCH_W), jnp.bfloat16),
                   jax.ShapeDtypeStruct((B, T, BRANCH_W), jnp.bfloat16),
                   jax.ShapeDtypeStruct((B, 2, HEADS, HEAD_W, HEAD_W), jnp.float32)],
        scratch_shapes=[pltpu.VMEM((2, HEADS, HEAD_W, HEAD_W), jnp.float32)],
        compiler_params=pltpu.CompilerParams(
            dimension_semantics=("parallel", "arbitrary"), vmem_limit_bytes=V7X_VMEM_LIMIT_BYTES),
        name="hgrn2_bidir",
    )(hq, hi, hff, hq, hi, hfb, lb_param, lvl, *([] if s0 is None else [s0]))


def _merge_kernel(att_ref, ag_ref, of_ref, ob_ref, hg_ref, x_ref, gate_ref, w_ref, hgain_ref, lng_ref, lnb_ref,
                  o_ref):
    f32 = jnp.float32
    att = att_ref[0].astype(f32) * _silu(ag_ref[0].astype(f32))
    o = of_ref[0].astype(f32) + ob_ref[0].astype(f32)
    parts = []
    for hh in range(HEADS):
        oh = o[:, hh * HEAD_W:(hh + 1) * HEAD_W]
        parts.append(oh * lax.rsqrt(jnp.mean(oh * oh, axis=-1, keepdims=True) + EPS) * hgain_ref[...])
    hg = jnp.concatenate(parts, axis=1) * _silu(hg_ref[0].astype(f32))
    y_in = jnp.concatenate([att, hg], axis=1).astype(jnp.bfloat16)
    y = jnp.dot(y_in, w_ref[...], preferred_element_type=jnp.float32)
    u = ALPHA * x_ref[0] + gate_ref[0] * y
    mu = jnp.mean(u, axis=-1, keepdims=True)
    uc = u - mu
    var = jnp.mean(uc * uc, axis=-1, keepdims=True)
    o_ref[0] = uc * lax.rsqrt(var + EPS) * lng_ref[...] + lnb_ref[...]


def _merge(att, ag, o_f, o_b, hg, x, gate, w_out_bf, hgain, ln_g, ln_b, tm):
    B, N, D = x.shape
    half = pl.BlockSpec((1, tm, BRANCH_W), lambda b, i: (b, i, 0))
    full = pl.BlockSpec((1, tm, D), lambda b, i: (b, i, 0))
    row = lambda w: pl.BlockSpec((1, w), lambda b, i: (0, 0))
    return pl.pallas_call(
        _merge_kernel,
        grid=(B, N // tm),
        in_specs=[half, half, half, half, half, full,
                  pl.BlockSpec((1, 1, D), lambda b, i: (b, 0, 0)),
                  pl.BlockSpec(w_out_bf.shape, lambda b, i: (0, 0)),
                  row(HEAD_W), row(D), row(D)],
        out_specs=full,
        out_shape=jax.ShapeDtypeStruct((B, N, D), jnp.float32),
        compiler_params=pltpu.CompilerParams(
            dimension_semantics=("parallel", "parallel"), vmem_limit_bytes=V7X_VMEM_LIMIT_BYTES),
        name="merge_out_proj_ln",
    )(att, ag, o_f, o_b, hg, x, gate, w_out_bf, hgain, ln_g, ln_b)


def kernel(x, c, ctx, c_ctx, w_ada, b_ada, w_in, w_out, diff_lambda, diff_subln_gain, hgrn_lower_bound,
           hgrn_norm_gain, ln_gain, ln_bias):
    B, N, D = x.shape
    assert DEPTH == 1 and w_ada.shape[0] == 1
    cvec = jnp.concatenate([c, c_ctx[None, :], jnp.zeros((SUBLANES - B - 1, D), c.dtype)], axis=0)
    mod = _modulation(cvec, w_ada, b_ada)
    shift, scale, gate = mod[:, :D], mod[:, D:2 * D], mod[:, 2 * D:]
    s1p = (1.0 + scale)[:, None, :]
    shift = shift[:, None, :]
    ctx_rows = jnp.full((B,), B, jnp.int32)

    k_c, v_c, hq_c, hi_c, hff_c, hfb_c = _in_projection(
        ctx, s1p[ctx_rows], shift[ctx_rows], w_in, _CTX_GROUPS, tm=ctx.shape[1])
    q, k, v, ag, hq, hi, hff, hfb, hg = _in_projection(
        x, s1p[:B], shift[:B], w_in, _LATENT_GROUPS, tm=ROW_TILE, rope_tables=_rope_tables(N))

    att = _diff_attention(q, k_c, v_c, k, v, diff_lambda, diff_subln_gain)

    _, _, s_ctx = _hgrn2_bidir(hq_c, hi_c, hff_c, hfb_c, hgrn_lower_bound, None, n_sub=ctx.shape[1] // HGRN_CHUNK)
    o_f, o_b, _ = _hgrn2_bidir(hq, hi, hff, hfb, hgrn_lower_bound, s_ctx, n_sub=HGRN_CHUNKS_PER_STEP)

    return _merge(att, ag, o_f, o_b, hg, x, gate[:B, None, :], w_out[0].astype(jnp.bfloat16),
                  hgrn_norm_gain, ln_gain, ln_bias, tm=MERGE_TILE)
```
